```python
import jax, jax.numpy as jnp
from jax import lax
import numpy as np

D_MODEL = 1024
BATCH = 32
SEQ = 256
DEPTH = 4
DEC_BATCH = 8
DEC_SEQ = 2048
PAST_LEN = 512

GRID_W = 64
HEAD_DIM = 64
ATT_DIM = D_MODEL // 2
RWKV_DIM = D_MODEL // 4
CONV_DIM = D_MODEL // 4
MIX_DIM = ATT_DIM + RWKV_DIM + CONV_DIM
ATT_HEADS = ATT_DIM // HEAD_DIM
ATT_KV_HEADS = 2
ATT_GROUP = ATT_HEADS // ATT_KV_HEADS
KV_DIM = ATT_KV_HEADS * HEAD_DIM
WINDOW = 128
ATT_BLOCK = 128
ROPE_THETA = 10000.0
RWKV_HEADS = RWKV_DIM // HEAD_DIM
W_RANK = 32
A_RANK = 32
G_RANK = 64
RWKV_LNX_EPS = 64e-5
CONV_WIDTH = 3
CONV_GROUPS = CONV_DIM // HEAD_DIM
PEER_HEADS = 8
PEER_NKEYS = 128
PEER_EXPERTS = PEER_NKEYS * PEER_NKEYS
PEER_TOPK = 16
PEER_DKEY = 256
PEER_HALF = PEER_DKEY // 2
PEER_CHUNK = 128
LN_EPS = 1e-5
ALPHA = (2 * DEPTH) ** 0.25
BETA = (8 * DEPTH) ** -0.25
NEG = -1e30
IN_WIDTHS = (ATT_DIM, KV_DIM, KV_DIM,
             RWKV_DIM, RWKV_DIM, RWKV_DIM, 2 * W_RANK, 2 * A_RANK, G_RANK,
             CONV_DIM, CONV_DIM, CONV_DIM)
IN_TOTAL = sum(IN_WIDTHS)
SPLIT_POINTS = tuple(int(s) for s in np.cumsum(IN_WIDTHS)[:-1])

kernel_name = 'hybrid_diffusion_rwkv7_conv_swa_peer_step'


def layer_norm(x, g, b, eps=LN_EPS):
    xf = x.astype(jnp.float32)
    mu = jnp.mean(xf, axis=-1, keepdims=True)
    var = jnp.mean(jnp.square(xf - mu), axis=-1, keepdims=True)
    return ((xf - mu) * lax.rsqrt(var + eps)).astype(x.dtype) * g + b


def _rope_axis(x, pos):
    half = x.shape[-1] // 2
    freqs = ROPE_THETA ** (-jnp.arange(half, dtype=jnp.float32) / half)
    ang = pos.astype(jnp.float32)[:, None] * freqs[None, :]
    cos = jnp.cos(ang)[None, :, None, :]
    sin = jnp.sin(ang)[None, :, None, :]
    x1, x2 = x[..., :half], x[..., half:]
    return jnp.concatenate([x1 * cos - x2 * sin, x1 * sin + x2 * cos], axis=-1).astype(x.dtype)


def rope_2d(x):
    n = x.shape[1]
    rows = n // GRID_W
    t = jnp.arange(rows * GRID_W)
    row, col = t // GRID_W, t % GRID_W
    ax = x.shape[-1] // 2
    return jnp.concatenate([_rope_axis(x[..., :ax], row), _rope_axis(x[..., ax:], col)], axis=-1)


def context_attention(q, k, v, sink):
    b, c = q.shape[:2]
    qg = q.reshape(b, c, ATT_KV_HEADS, ATT_GROUP, HEAD_DIM)
    s = jnp.einsum('bqhgd,bkhd->bhgqk', qg, k).astype(jnp.float32) * (HEAD_DIM ** -0.5)
    sink_l = jnp.broadcast_to(sink.astype(jnp.float32).reshape(1, ATT_KV_HEADS, ATT_GROUP, 1, 1), s.shape[:-1] + (1,))
    p = jax.nn.softmax(jnp.concatenate([sink_l, s], axis=-1), axis=-1)[..., 1:].astype(v.dtype)
    out = jnp.einsum('bhgqk,bkhd->bqhgd', p, v)
    return out.reshape(b, c, ATT_DIM)


def window_attention(q, k, v, ck, cv, sink):
    b, s = q.shape[:2]
    nb = s // ATT_BLOCK
    c = ck.shape[1]
    qb = q.reshape(b, nb, ATT_BLOCK, ATT_KV_HEADS, ATT_GROUP, HEAD_DIM)

    def band(t):
        tb = t.reshape(b, nb, ATT_BLOCK, ATT_KV_HEADS, HEAD_DIM)
        tp = jnp.pad(tb, ((0, 0), (1, 1), (0, 0), (0, 0), (0, 0)))
        return jnp.concatenate([tp[:, :-2], tp[:, 1:-1], tp[:, 2:]], axis=2)

    kw, vw = band(k), band(v)
    scale = HEAD_DIM ** -0.5
    s_win = jnp.einsum('bnqhgd,bnkhd->bnhgqk', qb, kw).astype(jnp.float32) * scale
    s_ctx = jnp.einsum('bnqhgd,bchd->bnhgqc', qb, ck).astype(jnp.float32) * scale
    qi = jnp.arange(ATT_BLOCK)[:, None]
    kj = jnp.arange(3 * ATT_BLOCK)[None, :]
    rel = kj - ATT_BLOCK - qi
    kpos = (jnp.arange(nb)[:, None, None] - 1) * ATT_BLOCK + kj[None]
    mask = (jnp.abs(rel) <= WINDOW)[None] & (kpos >= 0) & (kpos < s)
    s_win = jnp.where(mask[None, :, None, None], s_win, NEG)
    sink_l = jnp.broadcast_to(sink.astype(jnp.float32).reshape(1, 1, ATT_KV_HEADS, ATT_GROUP, 1, 1), s_ctx.shape[:-1] + (1,))
    p = jax.nn.softmax(jnp.concatenate([sink_l, s_ctx, s_win], axis=-1), axis=-1).astype(v.dtype)
    p_ctx, p_win = p[..., 1:1 + c], p[..., 1 + c:]
    out = jnp.einsum('bnhgqc,bchd->bnqhgd', p_ctx, cv) + jnp.einsum('bnhgqk,bnkhd->bnqhgd', p_win, vw)
    return out.reshape(b, s, ATT_DIM)


def rwkv_scan(r, w, k, v, a, b, s0, reverse):
    xs = tuple(jnp.moveaxis(z.astype(jnp.float32), 1, 0) for z in (r, w, k, v, a, b))

    def step(s, inp):
        r_t, w_t, k_t, v_t, a_t, b_t = inp
        sa = jnp.einsum('bhij,bhj->bhi', s, a_t)
        s = s * w_t[:, :, None, :] + sa[..., None] * b_t[:, :, None, :] + v_t[..., None] * k_t[:, :, None, :]
        return s, jnp.einsum('bhij,bhj->bhi', s, r_t)

    s_fin, ys = lax.scan(step, s0.astype(jnp.float32), xs, reverse=reverse)
    return jnp.moveaxis(ys, 0, 1), s_fin


def rwkv_time_mix(r, k, v, w_down, a_down, g_down, s0, w0, w_up, a0, a_up, g_up, k_k, k_a, r_k, lnx_g, lnx_b):
    bsz, t = r.shape[:2]

    def heads(z):
        return z.reshape(bsz, t, RWKV_HEADS, HEAD_DIM)

    kk = heads(k * k_k).astype(jnp.float32)
    kk = kk * lax.rsqrt(jnp.maximum(jnp.sum(kk * kk, axis=-1, keepdims=True), 1e-24))
    rh, vh = heads(r), heads(v)
    rk_h = r_k.reshape(RWKV_HEADS, HEAD_DIM)
    ys, bonus, finals = [], [], []
    for d in range(2):
        w_log = -jax.nn.softplus(-(w0[d] + jnp.tanh(w_down[..., d * W_RANK:(d + 1) * W_RANK]) @ w_up[d])) - 0.5
        w = jnp.exp(-jnp.exp(w_log.astype(jnp.float32)))
        a = jax.nn.sigmoid(a0[d] + a_down[..., d * A_RANK:(d + 1) * A_RANK] @ a_up[d])
        kd = heads(k * (1.0 + (a - 1.0) * k_a))
        y, s_fin = rwkv_scan(rh, heads(w), kd, vh, -kk, kk * heads(a).astype(jnp.float32), s0[:, d], reverse=(d == 1))
        ys.append(y)
        finals.append(s_fin)
        bonus.append(jnp.sum(rh * kd * rk_h, axis=-1, keepdims=True) * vh)
    y = ys[0] + ys[1]
    mu = jnp.mean(y, axis=-1, keepdims=True)
    var = jnp.mean(jnp.square(y - mu), axis=-1, keepdims=True)
    yn = ((y - mu) * lax.rsqrt(var + RWKV_LNX_EPS)).reshape(bsz, t, RWKV_DIM).astype(r.dtype) * lnx_g + lnx_b
    g = jax.nn.sigmoid(g_down) @ g_up
    out = (yn + (bonus[0] + bonus[1]).reshape(bsz, t, RWKV_DIM)) * g
    return out, jnp.stack(finals, axis=1).astype(r.dtype)


def short_conv(bg, cg, u, w):
    z = cg * u
    zp = jnp.pad(z, ((0, 0), (1, 1), (0, 0)))
    y = w[0] * zp[:, :-2] + w[1] * zp[:, 1:-1] + w[2] * zp[:, 2:]
    return bg * y


def peer_ffn(h, wq, subkeys, u_tab, v_tab):
    b, s, d = h.shape
    xc = h.reshape(-1, PEER_CHUNK, d)

    def chunk(xt):
        q = (xt @ wq).reshape(PEER_CHUNK, PEER_HEADS, 2, PEER_HALF)
        sc = jnp.einsum('thpd,hpnd->thpn', q, subkeys).astype(jnp.float32)
        s_top, i_top = lax.top_k(sc, PEER_TOPK)
        cand = (s_top[:, :, 0, :, None] + s_top[:, :, 1, None, :]).reshape(PEER_CHUNK, PEER_HEADS, PEER_TOPK * PEER_TOPK)
        cidx = (i_top[:, :, 0, :, None] * PEER_NKEYS + i_top[:, :, 1, None, :]).reshape(PEER_CHUNK, PEER_HEADS, PEER_TOPK * PEER_TOPK)
        f_s, f_i = lax.top_k(cand, PEER_TOPK)
        e = jnp.take_along_axis(cidx, f_i, axis=-1)
        g = jax.nn.softmax(f_s, axis=-1).astype(xt.dtype)
        act = jax.nn.gelu(jnp.einsum('thkd,td->thk', u_tab[e], xt))
        return jnp.einsum('thk,thkd->td', g * act, v_tab[e])

    return lax.map(chunk, xc).reshape(b, s, d)


def trunk_layer(x, mod, ctx, w_in, w_out, sink, rw_w0, rw_w_up, rw_a0, rw_a_up, rw_g_up, rw_kk, rw_ka, rw_rk,
                rw_lnx_g, rw_lnx_b, conv_w, ln1_g, ln1_b, ln2_g, ln2_b, peer_wq, peer_subkeys, peer_u, peer_v):
    sh1, sc1, g1, sh2, sc2, g2 = jnp.split(mod, 6, axis=-1)
    b, s = x.shape[:2]
    h = x * (1.0 + sc1) + sh1
    p = h @ w_in
    aq, ak, av, rr, rk, rv, wd, ad, gd, cb, cc, cu = jnp.split(p, SPLIT_POINTS, axis=-1)
    aq = aq.reshape(b, s, ATT_HEADS, HEAD_DIM)
    ak = ak.reshape(b, s, ATT_KV_HEADS, HEAD_DIM)
    av = av.reshape(b, s, ATT_KV_HEADS, HEAD_DIM)
    if ctx is None:
        att = context_attention(aq, ak, av, sink)
        s0 = jnp.zeros((b, 2, RWKV_HEADS, HEAD_DIM, HEAD_DIM), jnp.float32)
    else:
        ck, cv, s0 = ctx
        att = window_attention(rope_2d(aq), rope_2d(ak), av, ck, cv, sink)
    rw, s_fin = rwkv_time_mix(rr, rk, rv, wd, ad, gd, s0, rw_w0, rw_w_up, rw_a0, rw_a_up, rw_g_up,
                              rw_kk, rw_ka, rw_rk, rw_lnx_g, rw_lnx_b)
    cvo = short_conv(cb, cc, cu, conv_w)
    mix = jnp.concatenate([att, rw, cvo], axis=-1) @ w_out
    x = layer_norm(ALPHA * x + g1 * mix, ln1_g, ln1_b)
    h = x * (1.0 + sc2) + sh2
    x = layer_norm(ALPHA * x + g2 * peer_ffn(h, peer_wq, peer_subkeys, peer_u, peer_v), ln2_g, ln2_b)
    if ctx is None:
        return x, (ak, av, s_fin)
    return x, None


def setup_inputs(seed: int = 0) -> dict:
    key = jax.random.key(seed)
    ks = jax.random.split(key, 40)
    f32 = jnp.float32
    D = D_MODEL

    def nrm(i, shape, scale=1.0):
        return scale * jax.random.normal(ks[i], shape, f32)

    return {
        'x_prompt': nrm(0, (BATCH, SEQ, D)),
        'x_sample': nrm(1, (DEC_BATCH, DEC_SEQ, D)),
        'cache_k': nrm(2, (DEC_BATCH, DEPTH, PAST_LEN, ATT_KV_HEADS, HEAD_DIM)),
        'cache_v': nrm(3, (DEC_BATCH, DEPTH, PAST_LEN, ATT_KV_HEADS, HEAD_DIM)),
        'state_rwkv': nrm(4, (DEC_BATCH, DEPTH, 2, RWKV_HEADS, HEAD_DIM, HEAD_DIM), 0.3),
        'c': nrm(5, (DEC_BATCH, D)),
        'c_ctx': nrm(6, (D,)),
        'ln_in_g': 1.0 + nrm(7, (D,), 0.02),
        'ln_in_b': nrm(8, (D,), 0.02),
        'w_ada': nrm(9, (DEPTH, D, 6 * D), 0.5 * D ** -0.5),
        'b_ada': nrm(10, (DEPTH, 6 * D), 0.02),
        'w_in': nrm(11, (DEPTH, D, IN_TOTAL), D ** -0.5),
        'w_out': nrm(12, (DEPTH, MIX_DIM, D), BETA * MIX_DIM ** -0.5),
        'att_sink': nrm(13, (DEPTH, ATT_HEADS), 0.5),
        'rw_w0': jax.random.uniform(ks[14], (DEPTH, 2, RWKV_DIM), f32, -6.0, 0.0),
        'rw_w_up': nrm(15, (DEPTH, 2, W_RANK, RWKV_DIM), 0.1),
        'rw_a0': nrm(16, (DEPTH, 2, RWKV_DIM), 0.5),
        'rw_a_up': nrm(17, (DEPTH, 2, A_RANK, RWKV_DIM), 0.5 * A_RANK ** -0.5),
        'rw_g_up': nrm(18, (DEPTH, G_RANK, RWKV_DIM), G_RANK ** -0.5),
        'rw_kk': 0.85 + nrm(19, (DEPTH, RWKV_DIM), 0.1),
        'rw_ka': 1.0 + nrm(20, (DEPTH, RWKV_DIM), 0.1),
        'rw_rk': nrm(21, (DEPTH, RWKV_DIM), 0.1),
        'rw_lnx_g': 1.0 + nrm(22, (DEPTH, RWKV_DIM), 0.02),
        'rw_lnx_b': nrm(23, (DEPTH, RWKV_DIM), 0.02),
        'conv_w': nrm(24, (DEPTH, CONV_WIDTH, CONV_DIM), CONV_WIDTH ** -0.5),
        'ln1_g': 1.0 + nrm(25, (DEPTH, D), 0.02),
        'ln1_b': nrm(26, (DEPTH, D), 0.02),
        'ln2_g': 1.0 + nrm(27, (DEPTH, D), 0.02),
        'ln2_b': nrm(28, (DEPTH, D), 0.02),
        'peer_wq': nrm(29, (DEPTH, D, PEER_HEADS * PEER_DKEY), D ** -0.5),
        'peer_subkeys': nrm(30, (DEPTH, PEER_HEADS, 2, PEER_NKEYS, PEER_HALF), PEER_HALF ** -0.5),
        'peer_u': nrm(31, (DEPTH, PEER_EXPERTS, D), D ** -0.5),
        'peer_v': nrm(32, (DEPTH, PEER_EXPERTS, D), BETA),
    }


def reference(x_prompt, x_sample, cache_k, cache_v, state_rwkv, c, c_ctx, ln_in_g, ln_in_b, w_ada, b_ada,
              w_in, w_out, att_sink, rw_w0, rw_w_up, rw_a0, rw_a_up, rw_g_up, rw_kk, rw_ka, rw_rk,
              rw_lnx_g, rw_lnx_b, conv_w, ln1_g, ln1_b, ln2_g, ln2_b, peer_wq, peer_subkeys, peer_u, peer_v):
    xp = layer_norm(x_prompt, ln_in_g, ln_in_b)
    xs = layer_norm(x_sample, ln_in_g, ln_in_b)
    ks, vs, sts = [], [], []
    for l in range(DEPTH):
        mod_ctx = (jax.nn.silu(c_ctx) @ w_ada[l] + b_ada[l])[None, None, :]
        mod_lat = (jax.nn.silu(c) @ w_ada[l] + b_ada[l])[:, None, :]
        wl = (w_in[l], w_out[l], att_sink[l], rw_w0[l], rw_w_up[l], rw_a0[l], rw_a_up[l], rw_g_up[l],
              rw_kk[l], rw_ka[l], rw_rk[l], rw_lnx_g[l], rw_lnx_b[l], conv_w[l], ln1_g[l], ln1_b[l],
              ln2_g[l], ln2_b[l], peer_wq[l], peer_subkeys[l], peer_u[l], peer_v[l])
        xp, (k_l, v_l, s_l) = trunk_layer(xp, mod_ctx, None, *wl)
        ks.append(k_l)
        vs.append(v_l)
        sts.append(s_l)
        xs, _ = trunk_layer(xs, mod_lat, (cache_k[:, l], cache_v[:, l], state_rwkv[:, l]), *wl)
    new_k = jnp.stack(ks, axis=1)
    new_v = jnp.stack(vs, axis=1)
    new_s = jnp.stack(sts, axis=1)
    return (xp, xs, new_k, new_v, new_s)
```

```python
import functools
import math

import jax
import jax.numpy as jnp
import numpy as np
from jax import lax
from jax.experimental import pallas as pl
from jax.experimental.pallas import tpu as pltpu

D_MODEL = 1024
DEPTH = 4
GRID_W = 64
HEAD_DIM = 64
ATT_DIM = 512
RWKV_DIM = 256
CONV_DIM = 256
ATT_HEADS = 8
ATT_KV_HEADS = 2
ATT_GROUP = 4
KV_DIM = 128
WINDOW = 128
ATT_BLOCK = 128
ROPE_THETA = 10000.0
RWKV_HEADS = 4
W_RANK = 32
A_RANK = 32
G_RANK = 64
RWKV_LNX_EPS = 64e-5
PEER_HEADS = 8
PEER_NKEYS = 128
PEER_TOPK = 16
PEER_HALF = 128
LN_EPS = 1e-5
ALPHA = (2 * DEPTH) ** 0.25
NEG = -1e30

LANES = 128
SUBLANES = 8
VMEM_LIMIT = 56 * 1024 * 1024

QK_W = ATT_DIM + KV_DIM
ATT_W = ATT_DIM + 2 * KV_DIM
RW_W = 1024
CV_W = 3 * CONV_DIM
IN_PAD = ATT_W + RW_W + CV_W
RW_LOW = 3 * RWKV_DIM

RW_CHUNK = 64
HIGHEST = lax.Precision.HIGHEST
NN = (((1,), (0,)), ((), ()))
NT = (((1,), (1,)), ((), ()))
TN = (((0,), (0,)), ((), ()))


def _mm(a, b, dims=NN, precision=None):
    return lax.dot_general(a, b, dims, precision=precision, preferred_element_type=jnp.float32)


def _bf(x):
    return x.astype(jnp.bfloat16)


def _layer_norm(x, g, b, eps):
    mu = jnp.mean(x, axis=-1, keepdims=True)
    xc = x - mu
    var = jnp.mean(xc * xc, axis=-1, keepdims=True)
    return xc * lax.rsqrt(var + eps) * g + b


def _params(*sem):
    return pltpu.CompilerParams(dimension_semantics=sem, vmem_limit_bytes=VMEM_LIMIT)


def _full(shape):
    nd = len(shape)
    return pl.BlockSpec(shape, lambda *_: (0,) * nd)


def _mod_kernel(c_ref, w_ref, b_ref, o_ref):
    cv = c_ref[...]
    s = cv * jax.nn.sigmoid(cv)
    o_ref[...] = _mm(s, w_ref[...], precision=HIGHEST) + b_ref[...]


def _mod_call(cvec, w_ada, b_ada):
    depth, d, n6 = w_ada.shape
    rows = cvec.shape[0]
    nt = n6 // d
    return pl.pallas_call(
        _mod_kernel,
        grid=(depth, nt),
        in_specs=[
            pl.BlockSpec((rows, d), lambda l, n: (0, 0)),
            pl.BlockSpec((None, d, d), lambda l, n: (l, 0, n)),
            pl.BlockSpec((None, 1, d), lambda l, n: (l, 0, n)),
        ],
        out_specs=pl.BlockSpec((None, rows, d), lambda l, n: (l, 0, n)),
        out_shape=jax.ShapeDtypeStruct((depth, rows, n6), jnp.float32),
        compiler_params=_params("parallel", "parallel"),
        name="adaln_mod",
    )(cvec, w_ada, b_ada.reshape(depth, 1, n6))


def _ln_kernel(x_ref, g_ref, b_ref, o_ref):
    o_ref[...] = _layer_norm(x_ref[...], g_ref[...], b_ref[...], LN_EPS)


def _ln_call(x, g, b, tb):
    rows, d = x.shape
    return pl.pallas_call(
        _ln_kernel,
        grid=(rows // tb,),
        in_specs=[pl.BlockSpec((tb, d), lambda i: (i, 0)), _full((1, d)), _full((1, d))],
        out_specs=pl.BlockSpec((tb, d), lambda i: (i, 0)),
        out_shape=jax.ShapeDtypeStruct((rows, d), jnp.float32),
        compiler_params=_params("parallel"),
        name="ln_in",
    )(x, g.reshape(1, d), b.reshape(1, d))


def _group_of_block(i, n_ctx_blocks, blocks_per_lat_seq):
    return jnp.where(i < n_ctx_blocks, 0, 1 + (i - n_ctx_blocks) // blocks_per_lat_seq)


def _in_proj_kernel(x_ref, mod_ref, w_ref, cos_ref, sin_ref, att_ref, rw_ref, cv_ref):
    x = x_ref[...]
    h = x * (1.0 + mod_ref[1:2, :]) + mod_ref[0:1, :]
    p = _mm(_bf(h), w_ref[...])
    qk = p[:, :QK_W]
    lane = lax.broadcasted_iota(jnp.int32, qk.shape, 1)
    partner = jnp.where((lane % 32) < 16,
                        pltpu.roll(qk, QK_W - 16, 1),
                        pltpu.roll(qk, 16, 1))
    att_ref[:, :QK_W] = qk * cos_ref[...] + partner * sin_ref[...]
    att_ref[:, QK_W:] = p[:, QK_W:ATT_W]
    rw_ref[...] = p[:, ATT_W:ATT_W + RW_W]
    cv_ref[...] = p[:, ATT_W + RW_W:]


def _in_proj_call(x, mod, w_in_p, cos_t, sin_t, tb, n_ctx_rows, lat_seq):
    rows, d = x.shape
    ncb = n_ctx_rows // tb
    bps = lat_seq // tb

    def mod_map(i):
        return (_group_of_block(i, ncb, bps), 0, 0)

    def rope_map(i):
        return (jnp.where(i < ncb, bps, (i - ncb) % bps), 0)

    return pl.pallas_call(
        _in_proj_kernel,
        grid=(rows // tb,),
        in_specs=[
            pl.BlockSpec((tb, d), lambda i: (i, 0)),
            pl.BlockSpec((None, 6, d), mod_map),
            _full((d, IN_PAD)),
            pl.BlockSpec((tb, QK_W), rope_map),
            pl.BlockSpec((tb, QK_W), rope_map),
        ],
        out_specs=[
            pl.BlockSpec((tb, ATT_W), lambda i: (i, 0)),
            pl.BlockSpec((tb, RW_W), lambda i: (i, 0)),
            pl.BlockSpec((tb, CV_W), lambda i: (i, 0)),
        ],
        out_shape=[
            jax.ShapeDtypeStruct((rows, ATT_W), jnp.float32),
            jax.ShapeDtypeStruct((rows, RW_W), jnp.float32),
            jax.ShapeDtypeStruct((rows, CV_W), jnp.float32),
        ],
        compiler_params=_params("parallel"),
        name="in_proj",
    )(x, mod, w_in_p, cos_t, sin_t)


def _ctx_attn_kernel(sink_ref, p_ref, o_ref):
    scale = HEAD_DIM ** -0.5
    for kv in range(ATT_KV_HEADS):
        k = _bf(p_ref[:, ATT_DIM + kv * HEAD_DIM:ATT_DIM + (kv + 1) * HEAD_DIM])
        v = _bf(p_ref[:, ATT_DIM + KV_DIM + kv * HEAD_DIM:ATT_DIM + KV_DIM + (kv + 1) * HEAD_DIM])
        for g in range(ATT_GROUP):
            hd = kv * ATT_GROUP + g
            q = _bf(p_ref[:, hd * HEAD_DIM:(hd + 1) * HEAD_DIM])
            s = _mm(q, k, NT) * scale
            sink = sink_ref[hd]
            m = jnp.maximum(jnp.max(s, axis=-1, keepdims=True), sink)
            e = jnp.exp(s - m)
            den = jnp.sum(e, axis=-1, keepdims=True) + jnp.exp(sink - m)
            o_ref[:, hd * HEAD_DIM:(hd + 1) * HEAD_DIM] = _mm(_bf(e), v) / den


def _ctx_attn_call(p_att, sink, n_seq, seq):
    return pl.pallas_call(
        _ctx_attn_kernel,
        grid=(n_seq,),
        in_specs=[
            pl.BlockSpec(memory_space=pltpu.SMEM),
            pl.BlockSpec((seq, ATT_W), lambda b: (b, 0)),
        ],
        out_specs=pl.BlockSpec((seq, ATT_DIM), lambda b: (b, 0)),
        out_shape=jax.ShapeDtypeStruct((n_seq * seq, ATT_DIM), jnp.float32),
        compiler_params=_params("parallel"),
        name="ctx_attn",
    )(sink, p_att)


def _lat_attn_kernel(sink_ref, own_ref, prev_ref, next_ref, ck_ref, cv_ref, o_ref, *, n_blocks):
    n = pl.program_id(1)
    scale = HEAD_DIM ** -0.5
    qi = lax.broadcasted_iota(jnp.int32, (ATT_BLOCK, 3 * ATT_BLOCK), 0)
    kj = lax.broadcasted_iota(jnp.int32, (ATT_BLOCK, 3 * ATT_BLOCK), 1)
    rel = kj - ATT_BLOCK - qi
    kpos = (n - 1) * ATT_BLOCK + kj
    mask = (jnp.abs(rel) <= WINDOW) & (kpos >= 0) & (kpos < n_blocks * ATT_BLOCK)
    mask4 = jnp.concatenate([mask] * ATT_GROUP, axis=0)
    row_group = lax.broadcasted_iota(jnp.int32, (ATT_GROUP * ATT_BLOCK, 1), 0) // ATT_BLOCK
    for kv in range(ATT_KV_HEADS):
        ks = slice(ATT_DIM + kv * HEAD_DIM, ATT_DIM + (kv + 1) * HEAD_DIM)
        vs = slice(ATT_DIM + KV_DIM + kv * HEAD_DIM, ATT_DIM + KV_DIM + (kv + 1) * HEAD_DIM)
        kw = _bf(jnp.concatenate([prev_ref[:, ks], own_ref[:, ks], next_ref[:, ks]], axis=0))
        vw = _bf(jnp.concatenate([prev_ref[:, vs], own_ref[:, vs], next_ref[:, vs]], axis=0))
        ck = _bf(ck_ref[:, kv * HEAD_DIM:(kv + 1) * HEAD_DIM])
        cv = _bf(cv_ref[:, kv * HEAD_DIM:(kv + 1) * HEAD_DIM])
        q4 = _bf(jnp.concatenate(
            [own_ref[:, (kv * ATT_GROUP + g) * HEAD_DIM:(kv * ATT_GROUP + g + 1) * HEAD_DIM]
             for g in range(ATT_GROUP)], axis=0))
        s_ctx = _mm(q4, ck, NT) * scale
        s_win = jnp.where(mask4, _mm(q4, kw, NT) * scale, NEG)
        sink = jnp.full((ATT_GROUP * ATT_BLOCK, 1), sink_ref[kv * ATT_GROUP], jnp.float32)
        for g in range(1, ATT_GROUP):
            sink = jnp.where(row_group == g, sink_ref[kv * ATT_GROUP + g], sink)
        m = jnp.maximum(jnp.maximum(jnp.max(s_ctx, axis=-1, keepdims=True),
                                    jnp.max(s_win, axis=-1, keepdims=True)), sink)
        e_ctx = jnp.exp(s_ctx - m)
        e_win = jnp.exp(s_win - m)
        den = (jnp.sum(e_ctx, axis=-1, keepdims=True) + jnp.sum(e_win, axis=-1, keepdims=True)
               + jnp.exp(sink - m))
        o4 = (_mm(_bf(e_ctx), cv) + _mm(_bf(e_win), vw)) / den
        for g in range(ATT_GROUP):
            hd = kv * ATT_GROUP + g
            o_ref[:, hd * HEAD_DIM:(hd + 1) * HEAD_DIM] = o4[g * ATT_BLOCK:(g + 1) * ATT_BLOCK]


def _lat_attn_call(p_att, ck, cv, sink, layer, n_seq, seq, row_off):
    nb = seq // ATT_BLOCK
    off = row_off // ATT_BLOCK
    past = ck.shape[2]

    def own(b, n):
        return (off + b * nb + n, 0)

    def prev(b, n):
        return (off + b * nb + jnp.maximum(n - 1, 0), 0)

    def nxt(b, n):
        return (off + b * nb + jnp.minimum(n + 1, nb - 1), 0)

    return pl.pallas_call(
        functools.partial(_lat_attn_kernel, n_blocks=nb),
        grid=(n_seq, nb),
        in_specs=[
            pl.BlockSpec(memory_space=pltpu.SMEM),
            pl.BlockSpec((ATT_BLOCK, ATT_W), own),
            pl.BlockSpec((ATT_BLOCK, ATT_W), prev),
            pl.BlockSpec((ATT_BLOCK, ATT_W), nxt),
            pl.BlockSpec((None, None, past, KV_DIM), lambda b, n: (b, layer, 0, 0)),
            pl.BlockSpec((None, None, past, KV_DIM), lambda b, n: (b, layer, 0, 0)),
        ],
        out_specs=pl.BlockSpec((ATT_BLOCK, ATT_DIM), lambda b, n: (b * nb + n, 0)),
        out_shape=jax.ShapeDtypeStruct((n_seq * seq, ATT_DIM), jnp.float32),
        compiler_params=_params("parallel", "parallel"),
        name="lat_attn",
    )(sink, p_att, p_att, p_att, ck, cv)


def _rwkv_consts():
    L, H, W = RW_CHUNK, RWKV_HEADS, RWKV_DIM
    t = np.arange(L)
    inc = np.stack([(t[None, :] <= t[:, None]), (t[None, :] >= t[:, None])]).astype(np.float32)
    stc = np.stack([(t[None, :] < t[:, None]), (t[None, :] > t[:, None])]).astype(np.float32)
    eye_h = np.eye(H, dtype=np.float32)
    inc_bd = np.stack([np.kron(eye_h, inc[d]) for d in range(2)])
    stc_bd = np.stack([np.kron(eye_h, stc[d]) for d in range(2)])
    blk = np.kron(eye_h, np.ones((L, HEAD_DIM), np.float32))
    eye = np.eye(W, dtype=np.float32)
    return inc, inc_bd, stc_bd, blk, eye


def _rwkv_kernel(p_ref, s0_ref, inc_ref, incbd_ref, stcbd_ref, blk_ref, eye_ref,
                 w0_ref, wup_ref, a0_ref, aup_ref, gup_ref, kkw_ref, kaw_ref, rkw_ref, lng_ref, lnb_ref,
                 o_ref, sfin_ref, y_scr, bon_scr, *, seq):
    L = RW_CHUNK
    nc = seq // L
    blk = blk_ref[...]
    ones_bd = blk
    eye = eye_ref[...]

    def hi(a, b, dims=NN):
        return _mm(a, b, dims, HIGHEST)

    def stack(x):
        return jnp.concatenate([x] * RWKV_HEADS, axis=0) * blk

    def chunk(d, r0, s_in):
        x = p_ref[pl.ds(r0, L), :]
        rr = x[:, 0:RWKV_DIM]
        rk = x[:, RWKV_DIM:2 * RWKV_DIM]
        rv = x[:, 2 * RWKV_DIM:3 * RWKV_DIM]
        low = x[:, RW_LOW:RW_LOW + LANES]
        zw = w0_ref[d] + hi(jnp.tanh(low), wup_ref[d])
        u = -zw
        softplus = jnp.maximum(u, 0.0) + jnp.log1p(jnp.exp(-jnp.abs(u)))
        lw = -jnp.exp(-softplus - 0.5)
        asig = jax.nn.sigmoid(a0_ref[d] + hi(low, aup_ref[d]))
        kd = rk * (1.0 + (asig - 1.0) * kaw_ref[...])
        kkr = rk * kkw_ref[...]
        kk = kkr * lax.rsqrt(jnp.maximum(hi(kkr * kkr, ones_bd), 1e-24))
        a_s = -kk
        b_s = kk * asig
        bon_scr[d, pl.ds(r0, L), :] = hi(rr * kd * rkw_ref[...], ones_bd) * rv

        cl = hi(inc_ref[d], lw)
        cl_end = cl[L - 1:L, :] if d == 0 else cl[0:1, :]
        e_neg = jnp.exp(-cl)
        e_tail = jnp.exp(cl_end - cl)
        a_st = stack(a_s * jnp.exp(cl - lw))
        b_st = stack(b_s * e_neg)
        k_st = stack(kd * e_neg)
        r_st = stack(rr * jnp.exp(cl))
        v_st = stack(rv)
        bh_st = stack(b_s * e_tail)
        kh_st = stack(kd * e_tail)
        stc = stcbd_ref[d]
        inc = incbd_ref[d]
        n_m = stc * hi(a_st, b_st, NT)
        m1 = stc * hi(a_st, k_st, NT)
        p_m = inc * hi(r_st, b_st, NT)
        q_m = inc * hi(r_st, k_st, NT)
        t_m = eye + n_m
        n_p = n_m
        for _ in range(5):
            n_p = hi(n_p, n_p)
            t_m = t_m + hi(t_m, n_p)
        u_loc = hi(t_m, hi(m1, v_st))
        w_t = hi(t_m, a_st)
        qv = hi(q_m, v_st)
        vk = hi(v_st, kh_st, TN)
        u_m = hi(w_t, s_in, NT) + u_loc
        y_bd = hi(r_st, s_in, NT) + hi(p_m, u_m) + qv
        y = y_bd[0:L] + y_bd[L:2 * L] + y_bd[2 * L:3 * L] + y_bd[3 * L:4 * L]
        y_scr[d, pl.ds(r0, L), :] = y
        return s_in * jnp.exp(cl_end) + hi(u_m, bh_st, TN) + vk

    def body(c, carry):
        s_f, s_b = carry
        s_f = chunk(0, pl.multiple_of(c * L, L), s_f)
        s_b = chunk(1, pl.multiple_of((nc - 1 - c) * L, L), s_b)
        return s_f, s_b

    s_f, s_b = lax.fori_loop(0, nc, body, (s0_ref[0], s0_ref[1]))
    sfin_ref[0] = s_f
    sfin_ref[1] = s_b

    inv_n = 1.0 / HEAD_DIM

    def epilogue(c, carry):
        r0 = pl.multiple_of(c * L, L)
        y = y_scr[0, pl.ds(r0, L), :] + y_scr[1, pl.ds(r0, L), :]
        mu = hi(y, ones_bd) * inv_n
        yc = y - mu
        var = hi(yc * yc, ones_bd) * inv_n
        yn = yc * lax.rsqrt(var + RWKV_LNX_EPS) * lng_ref[...] + lnb_ref[...]
        gd = p_ref[pl.ds(r0, L), RW_LOW + LANES:RW_LOW + 2 * LANES]
        g = hi(jax.nn.sigmoid(gd), gup_ref[...])
        bon = bon_scr[0, pl.ds(r0, L), :] + bon_scr[1, pl.ds(r0, L), :]
        o_ref[pl.ds(r0, L), :] = (yn + bon) * g
        return carry

    lax.fori_loop(0, nc, epilogue, 0)


def _rwkv_call(p_rw, s0_bd, consts, wts, n_seq, seq, row_off):
    inc, inc_bd, stc_bd, blk, eye = consts
    W = RWKV_DIM
    off = row_off // seq
    in_specs = [
        pl.BlockSpec((seq, RW_W), lambda b: (off + b, 0)),
        pl.BlockSpec((None, 2, W, W), lambda b: (b, 0, 0, 0)),
        _full(inc.shape), _full(inc_bd.shape), _full(stc_bd.shape), _full(blk.shape), _full(eye.shape),
    ] + [_full(w.shape) for w in wts]
    return pl.pallas_call(
        functools.partial(_rwkv_kernel, seq=seq),
        grid=(n_seq,),
        in_specs=in_specs,
        out_specs=[
            pl.BlockSpec((seq, W), lambda b: (b, 0)),
            pl.BlockSpec((None, 2, W, W), lambda b: (b, 0, 0, 0)),
        ],
        out_shape=[
            jax.ShapeDtypeStruct((n_seq * seq, W), jnp.float32),
            jax.ShapeDtypeStruct((n_seq, 2, W, W), jnp.float32),
        ],
        scratch_shapes=[pltpu.VMEM((2, seq, W), jnp.float32), pltpu.VMEM((2, seq, W), jnp.float32)],
        compiler_params=_params("parallel"),
        name="rwkv",
    )(p_rw, s0_bd, inc, inc_bd, stc_bd, blk, eye, *wts)


def _rwkv_weights(w0, w_up, a0, a_up, g_up, kk, ka, rk, lng, lnb):
    W = RWKV_DIM
    wup_p = jnp.zeros((2, LANES, W), jnp.float32)
    aup_p = jnp.zeros((2, LANES, W), jnp.float32)
    for d in range(2):
        wup_p = wup_p.at[d, d * W_RANK:(d + 1) * W_RANK].set(w_up[d])
        aup_p = aup_p.at[d, 2 * W_RANK + d * A_RANK:2 * W_RANK + (d + 1) * A_RANK].set(a_up[d])
    gup_p = jnp.zeros((LANES, W), jnp.float32).at[:G_RANK].set(g_up)
    row = lambda v: v.reshape(1, W)
    return (w0.reshape(2, 1, W), wup_p, a0.reshape(2, 1, W), aup_p, gup_p,
            row(kk), row(ka), row(rk), row(lng), row(lnb))


def _out_proj_kernel(attc_ref, attl_ref, rwc_ref, rwl_ref, cv_ref, cvp_ref, cvn_ref, x_ref, mod_ref,
                     w_ref, cw_ref, g_ref, b_ref, o_ref, *, tb, ncb, bps_c, bps):
    i = pl.program_id(0)
    is_ctx = i < ncb
    pos = jnp.where(is_ctx, i % bps_c, (i - ncb) % bps)
    last = jnp.where(is_ctx, bps_c - 1, bps - 1)
    att = jnp.where(is_ctx, attc_ref[...], attl_ref[...])
    rw = jnp.where(is_ctx, rwc_ref[...], rwl_ref[...])

    cb = cv_ref[:, 0:CONV_DIM]
    z = cv_ref[:, CONV_DIM:2 * CONV_DIM] * cv_ref[:, 2 * CONV_DIM:3 * CONV_DIM]
    halo_p = cvp_ref[SUBLANES - 1:SUBLANES, CONV_DIM:2 * CONV_DIM] * cvp_ref[SUBLANES - 1:SUBLANES, 2 * CONV_DIM:]
    halo_n = cvn_ref[0:1, CONV_DIM:2 * CONV_DIM] * cvn_ref[0:1, 2 * CONV_DIM:]
    halo_p = jnp.where(pos > 0, halo_p, 0.0)
    halo_n = jnp.where(pos < last, halo_n, 0.0)
    row = lax.broadcasted_iota(jnp.int32, z.shape, 0)
    z_prev = jnp.where(row == 0, halo_p, pltpu.roll(z, 1, 0))
    z_next = jnp.where(row == tb - 1, halo_n, pltpu.roll(z, tb - 1, 0))
    conv = cb * (cw_ref[0:1, :] * z_prev + cw_ref[1:2, :] * z + cw_ref[2:3, :] * z_next)

    mix = (_mm(_bf(att), w_ref[0:ATT_DIM, :])
           + _mm(_bf(rw), w_ref[ATT_DIM:ATT_DIM + RWKV_DIM, :])
           + _mm(_bf(conv), w_ref[ATT_DIM + RWKV_DIM:, :]))
    o_ref[...] = _layer_norm(ALPHA * x_ref[...] + mod_ref[2:3, :] * mix, g_ref[...], b_ref[...], LN_EPS)


def _out_proj_call(att_c, att_l, rw_c, rw_l, p_cv, x, mod, w_out_b, conv_w, ln_g, ln_b, tb, n_ctx_rows, ctx_seq,
                   lat_seq):
    rows, d = x.shape
    ncb = n_ctx_rows // tb
    bps = lat_seq // tb
    sub = tb // SUBLANES
    n_sub = rows // SUBLANES

    def ctx_map(i):
        return (jnp.minimum(i, ncb - 1), 0)

    def lat_map(i):
        return (jnp.maximum(i - ncb, 0), 0)

    return pl.pallas_call(
        functools.partial(_out_proj_kernel, tb=tb, ncb=ncb, bps_c=ctx_seq // tb, bps=bps),
        grid=(rows // tb,),
        in_specs=[
            pl.BlockSpec((tb, ATT_DIM), ctx_map),
            pl.BlockSpec((tb, ATT_DIM), lat_map),
            pl.BlockSpec((tb, RWKV_DIM), ctx_map),
            pl.BlockSpec((tb, RWKV_DIM), lat_map),
            pl.BlockSpec((tb, CV_W), lambda i: (i, 0)),
            pl.BlockSpec((SUBLANES, CV_W), lambda i: (jnp.maximum(i * sub - 1, 0), 0)),
            pl.BlockSpec((SUBLANES, CV_W), lambda i: (jnp.minimum((i + 1) * sub, n_sub - 1), 0)),
            pl.BlockSpec((tb, d), lambda i: (i, 0)),
            pl.BlockSpec((None, 6, d), lambda i: (_group_of_block(i, ncb, bps), 0, 0)),
            _full((d, d)),
            _full((3, CONV_DIM)),
            _full((1, d)),
            _full((1, d)),
        ],
        out_specs=pl.BlockSpec((tb, d), lambda i: (i, 0)),
        out_shape=jax.ShapeDtypeStruct((rows, d), jnp.float32),
        compiler_params=_params("parallel"),
        name="out_proj",
    )(att_c, att_l, rw_c, rw_l, p_cv, p_cv, p_cv, x, mod, w_out_b, conv_w, ln_g.reshape(1, d), ln_b.reshape(1, d))


def _gelu_tanh(x):
    return 0.5 * x * (1.0 + jnp.tanh(math.sqrt(2.0 / math.pi) * (x + 0.044715 * (x * x * x))))


def _extract_top(work, iota, n_out):
    vals = []
    big = work.shape[0]
    for _ in range(n_out):
        m = jnp.max(work, axis=0, keepdims=True)
        idx = jnp.min(jnp.where(work == m, iota, big), axis=0, keepdims=True)
        work = jnp.where(iota == idx, -jnp.inf, work)
        vals.append(m)
    return vals, work


def _peer_kernel(x_ref, mod_ref, wq_ref, sk_ref, u_ref, vt_ref, g_ref, b_ref, o_ref,
                 ht_scr, s0_scr, e0_scr, s1_scr, e1_scr, tau_scr, a_scr, z_scr, acc_scr, *, tb, et):
    e_step = pl.program_id(1)
    n_lt = tb // LANES
    K = PEER_TOPK

    @pl.when(e_step == 0)
    def _route():
        h2 = x_ref[...] * (1.0 + mod_ref[4:5, :]) + mod_ref[3:4, :]
        ht = _bf(h2.T)
        ht_scr[...] = ht
        qt = _mm(wq_ref[...], ht)
        for hp in range(2 * PEER_HEADS):
            sc = _mm(sk_ref[hp], qt[hp * PEER_HALF:(hp + 1) * PEER_HALF, :], precision=HIGHEST)
            if hp % 2 == 0:
                s0_scr[hp // 2] = sc
            else:
                s1_scr[hp // 2] = sc

        iota_k = lax.broadcasted_iota(jnp.int32, (PEER_NKEYS, LANES), 0)
        iota_c = lax.broadcasted_iota(jnp.int32, (K * K, LANES), 0)

        def per_tile(j, carry):
            hd = j // n_lt
            lanes = pl.ds(pl.multiple_of((j % n_lt) * LANES, LANES), LANES)
            sc0 = s0_scr[hd, :, lanes]
            sc1 = s1_scr[hd, :, lanes]
            v0, w0 = _extract_top(sc0, iota_k, K)
            v1, w1 = _extract_top(sc1, iota_k, K)
            cand0 = w0 != sc0
            cand1 = w1 != sc1
            top1 = jnp.concatenate(v1, axis=0)
            pair = jnp.concatenate([v0[a] + top1 for a in range(K)], axis=0)
            f, _ = _extract_top(pair, iota_c, K)
            zsum = jnp.zeros_like(f[0])
            for k in range(K):
                zsum = zsum + jnp.exp(f[k] - f[0])
            tau_scr[hd, :, lanes] = jnp.broadcast_to(f[K - 1], (SUBLANES, LANES))
            e0_scr[hd, :, lanes] = jnp.where(cand0, jnp.exp(sc0 - v0[0]), 0.0) / zsum
            e1_scr[hd, :, lanes] = jnp.where(cand1, jnp.exp(sc1 - v1[0]), 0.0)
            s1_scr[hd, :, lanes] = jnp.where(cand1, sc1, -jnp.inf)
            return carry

        lax.fori_loop(0, PEER_HEADS * n_lt, per_tile, 0)
        acc_scr[...] = jnp.zeros_like(acc_scr)

    a_scr[...] = _gelu_tanh(_mm(u_ref[...], ht_scr[...]))
    keys = pl.ds(pl.multiple_of(e_step * SUBLANES, SUBLANES), SUBLANES)

    def gate_tile(lt, carry):
        lanes = pl.ds(pl.multiple_of(lt * LANES, LANES), LANES)
        s0g = [s0_scr[hd, keys, lanes] for hd in range(PEER_HEADS)]
        e0g = [e0_scr[hd, keys, lanes] for hd in range(PEER_HEADS)]
        for ii in range(SUBLANES):
            rows = slice(ii * PEER_NKEYS, (ii + 1) * PEER_NKEYS)
            g = jnp.zeros((PEER_NKEYS, LANES), jnp.float32)
            for hd in range(PEER_HEADS):
                s0r = s0g[hd][ii:ii + 1, :]
                e0r = e0g[hd][ii:ii + 1, :]
                tau = tau_scr[hd, 0:1, lanes]
                g = g + jnp.where(s0r + s1_scr[hd, :, lanes] >= tau, e0r * e1_scr[hd, :, lanes], 0.0)
            z_scr[rows, lanes] = _bf(g * a_scr[rows, lanes])
        return carry

    lax.fori_loop(0, n_lt, gate_tile, 0)
    acc_scr[...] += _mm(vt_ref[...], z_scr[...])

    @pl.when(e_step == pl.num_programs(1) - 1)
    def _finish():
        out = acc_scr[...].T
        o_ref[...] = _layer_norm(ALPHA * x_ref[...] + mod_ref[5:6, :] * out, g_ref[...], b_ref[...], LN_EPS)


def _peer_call(x, mod, wq_t, subkeys, u_b, vt_b, ln_g, ln_b, tb, n_ctx_rows, lat_seq):
    rows, d = x.shape
    et = SUBLANES * PEER_NKEYS
    ncb = n_ctx_rows // tb
    bps = lat_seq // tb
    n_exp = u_b.shape[0]
    nq = wq_t.shape[0]
    f32 = jnp.float32
    return pl.pallas_call(
        functools.partial(_peer_kernel, tb=tb, et=et),
        grid=(rows // tb, n_exp // et),
        in_specs=[
            pl.BlockSpec((tb, d), lambda i, e: (i, 0)),
            pl.BlockSpec((None, 6, d), lambda i, e: (_group_of_block(i, ncb, bps), 0, 0)),
            pl.BlockSpec((nq, d), lambda i, e: (0, 0)),
            pl.BlockSpec(subkeys.shape, lambda i, e: (0, 0, 0)),
            pl.BlockSpec((et, d), lambda i, e: (e, 0)),
            pl.BlockSpec((d, et), lambda i, e: (0, e)),
            pl.BlockSpec((1, d), lambda i, e: (0, 0)),
            pl.BlockSpec((1, d), lambda i, e: (0, 0)),
        ],
        out_specs=pl.BlockSpec((tb, d), lambda i, e: (i, 0)),
        out_shape=jax.ShapeDtypeStruct((rows, d), f32),
        scratch_shapes=[
            pltpu.VMEM((d, tb), jnp.bfloat16),
            pltpu.VMEM((PEER_HEADS, PEER_NKEYS, tb), f32),
            pltpu.VMEM((PEER_HEADS, PEER_NKEYS, tb), f32),
            pltpu.VMEM((PEER_HEADS, PEER_NKEYS, tb), f32),
            pltpu.VMEM((PEER_HEADS, PEER_NKEYS, tb), f32),
            pltpu.VMEM((PEER_HEADS, SUBLANES, tb), f32),
            pltpu.VMEM((et, tb), f32),
            pltpu.VMEM((et, tb), jnp.bfloat16),
            pltpu.VMEM((d, tb), f32),
        ],
        compiler_params=_params("parallel", "arbitrary"),
        name="peer",
    )(x, mod, wq_t, subkeys, u_b, vt_b, ln_g.reshape(1, d), ln_b.reshape(1, d))


def _rope_tables(seq, tb):
    half = HEAD_DIM // 4
    freqs = ROPE_THETA ** (-jnp.arange(half, dtype=jnp.float32) / half)
    t = jnp.arange(seq)
    cos_parts, sin_parts = [], []
    for pos in (t // GRID_W, t % GRID_W):
        ang = pos.astype(jnp.float32)[:, None] * freqs[None, :]
        c, s = jnp.cos(ang), jnp.sin(ang)
        cos_parts += [c, c]
        sin_parts += [-s, s]
    cos_h = jnp.concatenate(cos_parts, axis=1)
    sin_h = jnp.concatenate(sin_parts, axis=1)
    n_rot = QK_W // HEAD_DIM
    cos_t = jnp.concatenate([jnp.tile(cos_h, (1, n_rot)), jnp.ones((tb, QK_W), jnp.float32)], axis=0)
    sin_t = jnp.concatenate([jnp.tile(sin_h, (1, n_rot)), jnp.zeros((tb, QK_W), jnp.float32)], axis=0)
    return cos_t, sin_t


def _pad_w_in(w):
    split = w.shape[1] - CV_W
    z = jnp.zeros((w.shape[0], IN_PAD - w.shape[1]), w.dtype)
    return jnp.concatenate([w[:, :split], z, w[:, split:]], axis=1)


def _forward(x_prompt, x_sample, cache_k, cache_v, state_rwkv, c, c_ctx, ln_in_g, ln_in_b, w_ada, b_ada,
             w_in, w_out, att_sink, rw_w0, rw_w_up, rw_a0, rw_a_up, rw_g_up, rw_kk, rw_ka, rw_rk,
             rw_lnx_g, rw_lnx_b, conv_w, ln1_g, ln1_b, ln2_g, ln2_b, peer_wq, peer_subkeys, peer_u, peer_v,
             tb_in=512, tb_out=256, tb_peer=512):
    nb_c, seq_c, d = x_prompt.shape
    nb_l, seq_l, _ = x_sample.shape
    depth = w_in.shape[0]
    n_ctx = nb_c * seq_c
    past = cache_k.shape[2]
    W = RWKV_DIM

    x = jnp.concatenate([x_prompt.reshape(n_ctx, d), x_sample.reshape(nb_l * seq_l, d)], axis=0)
    x = _ln_call(x, ln_in_g, ln_in_b, tb_in)

    n_groups = 1 + nb_l
    g_pad = -(-n_groups // SUBLANES) * SUBLANES
    cvec = jnp.concatenate([c_ctx[None], c, jnp.zeros((g_pad - n_groups, d), jnp.float32)], axis=0)
    mod_all = _mod_call(cvec, w_ada, b_ada).reshape(depth, g_pad, 6, d)

    cos_t, sin_t = _rope_tables(seq_l, tb_in)
    consts = tuple(jnp.asarray(a) for a in _rwkv_consts())
    ck_all = cache_k.reshape(nb_l, depth, past, KV_DIM)
    cv_all = cache_v.reshape(nb_l, depth, past, KV_DIM)
    eye_h = jnp.eye(RWKV_HEADS, dtype=jnp.float32)
    s0_bd_all = jnp.einsum('bldhij,hg->bldhigj', state_rwkv, eye_h).reshape(nb_l, depth, 2, W, W)
    s0_zero = jnp.zeros((nb_c, 2, W, W), jnp.float32)

    ks, vs, sts = [], [], []
    for l in range(depth):
        mod = mod_all[l]
        p_att, p_rw, p_cv = _in_proj_call(x, mod, _bf(_pad_w_in(w_in[l])), cos_t, sin_t, tb_in, n_ctx, seq_l)
        att_c = _ctx_attn_call(p_att, att_sink[l], nb_c, seq_c)
        att_l = _lat_attn_call(p_att, ck_all, cv_all, att_sink[l], l, nb_l, seq_l, n_ctx)
        wts = _rwkv_weights(rw_w0[l], rw_w_up[l], rw_a0[l], rw_a_up[l], rw_g_up[l], rw_kk[l], rw_ka[l],
                            rw_rk[l], rw_lnx_g[l], rw_lnx_b[l])
        rw_c, sfin = _rwkv_call(p_rw, s0_zero, consts, wts, nb_c, seq_c, 0)
        rw_l, _ = _rwkv_call(p_rw, s0_bd_all[:, l], consts, wts, nb_l, seq_l, n_ctx)
        x1 = _out_proj_call(att_c, att_l, rw_c, rw_l, p_cv, x, mod, _bf(w_out[l]), conv_w[l],
                            ln1_g[l], ln1_b[l], tb_out, n_ctx, seq_c, seq_l)
        sk = peer_subkeys[l].reshape(2 * PEER_HEADS, PEER_NKEYS, PEER_HALF)
        x = _peer_call(x1, mod, _bf(peer_wq[l].T), sk, _bf(peer_u[l]), _bf(peer_v[l].T),
                       ln2_g[l], ln2_b[l], tb_peer, n_ctx, seq_l)
        ks.append(p_att[:n_ctx, ATT_DIM:ATT_DIM + KV_DIM].reshape(nb_c, seq_c, ATT_KV_HEADS, HEAD_DIM))
        vs.append(p_att[:n_ctx, ATT_DIM + KV_DIM:].reshape(nb_c, seq_c, ATT_KV_HEADS, HEAD_DIM))
        s5 = sfin.reshape(nb_c, 2, RWKV_HEADS, HEAD_DIM, RWKV_HEADS, HEAD_DIM)
        sts.append(jnp.stack([s5[:, :, h, :, h, :] for h in range(RWKV_HEADS)], axis=2))

    y_prompt = x[:n_ctx].reshape(nb_c, seq_c, d)
    y_sample = x[n_ctx:].reshape(nb_l, seq_l, d)
    return (y_prompt, y_sample, jnp.stack(ks, axis=1), jnp.stack(vs, axis=1), jnp.stack(sts, axis=1))


def kernel(x_prompt, x_sample, cache_k, cache_v, state_rwkv, c, c_ctx, ln_in_g, ln_in_b, w_ada, b_ada, w_in, w_out, att_sink, rw_w0, rw_w_up, rw_a0, rw_a_up, rw_g_up, rw_kk, rw_ka, rw_rk, rw_lnx_g, rw_lnx_b, conv_w, ln1_g, ln1_b, ln2_g, ln2_b, peer_wq, peer_subkeys, peer_u, peer_v):
    return _forward(x_prompt, x_sample, cache_k, cache_v, state_rwkv, c, c_ctx, ln_in_g, ln_in_b, w_ada, b_ada,
                    w_in, w_out, att_sink, rw_w0, rw_w_up, rw_a0, rw_a_up, rw_g_up, rw_kk, rw_ka, rw_rk,
                    rw_lnx_g, rw_lnx_b, conv_w, ln1_g, ln1_b, ln2_g, ln2_b, peer_wq, peer_subkeys, peer_u, peer_v)
```

```python
import functools
import math

import jax
import jax.numpy as jnp
import numpy as np
from jax import lax
from jax.experimental import pallas as pl
from jax.experimental.pallas import tpu as pltpu

D_MODEL = 1024
DEPTH = 4
GRID_W = 64
HEAD_DIM = 64
ATT_DIM = 512
RWKV_DIM = 256
CONV_DIM = 256
ATT_HEADS = 8
ATT_KV_HEADS = 2
ATT_GROUP = 4
KV_DIM = 128
WINDOW = 128
ATT_BLOCK = 128
ROPE_THETA = 10000.0
RWKV_HEADS = 4
W_RANK = 32
A_RANK = 32
G_RANK = 64
RWKV_LNX_EPS = 64e-5
PEER_HEADS = 8
PEER_NKEYS = 128
PEER_TOPK = 16
PEER_HALF = 128
LN_EPS = 1e-5
ALPHA = (2 * DEPTH) ** 0.25
NEG = -1e30

LANES = 128
SUBLANES = 8
VMEM_LIMIT = 56 * 1024 * 1024

QK_W = ATT_DIM + KV_DIM
ATT_W = ATT_DIM + 2 * KV_DIM
RW_W = 1024
CV_W = 3 * CONV_DIM
IN_PAD = ATT_W + RW_W + CV_W
RW_LOW = 3 * RWKV_DIM

RW_CHUNK = 64
PEER_MM_TOKENS = 256
HIGHEST = lax.Precision.HIGHEST
NN = (((1,), (0,)), ((), ()))
NT = (((1,), (1,)), ((), ()))
TN = (((0,), (0,)), ((), ()))


def _mm(a, b, dims=NN, precision=None):
    return lax.dot_general(a, b, dims, precision=precision, preferred_element_type=jnp.float32)


def _bf(x):
    return x.astype(jnp.bfloat16)


def _split(x):
    hi = _bf(x)
    return hi, _bf(x - hi.astype(jnp.float32))


def _mm_split(a, b, dims, passes):
    out = _mm(a[0], b[0], dims)
    if passes == 3:
        out = out + (_mm(a[0], b[1], dims) + _mm(a[1], b[0], dims))
    return out


def _mm_terms(x, exact, terms, x_is_rhs=False):
    out = None
    rest = x
    for _ in range(terms):
        piece = _bf(rest)
        rest = rest - piece.astype(jnp.float32)
        part = _mm(exact, piece) if x_is_rhs else _mm(piece, exact)
        out = part if out is None else out + part
    return out


RW_PASSES = {"gram_n": 1, "gram": 1, "inv": 1, "loc": 1, "seq": 1}


def _layer_norm(x, g, b, eps):
    mu = jnp.mean(x, axis=-1, keepdims=True)
    xc = x - mu
    var = jnp.mean(xc * xc, axis=-1, keepdims=True)
    return xc * lax.rsqrt(var + eps) * g + b


def _params(*sem):
    return pltpu.CompilerParams(dimension_semantics=sem, vmem_limit_bytes=VMEM_LIMIT)


def _full(shape):
    nd = len(shape)
    return pl.BlockSpec(shape, lambda *_: (0,) * nd)


def _mod_kernel(c_ref, w_ref, b_ref, o_ref):
    cv = c_ref[...]
    s = cv * jax.nn.sigmoid(cv)
    o_ref[...] = _mm(s, w_ref[...], precision=HIGHEST) + b_ref[...]


def _mod_call(cvec, w_ada, b_ada):
    depth, d, n6 = w_ada.shape
    rows = cvec.shape[0]
    nt = n6 // d
    return pl.pallas_call(
        _mod_kernel,
        grid=(depth, nt),
        in_specs=[
            pl.BlockSpec((rows, d), lambda l, n: (0, 0)),
            pl.BlockSpec((None, d, d), lambda l, n: (l, 0, n)),
            pl.BlockSpec((None, 1, d), lambda l, n: (l, 0, n)),
        ],
        out_specs=pl.BlockSpec((None, rows, d), lambda l, n: (l, 0, n)),
        out_shape=jax.ShapeDtypeStruct((depth, rows, n6), jnp.float32),
        compiler_params=_params("parallel", "parallel"),
        name="adaln_mod",
    )(cvec, w_ada, b_ada.reshape(depth, 1, n6))


def _ln_kernel(x_ref, g_ref, b_ref, o_ref):
    o_ref[...] = _layer_norm(x_ref[...], g_ref[...], b_ref[...], LN_EPS)


def _ln_call(x, g, b, tb):
    rows, d = x.shape
    return pl.pallas_call(
        _ln_kernel,
        grid=(rows // tb,),
        in_specs=[pl.BlockSpec((tb, d), lambda i: (i, 0)), _full((1, d)), _full((1, d))],
        out_specs=pl.BlockSpec((tb, d), lambda i: (i, 0)),
        out_shape=jax.ShapeDtypeStruct((rows, d), jnp.float32),
        compiler_params=_params("parallel"),
        name="ln_in",
    )(x, g.reshape(1, d), b.reshape(1, d))


def _group_of_block(i, n_ctx_blocks, blocks_per_lat_seq):
    return jnp.where(i < n_ctx_blocks, 0, 1 + (i - n_ctx_blocks) // blocks_per_lat_seq)


def _in_proj_kernel(x_ref, mod_ref, w_ref, cos_ref, sin_ref, att_ref, rw_ref, cv_ref):
    x = x_ref[...]
    h = x * (1.0 + mod_ref[1:2, :]) + mod_ref[0:1, :]
    p = _mm(_bf(h), w_ref[...])
    qk = p[:, :QK_W]
    lane = lax.broadcasted_iota(jnp.int32, qk.shape, 1)
    partner = jnp.where((lane % 32) < 16,
                        pltpu.roll(qk, QK_W - 16, 1),
                        pltpu.roll(qk, 16, 1))
    att_ref[:, :QK_W] = qk * cos_ref[...] + partner * sin_ref[...]
    att_ref[:, QK_W:] = p[:, QK_W:ATT_W]
    rw_ref[...] = p[:, ATT_W:ATT_W + RW_W]
    cv_ref[...] = p[:, ATT_W + RW_W:]


def _in_proj_call(x, mod, w_in_p, cos_t, sin_t, tb, n_ctx_rows, lat_seq):
    rows, d = x.shape
    ncb = n_ctx_rows // tb
    bps = lat_seq // tb

    def mod_map(i):
        return (_group_of_block(i, ncb, bps), 0, 0)

    def rope_map(i):
        return (jnp.where(i < ncb, bps, (i - ncb) % bps), 0)

    return pl.pallas_call(
        _in_proj_kernel,
        grid=(rows // tb,),
        in_specs=[
            pl.BlockSpec((tb, d), lambda i: (i, 0)),
            pl.BlockSpec((None, 6, d), mod_map),
            _full((d, IN_PAD)),
            pl.BlockSpec((tb, QK_W), rope_map),
            pl.BlockSpec((tb, QK_W), rope_map),
        ],
        out_specs=[
            pl.BlockSpec((tb, ATT_W), lambda i: (i, 0)),
            pl.BlockSpec((tb, RW_W), lambda i: (i, 0)),
            pl.BlockSpec((tb, CV_W), lambda i: (i, 0)),
        ],
        out_shape=[
            jax.ShapeDtypeStruct((rows, ATT_W), jnp.float32),
            jax.ShapeDtypeStruct((rows, RW_W), jnp.float32),
            jax.ShapeDtypeStruct((rows, CV_W), jnp.float32),
        ],
        compiler_params=_params("parallel"),
        name="in_proj",
    )(x, mod, w_in_p, cos_t, sin_t)


def _ctx_attn_kernel(sink_ref, p_ref, o_ref):
    scale = HEAD_DIM ** -0.5
    for kv in range(ATT_KV_HEADS):
        k = _bf(p_ref[:, ATT_DIM + kv * HEAD_DIM:ATT_DIM + (kv + 1) * HEAD_DIM])
        v = _bf(p_ref[:, ATT_DIM + KV_DIM + kv * HEAD_DIM:ATT_DIM + KV_DIM + (kv + 1) * HEAD_DIM])
        for g in range(ATT_GROUP):
            hd = kv * ATT_GROUP + g
            q = _bf(p_ref[:, hd * HEAD_DIM:(hd + 1) * HEAD_DIM])
            s = _mm(q, k, NT) * scale
            sink = sink_ref[hd]
            m = jnp.maximum(jnp.max(s, axis=-1, keepdims=True), sink)
            e = jnp.exp(s - m)
            den = jnp.sum(e, axis=-1, keepdims=True) + jnp.exp(sink - m)
            o_ref[:, hd * HEAD_DIM:(hd + 1) * HEAD_DIM] = _mm(_bf(e), v) / den


def _ctx_attn_call(p_att, sink, n_seq, seq):
    return pl.pallas_call(
        _ctx_attn_kernel,
        grid=(n_seq,),
        in_specs=[
            pl.BlockSpec(memory_space=pltpu.SMEM),
            pl.BlockSpec((seq, ATT_W), lambda b: (b, 0)),
        ],
        out_specs=pl.BlockSpec((seq, ATT_DIM), lambda b: (b, 0)),
        out_shape=jax.ShapeDtypeStruct((n_seq * seq, ATT_DIM), jnp.float32),
        compiler_params=_params("parallel"),
        name="ctx_attn",
    )(sink, p_att)


def _lat_attn_kernel(sink_ref, own_ref, prev_ref, next_ref, ck_ref, cv_ref, o_ref, *, n_blocks):
    n = pl.program_id(1)
    scale = HEAD_DIM ** -0.5
    qi = lax.broadcasted_iota(jnp.int32, (ATT_BLOCK, 3 * ATT_BLOCK), 0)
    kj = lax.broadcasted_iota(jnp.int32, (ATT_BLOCK, 3 * ATT_BLOCK), 1)
    rel = kj - ATT_BLOCK - qi
    kpos = (n - 1) * ATT_BLOCK + kj
    mask = (jnp.abs(rel) <= WINDOW) & (kpos >= 0) & (kpos < n_blocks * ATT_BLOCK)
    mask4 = jnp.concatenate([mask] * ATT_GROUP, axis=0)
    row_group = lax.broadcasted_iota(jnp.int32, (ATT_GROUP * ATT_BLOCK, 1), 0) // ATT_BLOCK
    for kv in range(ATT_KV_HEADS):
        ks = slice(ATT_DIM + kv * HEAD_DIM, ATT_DIM + (kv + 1) * HEAD_DIM)
        vs = slice(ATT_DIM + KV_DIM + kv * HEAD_DIM, ATT_DIM + KV_DIM + (kv + 1) * HEAD_DIM)
        kw = _bf(jnp.concatenate([prev_ref[:, ks], own_ref[:, ks], next_ref[:, ks]], axis=0))
        vw = _bf(jnp.concatenate([prev_ref[:, vs], own_ref[:, vs], next_ref[:, vs]], axis=0))
        ck = _bf(ck_ref[:, kv * HEAD_DIM:(kv + 1) * HEAD_DIM])
        cv = _bf(cv_ref[:, kv * HEAD_DIM:(kv + 1) * HEAD_DIM])
        q4 = _bf(jnp.concatenate(
            [own_ref[:, (kv * ATT_GROUP + g) * HEAD_DIM:(kv * ATT_GROUP + g + 1) * HEAD_DIM]
             for g in range(ATT_GROUP)], axis=0))
        s_ctx = _mm(q4, ck, NT) * scale
        s_win = jnp.where(mask4, _mm(q4, kw, NT) * scale, NEG)
        sink = jnp.full((ATT_GROUP * ATT_BLOCK, 1), sink_ref[kv * ATT_GROUP], jnp.float32)
        for g in range(1, ATT_GROUP):
            sink = jnp.where(row_group == g, sink_ref[kv * ATT_GROUP + g], sink)
        m = jnp.maximum(jnp.maximum(jnp.max(s_ctx, axis=-1, keepdims=True),
                                    jnp.max(s_win, axis=-1, keepdims=True)), sink)
        e_ctx = jnp.exp(s_ctx - m)
        e_win = jnp.exp(s_win - m)
        den = (jnp.sum(e_ctx, axis=-1, keepdims=True) + jnp.sum(e_win, axis=-1, keepdims=True)
               + jnp.exp(sink - m))
        o4 = (_mm(_bf(e_ctx), cv) + _mm(_bf(e_win), vw)) / den
        for g in range(ATT_GROUP):
            hd = kv * ATT_GROUP + g
            o_ref[:, hd * HEAD_DIM:(hd + 1) * HEAD_DIM] = o4[g * ATT_BLOCK:(g + 1) * ATT_BLOCK]


def _lat_attn_call(p_att, ck, cv, sink, layer, n_seq, seq, row_off):
    nb = seq // ATT_BLOCK
    off = row_off // ATT_BLOCK
    past = ck.shape[2]

    def own(b, n):
        return (off + b * nb + n, 0)

    def prev(b, n):
        return (off + b * nb + jnp.maximum(n - 1, 0), 0)

    def nxt(b, n):
        return (off + b * nb + jnp.minimum(n + 1, nb - 1), 0)

    return pl.pallas_call(
        functools.partial(_lat_attn_kernel, n_blocks=nb),
        grid=(n_seq, nb),
        in_specs=[
            pl.BlockSpec(memory_space=pltpu.SMEM),
            pl.BlockSpec((ATT_BLOCK, ATT_W), own),
            pl.BlockSpec((ATT_BLOCK, ATT_W), prev),
            pl.BlockSpec((ATT_BLOCK, ATT_W), nxt),
            pl.BlockSpec((None, None, past, KV_DIM), lambda b, n: (b, layer, 0, 0)),
            pl.BlockSpec((None, None, past, KV_DIM), lambda b, n: (b, layer, 0, 0)),
        ],
        out_specs=pl.BlockSpec((ATT_BLOCK, ATT_DIM), lambda b, n: (b * nb + n, 0)),
        out_shape=jax.ShapeDtypeStruct((n_seq * seq, ATT_DIM), jnp.float32),
        compiler_params=_params("parallel", "parallel"),
        name="lat_attn",
    )(sink, p_att, p_att, p_att, ck, cv)


def _rwkv_consts():
    L, H, W = RW_CHUNK, RWKV_HEADS, RWKV_DIM
    t = np.arange(L)
    inc = np.stack([(t[None, :] <= t[:, None]), (t[None, :] >= t[:, None])]).astype(np.float32)
    stc = np.stack([(t[None, :] < t[:, None]), (t[None, :] > t[:, None])]).astype(np.float32)
    eye_h = np.eye(H, dtype=np.float32)
    inc_bd = np.stack([np.kron(eye_h, inc[d]) for d in range(2)])
    stc_bd = np.stack([np.kron(eye_h, stc[d]) for d in range(2)])
    blk = np.kron(eye_h, np.ones((L, HEAD_DIM), np.float32))
    eye = np.eye(W, dtype=np.float32)
    return inc, inc_bd, stc_bd, blk, eye


def _rwkv_kernel(p_ref, s0_ref, inc_ref, incbd_ref, stcbd_ref, blk_ref, eye_ref,
                 w0_ref, wup_ref, a0_ref, aup_ref, gup_ref, kkw_ref, kaw_ref, rkw_ref, lng_ref, lnb_ref,
                 o_ref, sfin_ref, y_scr, bon_scr, *, seq):
    L = RW_CHUNK
    nc = seq // L
    blk = _bf(blk_ref[...])
    ones_bd = blk
    eye = eye_ref[...]

    def stack(x):
        return tuple(jnp.concatenate([z] * RWKV_HEADS, axis=0) * blk for z in _split(x))

    def chunk(d, r0, s_in):
        x = p_ref[pl.ds(r0, L), :]
        rr = x[:, 0:RWKV_DIM]
        rk = x[:, RWKV_DIM:2 * RWKV_DIM]
        rv = x[:, 2 * RWKV_DIM:3 * RWKV_DIM]
        low = x[:, RW_LOW:RW_LOW + LANES]
        zw = w0_ref[d] + _mm_split(_split(jnp.tanh(low)), _split(wup_ref[d]), NN, 3)
        u = -zw
        softplus = jnp.maximum(u, 0.0) + jnp.log1p(jnp.exp(-jnp.abs(u)))
        lw = -jnp.exp(-softplus - 0.5)
        asig = jax.nn.sigmoid(a0_ref[d] + _mm_split(_split(low), _split(aup_ref[d]), NN, 3))
        kd = rk * (1.0 + (asig - 1.0) * kaw_ref[...])
        kkr = rk * kkw_ref[...]
        kk = kkr * lax.rsqrt(jnp.maximum(_mm_terms(kkr * kkr, ones_bd, 2), 1e-24))
        a_s = -kk
        b_s = kk * asig
        bon_scr[d, pl.ds(r0, L), :] = _mm_terms(rr * kd * rkw_ref[...], ones_bd, 2) * rv

        cl = _mm_terms(lw, _bf(inc_ref[d]), 3, x_is_rhs=True)
        cl_end = cl[L - 1:L, :] if d == 0 else cl[0:1, :]
        e_neg = jnp.exp(-cl)
        e_tail = jnp.exp(cl_end - cl)
        a_st = stack(a_s * jnp.exp(cl - lw))
        b_st = stack(b_s * e_neg)
        k_st = stack(kd * e_neg)
        r_st = stack(rr * jnp.exp(cl))
        v_st = stack(rv)
        bh_st = stack(b_s * e_tail)
        kh_st = stack(kd * e_tail)
        stc = stcbd_ref[d]
        inc = incbd_ref[d]
        n_m = stc * _mm_split(a_st, b_st, NT, RW_PASSES["gram_n"])
        m1 = stc * _mm_split(a_st, k_st, NT, RW_PASSES["gram"])
        p_m = inc * _mm_split(r_st, b_st, NT, RW_PASSES["gram"])
        q_m = inc * _mm_split(r_st, k_st, NT, RW_PASSES["gram"])
        t_m = eye + n_m
        n_p = _split(n_m)
        for it in range(5):
            n_sq = _mm_split(n_p, n_p, NN, RW_PASSES["inv"])
            n_p = _split(n_sq)
            t_m = t_m + _mm_split(_split(t_m), n_p, NN, RW_PASSES["inv"])
        t_s = _split(t_m)
        u_loc = _mm_split(t_s, _split(_mm_split(_split(m1), v_st, NN, RW_PASSES["loc"])), NN, RW_PASSES["loc"])
        w_t = _mm_split(t_s, a_st, NN, RW_PASSES["loc"])
        qv = _mm_split(_split(q_m), v_st, NN, RW_PASSES["loc"])
        vk = _mm_split(v_st, kh_st, TN, RW_PASSES["loc"])
        s_s = _split(s_in)
        u_m = _mm_split(_split(w_t), s_s, NT, RW_PASSES["seq"]) + u_loc
        u_s = _split(u_m)
        y_bd = (_mm_split(r_st, s_s, NT, RW_PASSES["seq"])
                + _mm_split(_split(p_m), u_s, NN, RW_PASSES["seq"]) + qv)
        y = y_bd[0:L] + y_bd[L:2 * L] + y_bd[2 * L:3 * L] + y_bd[3 * L:4 * L]
        y_scr[d, pl.ds(r0, L), :] = y
        return s_in * jnp.exp(cl_end) + _mm_split(u_s, bh_st, TN, RW_PASSES["seq"]) + vk

    def body(c, carry):
        s_f, s_b = carry
        s_f = chunk(0, pl.multiple_of(c * L, L), s_f)
        s_b = chunk(1, pl.multiple_of((nc - 1 - c) * L, L), s_b)
        return s_f, s_b

    s_f, s_b = lax.fori_loop(0, nc, body, (s0_ref[0], s0_ref[1]))
    sfin_ref[0] = s_f
    sfin_ref[1] = s_b

    inv_n = 1.0 / HEAD_DIM

    def epilogue(c, carry):
        r0 = pl.multiple_of(c * L, L)
        y = y_scr[0, pl.ds(r0, L), :] + y_scr[1, pl.ds(r0, L), :]
        mu = _mm_terms(y, ones_bd, 3) * inv_n
        yc = y - mu
        var = _mm_terms(yc * yc, ones_bd, 2) * inv_n
        yn = yc * lax.rsqrt(var + RWKV_LNX_EPS) * lng_ref[...] + lnb_ref[...]
        gd = p_ref[pl.ds(r0, L), RW_LOW + LANES:RW_LOW + 2 * LANES]
        g = _mm_split(_split(jax.nn.sigmoid(gd)), _split(gup_ref[...]), NN, 3)
        bon = bon_scr[0, pl.ds(r0, L), :] + bon_scr[1, pl.ds(r0, L), :]
        o_ref[pl.ds(r0, L), :] = (yn + bon) * g
        return carry

    lax.fori_loop(0, nc, epilogue, 0)


def _rwkv_call(p_rw, s0_bd, consts, wts, n_seq, seq, row_off):
    inc, inc_bd, stc_bd, blk, eye = consts
    W = RWKV_DIM
    off = row_off // seq
    in_specs = [
        pl.BlockSpec((seq, RW_W), lambda b: (off + b, 0)),
        pl.BlockSpec((None, 2, W, W), lambda b: (b, 0, 0, 0)),
        _full(inc.shape), _full(inc_bd.shape), _full(stc_bd.shape), _full(blk.shape), _full(eye.shape),
    ] + [_full(w.shape) for w in wts]
    return pl.pallas_call(
        functools.partial(_rwkv_kernel, seq=seq),
        grid=(n_seq,),
        in_specs=in_specs,
        out_specs=[
            pl.BlockSpec((seq, W), lambda b: (b, 0)),
            pl.BlockSpec((None, 2, W, W), lambda b: (b, 0, 0, 0)),
        ],
        out_shape=[
            jax.ShapeDtypeStruct((n_seq * seq, W), jnp.float32),
            jax.ShapeDtypeStruct((n_seq, 2, W, W), jnp.float32),
        ],
        scratch_shapes=[pltpu.VMEM((2, seq, W), jnp.float32), pltpu.VMEM((2, seq, W), jnp.float32)],
        compiler_params=_params("parallel"),
        name="rwkv",
    )(p_rw, s0_bd, inc, inc_bd, stc_bd, blk, eye, *wts)


def _rwkv_weights(w0, w_up, a0, a_up, g_up, kk, ka, rk, lng, lnb):
    W = RWKV_DIM
    wup_p = jnp.zeros((2, LANES, W), jnp.float32)
    aup_p = jnp.zeros((2, LANES, W), jnp.float32)
    for d in range(2):
        wup_p = wup_p.at[d, d * W_RANK:(d + 1) * W_RANK].set(w_up[d])
        aup_p = aup_p.at[d, 2 * W_RANK + d * A_RANK:2 * W_RANK + (d + 1) * A_RANK].set(a_up[d])
    gup_p = jnp.zeros((LANES, W), jnp.float32).at[:G_RANK].set(g_up)
    row = lambda v: v.reshape(1, W)
    return (w0.reshape(2, 1, W), wup_p, a0.reshape(2, 1, W), aup_p, gup_p,
            row(kk), row(ka), row(rk), row(lng), row(lnb))


def _out_proj_kernel(attc_ref, attl_ref, rwc_ref, rwl_ref, cv_ref, cvp_ref, cvn_ref, x_ref, mod_ref,
                     w_ref, cw_ref, g_ref, b_ref, o_ref, *, tb, ncb, bps_c, bps):
    i = pl.program_id(0)
    is_ctx = i < ncb
    pos = jnp.where(is_ctx, i % bps_c, (i - ncb) % bps)
    last = jnp.where(is_ctx, bps_c - 1, bps - 1)
    att = jnp.where(is_ctx, attc_ref[...], attl_ref[...])
    rw = jnp.where(is_ctx, rwc_ref[...], rwl_ref[...])

    cb = cv_ref[:, 0:CONV_DIM]
    z = cv_ref[:, CONV_DIM:2 * CONV_DIM] * cv_ref[:, 2 * CONV_DIM:3 * CONV_DIM]
    halo_p = cvp_ref[SUBLANES - 1:SUBLANES, CONV_DIM:2 * CONV_DIM] * cvp_ref[SUBLANES - 1:SUBLANES, 2 * CONV_DIM:]
    halo_n = cvn_ref[0:1, CONV_DIM:2 * CONV_DIM] * cvn_ref[0:1, 2 * CONV_DIM:]
    halo_p = jnp.where(pos > 0, halo_p, 0.0)
    halo_n = jnp.where(pos < last, halo_n, 0.0)
    row = lax.broadcasted_iota(jnp.int32, z.shape, 0)
    z_prev = jnp.where(row == 0, halo_p, pltpu.roll(z, 1, 0))
    z_next = jnp.where(row == tb - 1, halo_n, pltpu.roll(z, tb - 1, 0))
    conv = cb * (cw_ref[0:1, :] * z_prev + cw_ref[1:2, :] * z + cw_ref[2:3, :] * z_next)

    mix = (_mm(_bf(att), w_ref[0:ATT_DIM, :])
           + _mm(_bf(rw), w_ref[ATT_DIM:ATT_DIM + RWKV_DIM, :])
           + _mm(_bf(conv), w_ref[ATT_DIM + RWKV_DIM:, :]))
    o_ref[...] = _layer_norm(ALPHA * x_ref[...] + mod_ref[2:3, :] * mix, g_ref[...], b_ref[...], LN_EPS)


def _out_proj_call(att_c, att_l, rw_c, rw_l, p_cv, x, mod, w_out_b, conv_w, ln_g, ln_b, tb, n_ctx_rows, ctx_seq,
                   lat_seq):
    rows, d = x.shape
    ncb = n_ctx_rows // tb
    bps = lat_seq // tb
    sub = tb // SUBLANES
    n_sub = rows // SUBLANES

    def ctx_map(i):
        return (jnp.minimum(i, ncb - 1), 0)

    def lat_map(i):
        return (jnp.maximum(i - ncb, 0), 0)

    return pl.pallas_call(
        functools.partial(_out_proj_kernel, tb=tb, ncb=ncb, bps_c=ctx_seq // tb, bps=bps),
        grid=(rows // tb,),
        in_specs=[
            pl.BlockSpec((tb, ATT_DIM), ctx_map),
            pl.BlockSpec((tb, ATT_DIM), lat_map),
            pl.BlockSpec((tb, RWKV_DIM), ctx_map),
            pl.BlockSpec((tb, RWKV_DIM), lat_map),
            pl.BlockSpec((tb, CV_W), lambda i: (i, 0)),
            pl.BlockSpec((SUBLANES, CV_W), lambda i: (jnp.maximum(i * sub - 1, 0), 0)),
            pl.BlockSpec((SUBLANES, CV_W), lambda i: (jnp.minimum((i + 1) * sub, n_sub - 1), 0)),
            pl.BlockSpec((tb, d), lambda i: (i, 0)),
            pl.BlockSpec((None, 6, d), lambda i: (_group_of_block(i, ncb, bps), 0, 0)),
            _full((d, d)),
            _full((3, CONV_DIM)),
            _full((1, d)),
            _full((1, d)),
        ],
        out_specs=pl.BlockSpec((tb, d), lambda i: (i, 0)),
        out_shape=jax.ShapeDtypeStruct((rows, d), jnp.float32),
        compiler_params=_params("parallel"),
        name="out_proj",
    )(att_c, att_l, rw_c, rw_l, p_cv, p_cv, p_cv, x, mod, w_out_b, conv_w, ln_g.reshape(1, d), ln_b.reshape(1, d))


def _gelu_tanh(x):
    return 0.5 * x * (1.0 + jnp.tanh(math.sqrt(2.0 / math.pi) * (x + 0.044715 * (x * x * x))))


def _extract_top(work, iota, n_out):
    vals = []
    big = work.shape[0]
    for _ in range(n_out):
        m = jnp.max(work, axis=0, keepdims=True)
        idx = jnp.min(jnp.where(work == m, iota, big), axis=0, keepdims=True)
        work = jnp.where(iota == idx, -jnp.inf, work)
        vals.append(m)
    return vals, work


def _peer_kernel(x_ref, mod_ref, wq_ref, sk_ref, u_ref, vt_ref, g_ref, b_ref, o_ref,
                 ht_scr, s0_scr, e0_scr, s1_scr, e1_scr, a_scr, z_scr, acc_scr, *, tb):
    e_step = pl.program_id(1)
    n_lt = tb // LANES
    K = PEER_TOPK

    @pl.when(e_step == 0)
    def _route():
        h2 = x_ref[...] * (1.0 + mod_ref[4:5, :]) + mod_ref[3:4, :]
        ht = _bf(h2.T)
        ht_scr[...] = ht
        qt = _mm(wq_ref[...], ht)
        for hp in range(2 * PEER_HEADS):
            sc = _mm(sk_ref[hp], qt[hp * PEER_HALF:(hp + 1) * PEER_HALF, :], precision=HIGHEST)
            if hp % 2 == 0:
                s0_scr[hp // 2] = sc
            else:
                s1_scr[hp // 2] = sc

        iota_k = lax.broadcasted_iota(jnp.int32, (PEER_NKEYS, LANES), 0)
        iota_c = lax.broadcasted_iota(jnp.int32, ((SUBLANES + 2) * SUBLANES, LANES), 0)

        def per_tile(j, carry):
            hd = j // n_lt
            lanes = pl.ds(pl.multiple_of((j % n_lt) * LANES, LANES), LANES)
            sc0 = s0_scr[hd, :, lanes]
            sc1 = s1_scr[hd, :, lanes]
            v0, w0 = _extract_top(sc0, iota_k, K)
            v1, w1 = _extract_top(sc1, iota_k, K)
            cand0 = w0 != sc0
            cand1 = w1 != sc1
            lo1 = jnp.concatenate(v1[:SUBLANES], axis=0)
            hi1 = jnp.concatenate(v1[SUBLANES:], axis=0)
            hi0 = jnp.concatenate(v0[SUBLANES:], axis=0)
            pair = jnp.concatenate([v0[a] + lo1 for a in range(SUBLANES)] + [v0[0] + hi1, hi0 + v1[0]], axis=0)
            f, _ = _extract_top(pair, iota_c, K + 1)
            zsum = jnp.zeros_like(f[0])
            for k in range(K):
                zsum = zsum + jnp.exp(f[k] - f[0])
            cut = 0.5 * (f[K - 1] + f[K])
            s0_scr[hd, :, lanes] = jnp.where(cand0, cut - sc0, jnp.inf)
            e0_scr[hd, :, lanes] = jnp.where(cand0, jnp.exp(sc0 - v0[0]), 0.0) / zsum
            e1_scr[hd, :, lanes] = jnp.where(cand1, jnp.exp(sc1 - v1[0]), 0.0)
            s1_scr[hd, :, lanes] = jnp.where(cand1, sc1, -jnp.inf)
            return carry

        lax.fori_loop(0, PEER_HEADS * n_lt, per_tile, 0)
        acc_scr[...] = jnp.zeros_like(acc_scr)
        a_scr[...] = jnp.zeros_like(a_scr)
        z_scr[...] = jnp.zeros_like(z_scr)

    n_groups = pl.num_programs(1) - 2
    slot = e_step % 2
    acc_scr[...] += _mm(vt_ref[...], z_scr[slot])

    g_idx = jnp.clip(e_step - 1, 0, n_groups - 1)
    g_valid = (e_step >= 1) & (e_step <= n_groups)
    keys = pl.ds(pl.multiple_of(g_idx * SUBLANES, SUBLANES), SUBLANES)
    half = PEER_NKEYS // 2
    group = SUBLANES // 2
    tile3 = (half // SUBLANES, SUBLANES, LANES)
    for ln in range(n_lt):
        lanes = slice(ln * LANES, (ln + 1) * LANES)
        th0 = [s0_scr[hd, keys, lanes] for hd in range(PEER_HEADS)]
        e0 = [e0_scr[hd, keys, lanes] for hd in range(PEER_HEADS)]
        for jh in range(2):
            jrows = slice(jh * half, (jh + 1) * half)
            for ig in range(SUBLANES // group):
                g = [jnp.zeros(tile3, jnp.float32) for _ in range(group)]
                for hd in range(PEER_HEADS):
                    s1 = s1_scr[hd, jrows, lanes].reshape(tile3)
                    e1 = e1_scr[hd, jrows, lanes].reshape(tile3)
                    for k in range(group):
                        ii = ig * group + k
                        th = jnp.broadcast_to(th0[hd][ii:ii + 1, :], (SUBLANES, LANES))[None]
                        w0 = jnp.broadcast_to(e0[hd][ii:ii + 1, :], (SUBLANES, LANES))[None]
                        g[k] = g[k] + jnp.where(s1 >= th, e1, 0.0) * w0
                for k in range(group):
                    r0 = (ig * group + k) * PEER_NKEYS + jh * half
                    z = g[k].reshape(half, LANES) * a_scr[1 - slot, r0:r0 + half, lanes]
                    z_scr[1 - slot, r0:r0 + half, lanes] = _bf(jnp.where(g_valid, z, 0.0))

    a_scr[slot] = _gelu_tanh(_mm(u_ref[...], ht_scr[...]))

    @pl.when(e_step == pl.num_programs(1) - 1)
    def _finish():
        out = acc_scr[...].T
        o_ref[...] = _layer_norm(ALPHA * x_ref[...] + mod_ref[5:6, :] * out, g_ref[...], b_ref[...], LN_EPS)


def _peer_call(x, mod, wq_t, subkeys, u_b, vt_b, ln_g, ln_b, tb, n_ctx_rows, lat_seq):
    rows, d = x.shape
    et = SUBLANES * PEER_NKEYS
    ncb = n_ctx_rows // tb
    bps = lat_seq // tb
    n_grp = u_b.shape[0] // et
    nq = wq_t.shape[0]
    f32 = jnp.float32
    return pl.pallas_call(
        functools.partial(_peer_kernel, tb=tb),
        grid=(rows // tb, n_grp + 2),
        in_specs=[
            pl.BlockSpec((tb, d), lambda i, e: (i, 0)),
            pl.BlockSpec((None, 6, d), lambda i, e: (_group_of_block(i, ncb, bps), 0, 0)),
            pl.BlockSpec((nq, d), lambda i, e: (0, 0)),
            pl.BlockSpec(subkeys.shape, lambda i, e: (0, 0, 0)),
            pl.BlockSpec((et, d), lambda i, e: (jnp.minimum(e, n_grp - 1), 0)),
            pl.BlockSpec((d, et), lambda i, e: (0, jnp.clip(e - 2, 0, n_grp - 1))),
            pl.BlockSpec((1, d), lambda i, e: (0, 0)),
            pl.BlockSpec((1, d), lambda i, e: (0, 0)),
        ],
        out_specs=pl.BlockSpec((tb, d), lambda i, e: (i, 0)),
        out_shape=jax.ShapeDtypeStruct((rows, d), f32),
        scratch_shapes=[
            pltpu.VMEM((d, tb), jnp.bfloat16),
            pltpu.VMEM((PEER_HEADS, PEER_NKEYS, tb), f32),
            pltpu.VMEM((PEER_HEADS, PEER_NKEYS, tb), f32),
            pltpu.VMEM((PEER_HEADS, PEER_NKEYS, tb), f32),
            pltpu.VMEM((PEER_HEADS, PEER_NKEYS, tb), f32),
            pltpu.VMEM((2, et, tb), f32),
            pltpu.VMEM((2, et, tb), jnp.bfloat16),
            pltpu.VMEM((d, tb), f32),
        ],
        compiler_params=_params("parallel", "arbitrary"),
        name="peer",
    )(x, mod, wq_t, subkeys, u_b, vt_b, ln_g.reshape(1, d), ln_b.reshape(1, d))


def _rope_tables(seq, tb):
    half = HEAD_DIM // 4
    freqs = ROPE_THETA ** (-jnp.arange(half, dtype=jnp.float32) / half)
    t = jnp.arange(seq)
    cos_parts, sin_parts = [], []
    for pos in (t // GRID_W, t % GRID_W):
        ang = pos.astype(jnp.float32)[:, None] * freqs[None, :]
        c, s = jnp.cos(ang), jnp.sin(ang)
        cos_parts += [c, c]
        sin_parts += [-s, s]
    cos_h = jnp.concatenate(cos_parts, axis=1)
    sin_h = jnp.concatenate(sin_parts, axis=1)
    n_rot = QK_W // HEAD_DIM
    cos_t = jnp.concatenate([jnp.tile(cos_h, (1, n_rot)), jnp.ones((tb, QK_W), jnp.float32)], axis=0)
    sin_t = jnp.concatenate([jnp.tile(sin_h, (1, n_rot)), jnp.zeros((tb, QK_W), jnp.float32)], axis=0)
    return cos_t, sin_t


def _pad_w_in(w):
    split = w.shape[1] - CV_W
    z = jnp.zeros((w.shape[0], IN_PAD - w.shape[1]), w.dtype)
    return jnp.concatenate([w[:, :split], z, w[:, split:]], axis=1)


def _forward(x_prompt, x_sample, cache_k, cache_v, state_rwkv, c, c_ctx, ln_in_g, ln_in_b, w_ada, b_ada,
             w_in, w_out, att_sink, rw_w0, rw_w_up, rw_a0, rw_a_up, rw_g_up, rw_kk, rw_ka, rw_rk,
             rw_lnx_g, rw_lnx_b, conv_w, ln1_g, ln1_b, ln2_g, ln2_b, peer_wq, peer_subkeys, peer_u, peer_v,
             tb_in=512, tb_out=256, tb_peer=512):
    nb_c, seq_c, d = x_prompt.shape
    nb_l, seq_l, _ = x_sample.shape
    depth = w_in.shape[0]
    n_ctx = nb_c * seq_c
    past = cache_k.shape[2]
    W = RWKV_DIM

    x = jnp.concatenate([x_prompt.reshape(n_ctx, d), x_sample.reshape(nb_l * seq_l, d)], axis=0)
    x = _ln_call(x, ln_in_g, ln_in_b, tb_in)

    n_groups = 1 + nb_l
    g_pad = -(-n_groups // SUBLANES) * SUBLANES
    cvec = jnp.concatenate([c_ctx[None], c, jnp.zeros((g_pad - n_groups, d), jnp.float32)], axis=0)
    mod_all = _mod_call(cvec, w_ada, b_ada).reshape(depth, g_pad, 6, d)

    cos_t, sin_t = _rope_tables(seq_l, tb_in)
    consts = tuple(jnp.asarray(a) for a in _rwkv_consts())
    ck_all = cache_k.reshape(nb_l, depth, past, KV_DIM)
    cv_all = cache_v.reshape(nb_l, depth, past, KV_DIM)
    eye_h = jnp.eye(RWKV_HEADS, dtype=jnp.float32)
    s0_bd_all = jnp.einsum('bldhij,hg->bldhigj', state_rwkv, eye_h).reshape(nb_l, depth, 2, W, W)
    s0_zero = jnp.zeros((nb_c, 2, W, W), jnp.float32)

    ks, vs, sts = [], [], []
    for l in range(depth):
        mod = mod_all[l]
        p_att, p_rw, p_cv = _in_proj_call(x, mod, _bf(_pad_w_in(w_in[l])), cos_t, sin_t, tb_in, n_ctx, seq_l)
        att_c = _ctx_attn_call(p_att, att_sink[l], nb_c, seq_c)
        att_l = _lat_attn_call(p_att, ck_all, cv_all, att_sink[l], l, nb_l, seq_l, n_ctx)
        wts = _rwkv_weights(rw_w0[l], rw_w_up[l], rw_a0[l], rw_a_up[l], rw_g_up[l], rw_kk[l], rw_ka[l],
                            rw_rk[l], rw_lnx_g[l], rw_lnx_b[l])
        rw_c, sfin = _rwkv_call(p_rw, s0_zero, consts, wts, nb_c, seq_c, 0)
        rw_l, _ = _rwkv_call(p_rw, s0_bd_all[:, l], consts, wts, nb_l, seq_l, n_ctx)
        x1 = _out_proj_call(att_c, att_l, rw_c, rw_l, p_cv, x, mod, _bf(w_out[l]), conv_w[l],
                            ln1_g[l], ln1_b[l], tb_out, n_ctx, seq_c, seq_l)
        sk = peer_subkeys[l].reshape(2 * PEER_HEADS, PEER_NKEYS, PEER_HALF)
        x = _peer_call(x1, mod, _bf(peer_wq[l].T), sk, _bf(peer_u[l]), _bf(peer_v[l].T),
                       ln2_g[l], ln2_b[l], tb_peer, n_ctx, seq_l)
        ks.append(p_att[:n_ctx, ATT_DIM:ATT_DIM + KV_DIM].reshape(nb_c, seq_c, ATT_KV_HEADS, HEAD_DIM))
        vs.append(p_att[:n_ctx, ATT_DIM + KV_DIM:].reshape(nb_c, seq_c, ATT_KV_HEADS, HEAD_DIM))
        s5 = sfin.reshape(nb_c, 2, RWKV_HEADS, HEAD_DIM, RWKV_HEADS, HEAD_DIM)
        sts.append(jnp.stack([s5[:, :, h, :, h, :] for h in range(RWKV_HEADS)], axis=2))

    y_prompt = x[:n_ctx].reshape(nb_c, seq_c, d)
    y_sample = x[n_ctx:].reshape(nb_l, seq_l, d)
    return (y_prompt, y_sample, jnp.stack(ks, axis=1), jnp.stack(vs, axis=1), jnp.stack(sts, axis=1))


def kernel(x_prompt, x_sample, cache_k, cache_v, state_rwkv, c, c_ctx, ln_in_g, ln_in_b, w_ada, b_ada, w_in, w_out, att_sink, rw_w0, rw_w_up, rw_a0, rw_a_up, rw_g_up, rw_kk, rw_ka, rw_rk, rw_lnx_g, rw_lnx_b, conv_w, ln1_g, ln1_b, ln2_g, ln2_b, peer_wq, peer_subkeys, peer_u, peer_v):
    return _forward(x_prompt, x_sample, cache_k, cache_v, state_rwkv, c, c_ctx, ln_in_g, ln_in_b, w_ada, b_ada,
                    w_in, w_out, att_sink, rw_w0, rw_w_up, rw_a0, rw_a_up, rw_g_up, rw_kk, rw_ka, rw_rk,
                    rw_lnx_g, rw_lnx_b, conv_w, ln1_g, ln1_b, ln2_g, ln2_b, peer_wq, peer_subkeys, peer_u, peer_v)
```

```python
import functools
import math

import jax
import jax.numpy as jnp
import numpy as np
from jax import lax
from jax.experimental import pallas as pl
from jax.experimental.pallas import tpu as pltpu

D_MODEL = 1024
DEPTH = 4
GRID_W = 64
HEAD_DIM = 64
ATT_DIM = 512
RWKV_DIM = 256
CONV_DIM = 256
ATT_HEADS = 8
ATT_KV_HEADS = 2
ATT_GROUP = 4
KV_DIM = 128
WINDOW = 128
ATT_BLOCK = 128
ROPE_THETA = 10000.0
RWKV_HEADS = 4
W_RANK = 32
A_RANK = 32
G_RANK = 64
RWKV_LNX_EPS = 64e-5
PEER_HEADS = 8
PEER_NKEYS = 128
PEER_TOPK = 16
PEER_HALF = 128
LN_EPS = 1e-5
ALPHA = (2 * DEPTH) ** 0.25
NEG = -1e30

LANES = 128
SUBLANES = 8
VMEM_LIMIT = 56 * 1024 * 1024

QK_W = ATT_DIM + KV_DIM
ATT_W = ATT_DIM + 2 * KV_DIM
RW_W = 1024
CV_W = 3 * CONV_DIM
IN_PAD = ATT_W + RW_W + CV_W
RW_LOW = 3 * RWKV_DIM

RW_CHUNK = 64
PEER_MM_ROWS = 256
HIGHEST = lax.Precision.HIGHEST
NN = (((1,), (0,)), ((), ()))
NT = (((1,), (1,)), ((), ()))
TN = (((0,), (0,)), ((), ()))


def _mm(a, b, dims=NN, precision=None):
    return lax.dot_general(a, b, dims, precision=precision, preferred_element_type=jnp.float32)


def _bf(x):
    return x.astype(jnp.bfloat16)


def _split(x):
    hi = _bf(x)
    return hi, _bf(x - hi.astype(jnp.float32))


def _mm_split(a, b, dims, passes):
    out = _mm(a[0], b[0], dims)
    if passes == 3:
        out = out + (_mm(a[0], b[1], dims) + _mm(a[1], b[0], dims))
    return out


def _mm_terms(x, exact, terms, x_is_rhs=False):
    out = None
    rest = x
    for _ in range(terms):
        piece = _bf(rest)
        rest = rest - piece.astype(jnp.float32)
        part = _mm(exact, piece) if x_is_rhs else _mm(piece, exact)
        out = part if out is None else out + part
    return out


RW_PASSES = {"gram_n": 1, "gram": 1, "inv": 1, "loc": 1, "seq": 1}


def _layer_norm(x, g, b, eps):
    mu = jnp.mean(x, axis=-1, keepdims=True)
    xc = x - mu
    var = jnp.mean(xc * xc, axis=-1, keepdims=True)
    return xc * lax.rsqrt(var + eps) * g + b


def _params(*sem):
    return pltpu.CompilerParams(dimension_semantics=sem, vmem_limit_bytes=VMEM_LIMIT)


def _full(shape):
    nd = len(shape)
    return pl.BlockSpec(shape, lambda *_: (0,) * nd)


def _mod_kernel(c_ref, w_ref, b_ref, o_ref):
    cv = c_ref[...]
    s = cv * jax.nn.sigmoid(cv)
    o_ref[...] = _mm(s, w_ref[...], precision=HIGHEST) + b_ref[...]


def _mod_call(cvec, w_ada, b_ada):
    depth, d, n6 = w_ada.shape
    rows = cvec.shape[0]
    nt = n6 // d
    return pl.pallas_call(
        _mod_kernel,
        grid=(depth, nt),
        in_specs=[
            pl.BlockSpec((rows, d), lambda l, n: (0, 0)),
            pl.BlockSpec((None, d, d), lambda l, n: (l, 0, n)),
            pl.BlockSpec((None, 1, d), lambda l, n: (l, 0, n)),
        ],
        out_specs=pl.BlockSpec((None, rows, d), lambda l, n: (l, 0, n)),
        out_shape=jax.ShapeDtypeStruct((depth, rows, n6), jnp.float32),
        compiler_params=_params("parallel", "parallel"),
        name="adaln_mod",
    )(cvec, w_ada, b_ada.reshape(depth, 1, n6))


def _ln_kernel(x_ref, g_ref, b_ref, o_ref):
    o_ref[...] = _layer_norm(x_ref[...], g_ref[...], b_ref[...], LN_EPS)


def _ln_call(x, g, b, tb):
    rows, d = x.shape
    return pl.pallas_call(
        _ln_kernel,
        grid=(rows // tb,),
        in_specs=[pl.BlockSpec((tb, d), lambda i: (i, 0)), _full((1, d)), _full((1, d))],
        out_specs=pl.BlockSpec((tb, d), lambda i: (i, 0)),
        out_shape=jax.ShapeDtypeStruct((rows, d), jnp.float32),
        compiler_params=_params("parallel"),
        name="ln_in",
    )(x, g.reshape(1, d), b.reshape(1, d))


def _group_of_block(i, n_ctx_blocks, blocks_per_lat_seq):
    return jnp.where(i < n_ctx_blocks, 0, 1 + (i - n_ctx_blocks) // blocks_per_lat_seq)


def _in_proj_kernel(x_ref, mod_ref, w_ref, cos_ref, sin_ref, att_ref, rw_ref, cv_ref):
    x = x_ref[...]
    h = x * (1.0 + mod_ref[1:2, :]) + mod_ref[0:1, :]
    p = _mm(_bf(h), w_ref[...])
    qk = p[:, :QK_W]
    lane = lax.broadcasted_iota(jnp.int32, qk.shape, 1)
    partner = jnp.where((lane % 32) < 16,
                        pltpu.roll(qk, QK_W - 16, 1),
                        pltpu.roll(qk, 16, 1))
    att_ref[:, :QK_W] = qk * cos_ref[...] + partner * sin_ref[...]
    att_ref[:, QK_W:] = p[:, QK_W:ATT_W]
    rw_ref[...] = p[:, ATT_W:ATT_W + RW_W]
    cv_ref[...] = p[:, ATT_W + RW_W:]


def _in_proj_call(x, mod, w_in_p, cos_t, sin_t, tb, n_ctx_rows, lat_seq):
    rows, d = x.shape
    ncb = n_ctx_rows // tb
    bps = lat_seq // tb

    def mod_map(i):
        return (_group_of_block(i, ncb, bps), 0, 0)

    def rope_map(i):
        return (jnp.where(i < ncb, bps, (i - ncb) % bps), 0)

    return pl.pallas_call(
        _in_proj_kernel,
        grid=(rows // tb,),
        in_specs=[
            pl.BlockSpec((tb, d), lambda i: (i, 0)),
            pl.BlockSpec((None, 6, d), mod_map),
            _full((d, IN_PAD)),
            pl.BlockSpec((tb, QK_W), rope_map),
            pl.BlockSpec((tb, QK_W), rope_map),
        ],
        out_specs=[
            pl.BlockSpec((tb, ATT_W), lambda i: (i, 0)),
            pl.BlockSpec((tb, RW_W), lambda i: (i, 0)),
            pl.BlockSpec((tb, CV_W), lambda i: (i, 0)),
        ],
        out_shape=[
            jax.ShapeDtypeStruct((rows, ATT_W), jnp.float32),
            jax.ShapeDtypeStruct((rows, RW_W), jnp.float32),
            jax.ShapeDtypeStruct((rows, CV_W), jnp.float32),
        ],
        compiler_params=_params("parallel"),
        name="in_proj",
    )(x, mod, w_in_p, cos_t, sin_t)


def _ctx_attn_kernel(sink_ref, p_ref, o_ref):
    scale = HEAD_DIM ** -0.5
    for kv in range(ATT_KV_HEADS):
        k = _bf(p_ref[:, ATT_DIM + kv * HEAD_DIM:ATT_DIM + (kv + 1) * HEAD_DIM])
        v = _bf(p_ref[:, ATT_DIM + KV_DIM + kv * HEAD_DIM:ATT_DIM + KV_DIM + (kv + 1) * HEAD_DIM])
        for g in range(ATT_GROUP):
            hd = kv * ATT_GROUP + g
            q = _bf(p_ref[:, hd * HEAD_DIM:(hd + 1) * HEAD_DIM])
            s = _mm(q, k, NT) * scale
            sink = sink_ref[hd]
            m = jnp.maximum(jnp.max(s, axis=-1, keepdims=True), sink)
            e = jnp.exp(s - m)
            den = jnp.sum(e, axis=-1, keepdims=True) + jnp.exp(sink - m)
            o_ref[:, hd * HEAD_DIM:(hd + 1) * HEAD_DIM] = _mm(_bf(e), v) / den


def _ctx_attn_call(p_att, sink, n_seq, seq):
    return pl.pallas_call(
        _ctx_attn_kernel,
        grid=(n_seq,),
        in_specs=[
            pl.BlockSpec(memory_space=pltpu.SMEM),
            pl.BlockSpec((seq, ATT_W), lambda b: (b, 0)),
        ],
        out_specs=pl.BlockSpec((seq, ATT_DIM), lambda b: (b, 0)),
        out_shape=jax.ShapeDtypeStruct((n_seq * seq, ATT_DIM), jnp.float32),
        compiler_params=_params("parallel"),
        name="ctx_attn",
    )(sink, p_att)


def _lat_attn_kernel(sink_ref, own_ref, prev_ref, next_ref, ck_ref, cv_ref, o_ref, *, n_blocks):
    n = pl.program_id(1)
    scale = HEAD_DIM ** -0.5
    qi = lax.broadcasted_iota(jnp.int32, (ATT_BLOCK, 3 * ATT_BLOCK), 0)
    kj = lax.broadcasted_iota(jnp.int32, (ATT_BLOCK, 3 * ATT_BLOCK), 1)
    rel = kj - ATT_BLOCK - qi
    kpos = (n - 1) * ATT_BLOCK + kj
    mask = (jnp.abs(rel) <= WINDOW) & (kpos >= 0) & (kpos < n_blocks * ATT_BLOCK)
    mask4 = jnp.concatenate([mask] * ATT_GROUP, axis=0)
    row_group = lax.broadcasted_iota(jnp.int32, (ATT_GROUP * ATT_BLOCK, 1), 0) // ATT_BLOCK
    for kv in range(ATT_KV_HEADS):
        ks = slice(ATT_DIM + kv * HEAD_DIM, ATT_DIM + (kv + 1) * HEAD_DIM)
        vs = slice(ATT_DIM + KV_DIM + kv * HEAD_DIM, ATT_DIM + KV_DIM + (kv + 1) * HEAD_DIM)
        kw = _bf(jnp.concatenate([prev_ref[:, ks], own_ref[:, ks], next_ref[:, ks]], axis=0))
        vw = _bf(jnp.concatenate([prev_ref[:, vs], own_ref[:, vs], next_ref[:, vs]], axis=0))
        ck = _bf(ck_ref[:, kv * HEAD_DIM:(kv + 1) * HEAD_DIM])
        cv = _bf(cv_ref[:, kv * HEAD_DIM:(kv + 1) * HEAD_DIM])
        q4 = _bf(jnp.concatenate(
            [own_ref[:, (kv * ATT_GROUP + g) * HEAD_DIM:(kv * ATT_GROUP + g + 1) * HEAD_DIM]
             for g in range(ATT_GROUP)], axis=0))
        s_ctx = _mm(q4, ck, NT) * scale
        s_win = jnp.where(mask4, _mm(q4, kw, NT) * scale, NEG)
        sink = jnp.full((ATT_GROUP * ATT_BLOCK, 1), sink_ref[kv * ATT_GROUP], jnp.float32)
        for g in range(1, ATT_GROUP):
            sink = jnp.where(row_group == g, sink_ref[kv * ATT_GROUP + g], sink)
        m = jnp.maximum(jnp.maximum(jnp.max(s_ctx, axis=-1, keepdims=True),
                                    jnp.max(s_win, axis=-1, keepdims=True)), sink)
        e_ctx = jnp.exp(s_ctx - m)
        e_win = jnp.exp(s_win - m)
        den = (jnp.sum(e_ctx, axis=-1, keepdims=True) + jnp.sum(e_win, axis=-1, keepdims=True)
               + jnp.exp(sink - m))
        o4 = (_mm(_bf(e_ctx), cv) + _mm(_bf(e_win), vw)) / den
        for g in range(ATT_GROUP):
            hd = kv * ATT_GROUP + g
            o_ref[:, hd * HEAD_DIM:(hd + 1) * HEAD_DIM] = o4[g * ATT_BLOCK:(g + 1) * ATT_BLOCK]


def _lat_attn_call(p_att, ck, cv, sink, layer, n_seq, seq, row_off):
    nb = seq // ATT_BLOCK
    off = row_off // ATT_BLOCK
    past = ck.shape[2]

    def own(b, n):
        return (off + b * nb + n, 0)

    def prev(b, n):
        return (off + b * nb + jnp.maximum(n - 1, 0), 0)

    def nxt(b, n):
        return (off + b * nb + jnp.minimum(n + 1, nb - 1), 0)

    return pl.pallas_call(
        functools.partial(_lat_attn_kernel, n_blocks=nb),
        grid=(n_seq, nb),
        in_specs=[
            pl.BlockSpec(memory_space=pltpu.SMEM),
            pl.BlockSpec((ATT_BLOCK, ATT_W), own),
            pl.BlockSpec((ATT_BLOCK, ATT_W), prev),
            pl.BlockSpec((ATT_BLOCK, ATT_W), nxt),
            pl.BlockSpec((None, None, past, KV_DIM), lambda b, n: (b, layer, 0, 0)),
            pl.BlockSpec((None, None, past, KV_DIM), lambda b, n: (b, layer, 0, 0)),
        ],
        out_specs=pl.BlockSpec((ATT_BLOCK, ATT_DIM), lambda b, n: (b * nb + n, 0)),
        out_shape=jax.ShapeDtypeStruct((n_seq * seq, ATT_DIM), jnp.float32),
        compiler_params=_params("parallel", "parallel"),
        name="lat_attn",
    )(sink, p_att, p_att, p_att, ck, cv)


def _rwkv_consts():
    L, H, W = RW_CHUNK, RWKV_HEADS, RWKV_DIM
    t = np.arange(L)
    inc = np.stack([(t[None, :] <= t[:, None]), (t[None, :] >= t[:, None])]).astype(np.float32)
    stc = np.stack([(t[None, :] < t[:, None]), (t[None, :] > t[:, None])]).astype(np.float32)
    eye_h = np.eye(H, dtype=np.float32)
    inc_bd = np.stack([np.kron(eye_h, inc[d]) for d in range(2)])
    stc_bd = np.stack([np.kron(eye_h, stc[d]) for d in range(2)])
    blk = np.kron(eye_h, np.ones((L, HEAD_DIM), np.float32))
    eye = np.eye(W, dtype=np.float32)
    return inc, inc_bd, stc_bd, blk, eye


def _rwkv_kernel(p_ref, s0_ref, inc_ref, incbd_ref, stcbd_ref, blk_ref, eye_ref,
                 w0_ref, wup_ref, a0_ref, aup_ref, gup_ref, kkw_ref, kaw_ref, rkw_ref, lng_ref, lnb_ref,
                 o_ref, sfin_ref, y_scr, bon_scr, *, seq):
    L = RW_CHUNK
    nc = seq // L
    blk = _bf(blk_ref[...])
    ones_bd = blk
    eye = eye_ref[...]

    def stack(x):
        return tuple(jnp.concatenate([z] * RWKV_HEADS, axis=0) * blk for z in _split(x))

    def chunk(d, r0, s_in):
        x = p_ref[pl.ds(r0, L), :]
        rr = x[:, 0:RWKV_DIM]
        rk = x[:, RWKV_DIM:2 * RWKV_DIM]
        rv = x[:, 2 * RWKV_DIM:3 * RWKV_DIM]
        low = x[:, RW_LOW:RW_LOW + LANES]
        zw = w0_ref[d] + _mm_split(_split(jnp.tanh(low)), _split(wup_ref[d]), NN, 3)
        u = -zw
        softplus = jnp.maximum(u, 0.0) + jnp.log1p(jnp.exp(-jnp.abs(u)))
        lw = -jnp.exp(-softplus - 0.5)
        asig = jax.nn.sigmoid(a0_ref[d] + _mm_split(_split(low), _split(aup_ref[d]), NN, 3))
        kd = rk * (1.0 + (asig - 1.0) * kaw_ref[...])
        kkr = rk * kkw_ref[...]
        kk = kkr * lax.rsqrt(jnp.maximum(_mm_terms(kkr * kkr, ones_bd, 2), 1e-24))
        a_s = -kk
        b_s = kk * asig
        bon_scr[d, pl.ds(r0, L), :] = _mm_terms(rr * kd * rkw_ref[...], ones_bd, 2) * rv

        cl = _mm_terms(lw, _bf(inc_ref[d]), 3, x_is_rhs=True)
        cl_end = cl[L - 1:L, :] if d == 0 else cl[0:1, :]
        e_neg = jnp.exp(-cl)
        e_tail = jnp.exp(cl_end - cl)
        a_st = stack(a_s * jnp.exp(cl - lw))
        b_st = stack(b_s * e_neg)
        k_st = stack(kd * e_neg)
        r_st = stack(rr * jnp.exp(cl))
        v_st = stack(rv)
        bh_st = stack(b_s * e_tail)
        kh_st = stack(kd * e_tail)
        stc = stcbd_ref[d]
        inc = incbd_ref[d]
        n_m = stc * _mm_split(a_st, b_st, NT, RW_PASSES["gram_n"])
        m1 = stc * _mm_split(a_st, k_st, NT, RW_PASSES["gram"])
        p_m = inc * _mm_split(r_st, b_st, NT, RW_PASSES["gram"])
        q_m = inc * _mm_split(r_st, k_st, NT, RW_PASSES["gram"])
        t_m = eye + n_m
        n_p = _split(n_m)
        for it in range(5):
            n_sq = _mm_split(n_p, n_p, NN, RW_PASSES["inv"])
            n_p = _split(n_sq)
            t_m = t_m + _mm_split(_split(t_m), n_p, NN, RW_PASSES["inv"])
        t_s = _split(t_m)
        u_loc = _mm_split(t_s, _split(_mm_split(_split(m1), v_st, NN, RW_PASSES["loc"])), NN, RW_PASSES["loc"])
        w_t = _mm_split(t_s, a_st, NN, RW_PASSES["loc"])
        qv = _mm_split(_split(q_m), v_st, NN, RW_PASSES["loc"])
        vk = _mm_split(v_st, kh_st, TN, RW_PASSES["loc"])
        s_s = _split(s_in)
        u_m = _mm_split(_split(w_t), s_s, NT, RW_PASSES["seq"]) + u_loc
        u_s = _split(u_m)
        y_bd = (_mm_split(r_st, s_s, NT, RW_PASSES["seq"])
                + _mm_split(_split(p_m), u_s, NN, RW_PASSES["seq"]) + qv)
        y = y_bd[0:L] + y_bd[L:2 * L] + y_bd[2 * L:3 * L] + y_bd[3 * L:4 * L]
        y_scr[d, pl.ds(r0, L), :] = y
        return s_in * jnp.exp(cl_end) + _mm_split(u_s, bh_st, TN, RW_PASSES["seq"]) + vk

    def body(c, carry):
        s_f, s_b = carry
        s_f = chunk(0, pl.multiple_of(c * L, L), s_f)
        s_b = chunk(1, pl.multiple_of((nc - 1 - c) * L, L), s_b)
        return s_f, s_b

    s_f, s_b = lax.fori_loop(0, nc, body, (s0_ref[0], s0_ref[1]))
    sfin_ref[0] = s_f
    sfin_ref[1] = s_b

    inv_n = 1.0 / HEAD_DIM

    def epilogue(c, carry):
        r0 = pl.multiple_of(c * L, L)
        y = y_scr[0, pl.ds(r0, L), :] + y_scr[1, pl.ds(r0, L), :]
        mu = _mm_terms(y, ones_bd, 3) * inv_n
        yc = y - mu
        var = _mm_terms(yc * yc, ones_bd, 2) * inv_n
        yn = yc * lax.rsqrt(var + RWKV_LNX_EPS) * lng_ref[...] + lnb_ref[...]
        gd = p_ref[pl.ds(r0, L), RW_LOW + LANES:RW_LOW + 2 * LANES]
        g = _mm_split(_split(jax.nn.sigmoid(gd)), _split(gup_ref[...]), NN, 3)
        bon = bon_scr[0, pl.ds(r0, L), :] + bon_scr[1, pl.ds(r0, L), :]
        o_ref[pl.ds(r0, L), :] = (yn + bon) * g
        return carry

    lax.fori_loop(0, nc, epilogue, 0)


def _rwkv_call(p_rw, s0_bd, consts, wts, n_seq, seq, row_off):
    inc, inc_bd, stc_bd, blk, eye = consts
    W = RWKV_DIM
    off = row_off // seq
    in_specs = [
        pl.BlockSpec((seq, RW_W), lambda b: (off + b, 0)),
        pl.BlockSpec((None, 2, W, W), lambda b: (b, 0, 0, 0)),
        _full(inc.shape), _full(inc_bd.shape), _full(stc_bd.shape), _full(blk.shape), _full(eye.shape),
    ] + [_full(w.shape) for w in wts]
    return pl.pallas_call(
        functools.partial(_rwkv_kernel, seq=seq),
        grid=(n_seq,),
        in_specs=in_specs,
        out_specs=[
            pl.BlockSpec((seq, W), lambda b: (b, 0)),
            pl.BlockSpec((None, 2, W, W), lambda b: (b, 0, 0, 0)),
        ],
        out_shape=[
            jax.ShapeDtypeStruct((n_seq * seq, W), jnp.float32),
            jax.ShapeDtypeStruct((n_seq, 2, W, W), jnp.float32),
        ],
        scratch_shapes=[pltpu.VMEM((2, seq, W), jnp.float32), pltpu.VMEM((2, seq, W), jnp.float32)],
        compiler_params=_params("parallel"),
        name="rwkv",
    )(p_rw, s0_bd, inc, inc_bd, stc_bd, blk, eye, *wts)


def _rwkv_weights(w0, w_up, a0, a_up, g_up, kk, ka, rk, lng, lnb):
    W = RWKV_DIM
    wup_p = jnp.zeros((2, LANES, W), jnp.float32)
    aup_p = jnp.zeros((2, LANES, W), jnp.float32)
    for d in range(2):
        wup_p = wup_p.at[d, d * W_RANK:(d + 1) * W_RANK].set(w_up[d])
        aup_p = aup_p.at[d, 2 * W_RANK + d * A_RANK:2 * W_RANK + (d + 1) * A_RANK].set(a_up[d])
    gup_p = jnp.zeros((LANES, W), jnp.float32).at[:G_RANK].set(g_up)
    row = lambda v: v.reshape(1, W)
    return (w0.reshape(2, 1, W), wup_p, a0.reshape(2, 1, W), aup_p, gup_p,
            row(kk), row(ka), row(rk), row(lng), row(lnb))


def _out_proj_kernel(attc_ref, attl_ref, rwc_ref, rwl_ref, cv_ref, cvp_ref, cvn_ref, x_ref, mod_ref,
                     w_ref, cw_ref, g_ref, b_ref, o_ref, *, tb, ncb, bps_c, bps):
    i = pl.program_id(0)
    is_ctx = i < ncb
    pos = jnp.where(is_ctx, i % bps_c, (i - ncb) % bps)
    last = jnp.where(is_ctx, bps_c - 1, bps - 1)
    att = jnp.where(is_ctx, attc_ref[...], attl_ref[...])
    rw = jnp.where(is_ctx, rwc_ref[...], rwl_ref[...])

    cb = cv_ref[:, 0:CONV_DIM]
    z = cv_ref[:, CONV_DIM:2 * CONV_DIM] * cv_ref[:, 2 * CONV_DIM:3 * CONV_DIM]
    halo_p = cvp_ref[SUBLANES - 1:SUBLANES, CONV_DIM:2 * CONV_DIM] * cvp_ref[SUBLANES - 1:SUBLANES, 2 * CONV_DIM:]
    halo_n = cvn_ref[0:1, CONV_DIM:2 * CONV_DIM] * cvn_ref[0:1, 2 * CONV_DIM:]
    halo_p = jnp.where(pos > 0, halo_p, 0.0)
    halo_n = jnp.where(pos < last, halo_n, 0.0)
    row = lax.broadcasted_iota(jnp.int32, z.shape, 0)
    z_prev = jnp.where(row == 0, halo_p, pltpu.roll(z, 1, 0))
    z_next = jnp.where(row == tb - 1, halo_n, pltpu.roll(z, tb - 1, 0))
    conv = cb * (cw_ref[0:1, :] * z_prev + cw_ref[1:2, :] * z + cw_ref[2:3, :] * z_next)

    mix = (_mm(_bf(att), w_ref[0:ATT_DIM, :])
           + _mm(_bf(rw), w_ref[ATT_DIM:ATT_DIM + RWKV_DIM, :])
           + _mm(_bf(conv), w_ref[ATT_DIM + RWKV_DIM:, :]))
    o_ref[...] = _layer_norm(ALPHA * x_ref[...] + mod_ref[2:3, :] * mix, g_ref[...], b_ref[...], LN_EPS)


def _out_proj_call(att_c, att_l, rw_c, rw_l, p_cv, x, mod, w_out_b, conv_w, ln_g, ln_b, tb, n_ctx_rows, ctx_seq,
                   lat_seq):
    rows, d = x.shape
    ncb = n_ctx_rows // tb
    bps = lat_seq // tb
    sub = tb // SUBLANES
    n_sub = rows // SUBLANES

    def ctx_map(i):
        return (jnp.minimum(i, ncb - 1), 0)

    def lat_map(i):
        return (jnp.maximum(i - ncb, 0), 0)

    return pl.pallas_call(
        functools.partial(_out_proj_kernel, tb=tb, ncb=ncb, bps_c=ctx_seq // tb, bps=bps),
        grid=(rows // tb,),
        in_specs=[
            pl.BlockSpec((tb, ATT_DIM), ctx_map),
            pl.BlockSpec((tb, ATT_DIM), lat_map),
            pl.BlockSpec((tb, RWKV_DIM), ctx_map),
            pl.BlockSpec((tb, RWKV_DIM), lat_map),
            pl.BlockSpec((tb, CV_W), lambda i: (i, 0)),
            pl.BlockSpec((SUBLANES, CV_W), lambda i: (jnp.maximum(i * sub - 1, 0), 0)),
            pl.BlockSpec((SUBLANES, CV_W), lambda i: (jnp.minimum((i + 1) * sub, n_sub - 1), 0)),
            pl.BlockSpec((tb, d), lambda i: (i, 0)),
            pl.BlockSpec((None, 6, d), lambda i: (_group_of_block(i, ncb, bps), 0, 0)),
            _full((d, d)),
            _full((3, CONV_DIM)),
            _full((1, d)),
            _full((1, d)),
        ],
        out_specs=pl.BlockSpec((tb, d), lambda i: (i, 0)),
        out_shape=jax.ShapeDtypeStruct((rows, d), jnp.float32),
        compiler_params=_params("parallel"),
        name="out_proj",
    )(att_c, att_l, rw_c, rw_l, p_cv, p_cv, p_cv, x, mod, w_out_b, conv_w, ln_g.reshape(1, d), ln_b.reshape(1, d))


def _gelu_tanh(x):
    return 0.5 * x * (1.0 + jnp.tanh(math.sqrt(2.0 / math.pi) * (x + 0.044715 * (x * x * x))))


def _extract_top(work, iota, n_out, want_rank=False):
    vals = []
    big = work.shape[0]
    rank = jnp.full(work.shape, float(big), jnp.float32) if want_rank else None
    for k in range(n_out):
        m = jnp.max(work, axis=0, keepdims=True)
        idx = jnp.min(jnp.where(work == m, iota, big), axis=0, keepdims=True)
        sel = iota == idx
        work = jnp.where(sel, -jnp.inf, work)
        if want_rank:
            rank = jnp.where(sel, float(k), rank)
        vals.append(m)
    return vals, work, rank


def _peer_kernel(x_ref, mod_ref, wq_ref, sk_ref, u_ref, vt_ref, g_ref, b_ref, o_ref,
                 ht_scr, s0_scr, e0_scr, s1_scr, r1_scr, e1_scr, a_scr, z_scr, acc_scr, *, tb):
    e_step = pl.program_id(1)
    n_lt = tb // LANES
    K = PEER_TOPK

    @pl.when(e_step == 0)
    def _route():
        h2 = x_ref[...] * (1.0 + mod_ref[4:5, :]) + mod_ref[3:4, :]
        ht = _bf(h2.T)
        ht_scr[...] = ht
        qt = _mm(wq_ref[...], ht)
        for hp in range(2 * PEER_HEADS):
            sc = _mm(sk_ref[hp], qt[hp * PEER_HALF:(hp + 1) * PEER_HALF, :], precision=HIGHEST)
            if hp % 2 == 0:
                s0_scr[hp // 2] = sc
            else:
                s1_scr[hp // 2] = sc

        iota_k = lax.broadcasted_iota(jnp.int32, (PEER_NKEYS, LANES), 0)
        iota_c = lax.broadcasted_iota(jnp.int32, ((SUBLANES + 2) * SUBLANES, LANES), 0)

        def per_tile(j, carry):
            hd = j // n_lt
            lanes = pl.ds(pl.multiple_of((j % n_lt) * LANES, LANES), LANES)
            sc0 = s0_scr[hd, :, lanes]
            sc1 = s1_scr[hd, :, lanes]
            v0, w0, _ = _extract_top(sc0, iota_k, K)
            v1, w1, rank1 = _extract_top(sc1, iota_k, K, want_rank=True)
            cand0 = w0 != sc0
            cand1 = w1 != sc1
            lo1 = jnp.concatenate(v1[:SUBLANES], axis=0)
            hi1 = jnp.concatenate(v1[SUBLANES:], axis=0)
            hi0 = jnp.concatenate(v0[SUBLANES:], axis=0)
            pair = jnp.concatenate([v0[a] + lo1 for a in range(SUBLANES)] + [v0[0] + hi1, hi0 + v1[0]], axis=0)
            f, _, _ = _extract_top(pair, iota_c, K + 1)
            zsum = jnp.zeros_like(f[0])
            for k in range(K):
                zsum = zsum + jnp.exp(f[k] - f[0])
            th = 0.5 * (f[K - 1] + f[K]) - sc0
            count0 = jnp.zeros_like(sc0)
            for b in range(K):
                count0 = count0 + jnp.where(v1[b] > th, 1.0, 0.0)
            s0_scr[hd, :, lanes] = jnp.where(cand0, count0, 0.0)
            e0_scr[hd, :, lanes] = jnp.where(cand0, jnp.exp(sc0 - v0[0]), 0.0) / zsum
            e1_scr[hd, :, lanes] = _bf(jnp.where(cand1, jnp.exp(sc1 - v1[0]), 0.0))
            r1_scr[hd, :, lanes] = _bf(rank1)
            return carry

        lax.fori_loop(0, PEER_HEADS * n_lt, per_tile, 0)
        acc_scr[...] = jnp.zeros_like(acc_scr)

    a_scr[...] = _gelu_tanh(_mm(u_ref[...], ht_scr[...]))

    keys = pl.ds(pl.multiple_of(e_step * SUBLANES, SUBLANES), SUBLANES)
    half = PEER_NKEYS // 2
    group = SUBLANES // 2
    packed = 2 * SUBLANES
    tile3 = (half // packed, packed, LANES)
    for ln in range(n_lt):
        lanes = slice(ln * LANES, (ln + 1) * LANES)
        c0 = [s0_scr[hd, keys, lanes] for hd in range(PEER_HEADS)]
        e0 = [e0_scr[hd, keys, lanes] for hd in range(PEER_HEADS)]
        for jh in range(2):
            jrows = slice(jh * half, (jh + 1) * half)
            for ig in range(SUBLANES // group):
                g = [jnp.zeros(tile3, jnp.bfloat16) for _ in range(group)]
                for hd in range(PEER_HEADS):
                    r1 = r1_scr[hd, jrows, lanes].reshape(tile3)
                    e1 = e1_scr[hd, jrows, lanes].reshape(tile3)
                    for k in range(group):
                        ii = ig * group + k
                        cnt = _bf(jnp.broadcast_to(c0[hd][ii:ii + 1, :], tile3[1:]))[None]
                        w0 = _bf(jnp.broadcast_to(e0[hd][ii:ii + 1, :], tile3[1:]))[None]
                        g[k] = g[k] + jnp.where(r1 < cnt, e1, jnp.zeros_like(e1)) * w0
                for k in range(group):
                    r0 = (ig * group + k) * PEER_NKEYS + jh * half
                    z_scr[r0:r0 + half, lanes] = g[k].reshape(half, LANES) * _bf(a_scr[r0:r0 + half, lanes])

    acc_scr[...] += _mm(vt_ref[...], z_scr[...])

    @pl.when(e_step == pl.num_programs(1) - 1)
    def _finish():
        out = acc_scr[...].T
        o_ref[...] = _layer_norm(ALPHA * x_ref[...] + mod_ref[5:6, :] * out, g_ref[...], b_ref[...], LN_EPS)


def _peer_call(x, mod, wq_t, subkeys, u_b, vt_b, ln_g, ln_b, tb, n_ctx_rows, lat_seq):
    rows, d = x.shape
    et = SUBLANES * PEER_NKEYS
    ncb = n_ctx_rows // tb
    bps = lat_seq // tb
    nq = wq_t.shape[0]
    f32 = jnp.float32
    return pl.pallas_call(
        functools.partial(_peer_kernel, tb=tb),
        grid=(rows // tb, u_b.shape[0] // et),
        in_specs=[
            pl.BlockSpec((tb, d), lambda i, e: (i, 0)),
            pl.BlockSpec((None, 6, d), lambda i, e: (_group_of_block(i, ncb, bps), 0, 0)),
            pl.BlockSpec((nq, d), lambda i, e: (0, 0), pipeline_mode=pl.Buffered(1)),
            pl.BlockSpec(subkeys.shape, lambda i, e: (0, 0, 0), pipeline_mode=pl.Buffered(1)),
            pl.BlockSpec((et, d), lambda i, e: (e, 0)),
            pl.BlockSpec((d, et), lambda i, e: (0, e)),
            pl.BlockSpec((1, d), lambda i, e: (0, 0)),
            pl.BlockSpec((1, d), lambda i, e: (0, 0)),
        ],
        out_specs=pl.BlockSpec((tb, d), lambda i, e: (i, 0)),
        out_shape=jax.ShapeDtypeStruct((rows, d), f32),
        scratch_shapes=[
            pltpu.VMEM((d, tb), jnp.bfloat16),
            pltpu.VMEM((PEER_HEADS, PEER_NKEYS, tb), f32),
            pltpu.VMEM((PEER_HEADS, PEER_NKEYS, tb), f32),
            pltpu.VMEM((PEER_HEADS, PEER_NKEYS, tb), f32),
            pltpu.VMEM((PEER_HEADS, PEER_NKEYS, tb), jnp.bfloat16),
            pltpu.VMEM((PEER_HEADS, PEER_NKEYS, tb), jnp.bfloat16),
            pltpu.VMEM((et, tb), f32),
            pltpu.VMEM((et, tb), jnp.bfloat16),
            pltpu.VMEM((d, tb), f32),
        ],
        compiler_params=_params("parallel", "arbitrary"),
        name="peer",
    )(x, mod, wq_t, subkeys, u_b, vt_b, ln_g.reshape(1, d), ln_b.reshape(1, d))


def _rope_tables(seq, tb):
    half = HEAD_DIM // 4
    freqs = ROPE_THETA ** (-jnp.arange(half, dtype=jnp.float32) / half)
    t = jnp.arange(seq)
    cos_parts, sin_parts = [], []
    for pos in (t // GRID_W, t % GRID_W):
        ang = pos.astype(jnp.float32)[:, None] * freqs[None, :]
        c, s = jnp.cos(ang), jnp.sin(ang)
        cos_parts += [c, c]
        sin_parts += [-s, s]
    cos_h = jnp.concatenate(cos_parts, axis=1)
    sin_h = jnp.concatenate(sin_parts, axis=1)
    n_rot = QK_W // HEAD_DIM
    cos_t = jnp.concatenate([jnp.tile(cos_h, (1, n_rot)), jnp.ones((tb, QK_W), jnp.float32)], axis=0)
    sin_t = jnp.concatenate([jnp.tile(sin_h, (1, n_rot)), jnp.zeros((tb, QK_W), jnp.float32)], axis=0)
    return cos_t, sin_t


def _pad_w_in(w):
    split = w.shape[1] - CV_W
    z = jnp.zeros((w.shape[0], IN_PAD - w.shape[1]), w.dtype)
    return jnp.concatenate([w[:, :split], z, w[:, split:]], axis=1)


def _forward(x_prompt, x_sample, cache_k, cache_v, state_rwkv, c, c_ctx, ln_in_g, ln_in_b, w_ada, b_ada,
             w_in, w_out, att_sink, rw_w0, rw_w_up, rw_a0, rw_a_up, rw_g_up, rw_kk, rw_ka, rw_rk,
             rw_lnx_g, rw_lnx_b, conv_w, ln1_g, ln1_b, ln2_g, ln2_b, peer_wq, peer_subkeys, peer_u, peer_v,
             tb_in=512, tb_out=256, tb_peer=512):
    nb_c, seq_c, d = x_prompt.shape
    nb_l, seq_l, _ = x_sample.shape
    depth = w_in.shape[0]
    n_ctx = nb_c * seq_c
    past = cache_k.shape[2]
    W = RWKV_DIM

    x = jnp.concatenate([x_prompt.reshape(n_ctx, d), x_sample.reshape(nb_l * seq_l, d)], axis=0)
    x = _ln_call(x, ln_in_g, ln_in_b, tb_in)

    n_groups = 1 + nb_l
    g_pad = -(-n_groups // SUBLANES) * SUBLANES
    cvec = jnp.concatenate([c_ctx[None], c, jnp.zeros((g_pad - n_groups, d), jnp.float32)], axis=0)
    mod_all = _mod_call(cvec, w_ada, b_ada).reshape(depth, g_pad, 6, d)

    cos_t, sin_t = _rope_tables(seq_l, tb_in)
    consts = tuple(jnp.asarray(a) for a in _rwkv_consts())
    ck_all = cache_k.reshape(nb_l, depth, past, KV_DIM)
    cv_all = cache_v.reshape(nb_l, depth, past, KV_DIM)
    eye_h = jnp.eye(RWKV_HEADS, dtype=jnp.float32)
    s0_bd_all = jnp.einsum('bldhij,hg->bldhigj', state_rwkv, eye_h).reshape(nb_l, depth, 2, W, W)
    s0_zero = jnp.zeros((nb_c, 2, W, W), jnp.float32)

    ks, vs, sts = [], [], []
    for l in range(depth):
        mod = mod_all[l]
        p_att, p_rw, p_cv = _in_proj_call(x, mod, _bf(_pad_w_in(w_in[l])), cos_t, sin_t, tb_in, n_ctx, seq_l)
        att_c = _ctx_attn_call(p_att, att_sink[l], nb_c, seq_c)
        att_l = _lat_attn_call(p_att, ck_all, cv_all, att_sink[l], l, nb_l, seq_l, n_ctx)
        wts = _rwkv_weights(rw_w0[l], rw_w_up[l], rw_a0[l], rw_a_up[l], rw_g_up[l], rw_kk[l], rw_ka[l],
                            rw_rk[l], rw_lnx_g[l], rw_lnx_b[l])
        rw_c, sfin = _rwkv_call(p_rw, s0_zero, consts, wts, nb_c, seq_c, 0)
        rw_l, _ = _rwkv_call(p_rw, s0_bd_all[:, l], consts, wts, nb_l, seq_l, n_ctx)
        x1 = _out_proj_call(att_c, att_l, rw_c, rw_l, p_cv, x, mod, _bf(w_out[l]), conv_w[l],
                            ln1_g[l], ln1_b[l], tb_out, n_ctx, seq_c, seq_l)
        sk = peer_subkeys[l].reshape(2 * PEER_HEADS, PEER_NKEYS, PEER_HALF)
        x = _peer_call(x1, mod, _bf(peer_wq[l].T), sk, _bf(peer_u[l]), _bf(peer_v[l].T),
                       ln2_g[l], ln2_b[l], tb_peer, n_ctx, seq_l)
        ks.append(p_att[:n_ctx, ATT_DIM:ATT_DIM + KV_DIM].reshape(nb_c, seq_c, ATT_KV_HEADS, HEAD_DIM))
        vs.append(p_att[:n_ctx, ATT_DIM + KV_DIM:].reshape(nb_c, seq_c, ATT_KV_HEADS, HEAD_DIM))
        s5 = sfin.reshape(nb_c, 2, RWKV_HEADS, HEAD_DIM, RWKV_HEADS, HEAD_DIM)
        sts.append(jnp.stack([s5[:, :, h, :, h, :] for h in range(RWKV_HEADS)], axis=2))

    y_prompt = x[:n_ctx].reshape(nb_c, seq_c, d)
    y_sample = x[n_ctx:].reshape(nb_l, seq_l, d)
    return (y_prompt, y_sample, jnp.stack(ks, axis=1), jnp.stack(vs, axis=1), jnp.stack(sts, axis=1))


def kernel(x_prompt, x_sample, cache_k, cache_v, state_rwkv, c, c_ctx, ln_in_g, ln_in_b, w_ada, b_ada, w_in, w_out, att_sink, rw_w0, rw_w_up, rw_a0, rw_a_up, rw_g_up, rw_kk, rw_ka, rw_rk, rw_lnx_g, rw_lnx_b, conv_w, ln1_g, ln1_b, ln2_g, ln2_b, peer_wq, peer_subkeys, peer_u, peer_v):
    return _forward(x_prompt, x_sample, cache_k, cache_v, state_rwkv, c, c_ctx, ln_in_g, ln_in_b, w_ada, b_ada,
                    w_in, w_out, att_sink, rw_w0, rw_w_up, rw_a0, rw_a_up, rw_g_up, rw_kk, rw_ka, rw_rk,
                    rw_lnx_g, rw_lnx_b, conv_w, ln1_g, ln1_b, ln2_g, ln2_b, peer_wq, peer_subkeys, peer_u, peer_v)
```

```python
import functools
import math

import jax
import jax.numpy as jnp
import numpy as np
from jax import lax
from jax.experimental import pallas as pl
from jax.experimental.pallas import tpu as pltpu

D_MODEL = 1024
DEPTH = 4
GRID_W = 64
HEAD_DIM = 64
ATT_DIM = 512
RWKV_DIM = 256
CONV_DIM = 256
ATT_HEADS = 8
ATT_KV_HEADS = 2
ATT_GROUP = 4
KV_DIM = 128
WINDOW = 128
ATT_BLOCK = 128
ROPE_THETA = 10000.0
RWKV_HEADS = 4
W_RANK = 32
A_RANK = 32
G_RANK = 64
RWKV_LNX_EPS = 64e-5
PEER_HEADS = 8
PEER_NKEYS = 128
PEER_TOPK = 16
PEER_HALF = 128
LN_EPS = 1e-5
ALPHA = (2 * DEPTH) ** 0.25
NEG = -1e30

LANES = 128
SUBLANES = 8
VMEM_LIMIT = 56 * 1024 * 1024

QK_W = ATT_DIM + KV_DIM
ATT_W = ATT_DIM + 2 * KV_DIM
RW_W = 1024
CV_W = 3 * CONV_DIM
IN_PAD = ATT_W + RW_W + CV_W
RW_LOW = 3 * RWKV_DIM

RW_CHUNK = 64
PEER_SCORE_PASSES = 3
HIGHEST = lax.Precision.HIGHEST
NN = (((1,), (0,)), ((), ()))
NT = (((1,), (1,)), ((), ()))
TN = (((0,), (0,)), ((), ()))


def _mm(a, b, dims=NN, precision=None):
    return lax.dot_general(a, b, dims, precision=precision, preferred_element_type=jnp.float32)


def _bf(x):
    return x.astype(jnp.bfloat16)


def _split(x):
    hi = _bf(x)
    return hi, _bf(x - hi.astype(jnp.float32))


def _mm_split(a, b, dims, passes):
    out = _mm(a[0], b[0], dims)
    if passes == 3:
        out = out + (_mm(a[0], b[1], dims) + _mm(a[1], b[0], dims))
    return out


def _mm_terms(x, exact, terms, x_is_rhs=False):
    out = None
    rest = x
    for _ in range(terms):
        piece = _bf(rest)
        rest = rest - piece.astype(jnp.float32)
        part = _mm(exact, piece) if x_is_rhs else _mm(piece, exact)
        out = part if out is None else out + part
    return out


RW_PASSES = {"gram_n": 1, "gram": 1, "inv": 1, "loc": 1, "seq": 1}


def _layer_norm(x, g, b, eps):
    mu = jnp.mean(x, axis=-1, keepdims=True)
    xc = x - mu
    var = jnp.mean(xc * xc, axis=-1, keepdims=True)
    return xc * lax.rsqrt(var + eps) * g + b


def _params(*sem):
    return pltpu.CompilerParams(dimension_semantics=sem, vmem_limit_bytes=VMEM_LIMIT)


def _full(shape):
    nd = len(shape)
    return pl.BlockSpec(shape, lambda *_: (0,) * nd)


def _mod_kernel(c_ref, w_ref, b_ref, o_ref):
    cv = c_ref[...]
    s = cv * jax.nn.sigmoid(cv)
    o_ref[...] = _mm(s, w_ref[...], precision=HIGHEST) + b_ref[...]


def _mod_call(cvec, w_ada, b_ada):
    depth, d, n6 = w_ada.shape
    rows = cvec.shape[0]
    nt = n6 // d
    return pl.pallas_call(
        _mod_kernel,
        grid=(depth, nt),
        in_specs=[
            pl.BlockSpec((rows, d), lambda l, n: (0, 0)),
            pl.BlockSpec((None, d, d), lambda l, n: (l, 0, n)),
            pl.BlockSpec((None, 1, d), lambda l, n: (l, 0, n)),
        ],
        out_specs=pl.BlockSpec((None, rows, d), lambda l, n: (l, 0, n)),
        out_shape=jax.ShapeDtypeStruct((depth, rows, n6), jnp.float32),
        compiler_params=_params("parallel", "parallel"),
        name="adaln_mod",
    )(cvec, w_ada, b_ada.reshape(depth, 1, n6))


def _ln_kernel(x_ref, g_ref, b_ref, o_ref):
    o_ref[...] = _layer_norm(x_ref[...], g_ref[...], b_ref[...], LN_EPS)


def _ln_call(x, g, b, tb):
    rows, d = x.shape
    return pl.pallas_call(
        _ln_kernel,
        grid=(rows // tb,),
        in_specs=[pl.BlockSpec((tb, d), lambda i: (i, 0)), _full((1, d)), _full((1, d))],
        out_specs=pl.BlockSpec((tb, d), lambda i: (i, 0)),
        out_shape=jax.ShapeDtypeStruct((rows, d), jnp.float32),
        compiler_params=_params("parallel"),
        name="ln_in",
    )(x, g.reshape(1, d), b.reshape(1, d))


def _group_of_block(i, n_ctx_blocks, blocks_per_lat_seq):
    return jnp.where(i < n_ctx_blocks, 0, 1 + (i - n_ctx_blocks) // blocks_per_lat_seq)


def _in_proj_kernel(x_ref, mod_ref, w_ref, cos_ref, sin_ref, att_ref, rw_ref, cv_ref):
    x = x_ref[...]
    h = x * (1.0 + mod_ref[1:2, :]) + mod_ref[0:1, :]
    p = _mm(_bf(h), w_ref[...])
    qk = p[:, :QK_W]
    lane = lax.broadcasted_iota(jnp.int32, qk.shape, 1)
    partner = jnp.where((lane % 32) < 16,
                        pltpu.roll(qk, QK_W - 16, 1),
                        pltpu.roll(qk, 16, 1))
    att_ref[:, :QK_W] = qk * cos_ref[...] + partner * sin_ref[...]
    att_ref[:, QK_W:] = p[:, QK_W:ATT_W]
    rw_ref[...] = p[:, ATT_W:ATT_W + RW_W]
    cv_ref[...] = p[:, ATT_W + RW_W:]


def _in_proj_call(x, mod, w_in_p, cos_t, sin_t, tb, n_ctx_rows, lat_seq):
    rows, d = x.shape
    ncb = n_ctx_rows // tb
    bps = lat_seq // tb

    def mod_map(i):
        return (_group_of_block(i, ncb, bps), 0, 0)

    def rope_map(i):
        return (jnp.where(i < ncb, bps, (i - ncb) % bps), 0)

    return pl.pallas_call(
        _in_proj_kernel,
        grid=(rows // tb,),
        in_specs=[
            pl.BlockSpec((tb, d), lambda i: (i, 0)),
            pl.BlockSpec((None, 6, d), mod_map),
            _full((d, IN_PAD)),
            pl.BlockSpec((tb, QK_W), rope_map),
            pl.BlockSpec((tb, QK_W), rope_map),
        ],
        out_specs=[
            pl.BlockSpec((tb, ATT_W), lambda i: (i, 0)),
            pl.BlockSpec((tb, RW_W), lambda i: (i, 0)),
            pl.BlockSpec((tb, CV_W), lambda i: (i, 0)),
        ],
        out_shape=[
            jax.ShapeDtypeStruct((rows, ATT_W), jnp.float32),
            jax.ShapeDtypeStruct((rows, RW_W), jnp.float32),
            jax.ShapeDtypeStruct((rows, CV_W), jnp.float32),
        ],
        compiler_params=_params("parallel"),
        name="in_proj",
    )(x, mod, w_in_p, cos_t, sin_t)


def _ctx_attn_kernel(sink_ref, p_ref, o_ref):
    scale = HEAD_DIM ** -0.5
    for kv in range(ATT_KV_HEADS):
        k = _bf(p_ref[:, ATT_DIM + kv * HEAD_DIM:ATT_DIM + (kv + 1) * HEAD_DIM])
        v = _bf(p_ref[:, ATT_DIM + KV_DIM + kv * HEAD_DIM:ATT_DIM + KV_DIM + (kv + 1) * HEAD_DIM])
        for g in range(ATT_GROUP):
            hd = kv * ATT_GROUP + g
            q = _bf(p_ref[:, hd * HEAD_DIM:(hd + 1) * HEAD_DIM])
            s = _mm(q, k, NT) * scale
            sink = sink_ref[hd]
            m = jnp.maximum(jnp.max(s, axis=-1, keepdims=True), sink)
            e = jnp.exp(s - m)
            den = jnp.sum(e, axis=-1, keepdims=True) + jnp.exp(sink - m)
            o_ref[:, hd * HEAD_DIM:(hd + 1) * HEAD_DIM] = _mm(_bf(e), v) / den


def _ctx_attn_call(p_att, sink, n_seq, seq):
    return pl.pallas_call(
        _ctx_attn_kernel,
        grid=(n_seq,),
        in_specs=[
            pl.BlockSpec(memory_space=pltpu.SMEM),
            pl.BlockSpec((seq, ATT_W), lambda b: (b, 0)),
        ],
        out_specs=pl.BlockSpec((seq, ATT_DIM), lambda b: (b, 0)),
        out_shape=jax.ShapeDtypeStruct((n_seq * seq, ATT_DIM), jnp.float32),
        compiler_params=_params("parallel"),
        name="ctx_attn",
    )(sink, p_att)


def _lat_attn_kernel(sink_ref, own_ref, prev_ref, next_ref, ck_ref, cv_ref, o_ref, *, n_blocks):
    n = pl.program_id(1)
    scale = HEAD_DIM ** -0.5
    qi = lax.broadcasted_iota(jnp.int32, (ATT_BLOCK, 3 * ATT_BLOCK), 0)
    kj = lax.broadcasted_iota(jnp.int32, (ATT_BLOCK, 3 * ATT_BLOCK), 1)
    rel = kj - ATT_BLOCK - qi
    kpos = (n - 1) * ATT_BLOCK + kj
    mask = (jnp.abs(rel) <= WINDOW) & (kpos >= 0) & (kpos < n_blocks * ATT_BLOCK)
    mask4 = jnp.concatenate([mask] * ATT_GROUP, axis=0)
    row_group = lax.broadcasted_iota(jnp.int32, (ATT_GROUP * ATT_BLOCK, 1), 0) // ATT_BLOCK
    for kv in range(ATT_KV_HEADS):
        ks = slice(ATT_DIM + kv * HEAD_DIM, ATT_DIM + (kv + 1) * HEAD_DIM)
        vs = slice(ATT_DIM + KV_DIM + kv * HEAD_DIM, ATT_DIM + KV_DIM + (kv + 1) * HEAD_DIM)
        kw = _bf(jnp.concatenate([prev_ref[:, ks], own_ref[:, ks], next_ref[:, ks]], axis=0))
        vw = _bf(jnp.concatenate([prev_ref[:, vs], own_ref[:, vs], next_ref[:, vs]], axis=0))
        ck = _bf(ck_ref[:, kv * HEAD_DIM:(kv + 1) * HEAD_DIM])
        cv = _bf(cv_ref[:, kv * HEAD_DIM:(kv + 1) * HEAD_DIM])
        q4 = _bf(jnp.concatenate(
            [own_ref[:, (kv * ATT_GROUP + g) * HEAD_DIM:(kv * ATT_GROUP + g + 1) * HEAD_DIM]
             for g in range(ATT_GROUP)], axis=0))
        s_ctx = _mm(q4, ck, NT) * scale
        s_win = jnp.where(mask4, _mm(q4, kw, NT) * scale, NEG)
        sink = jnp.full((ATT_GROUP * ATT_BLOCK, 1), sink_ref[kv * ATT_GROUP], jnp.float32)
        for g in range(1, ATT_GROUP):
            sink = jnp.where(row_group == g, sink_ref[kv * ATT_GROUP + g], sink)
        m = jnp.maximum(jnp.maximum(jnp.max(s_ctx, axis=-1, keepdims=True),
                                    jnp.max(s_win, axis=-1, keepdims=True)), sink)
        e_ctx = jnp.exp(s_ctx - m)
        e_win = jnp.exp(s_win - m)
        den = (jnp.sum(e_ctx, axis=-1, keepdims=True) + jnp.sum(e_win, axis=-1, keepdims=True)
               + jnp.exp(sink - m))
        o4 = (_mm(_bf(e_ctx), cv) + _mm(_bf(e_win), vw)) / den
        for g in range(ATT_GROUP):
            hd = kv * ATT_GROUP + g
            o_ref[:, hd * HEAD_DIM:(hd + 1) * HEAD_DIM] = o4[g * ATT_BLOCK:(g + 1) * ATT_BLOCK]


def _lat_attn_call(p_att, ck, cv, sink, layer, n_seq, seq, row_off):
    nb = seq // ATT_BLOCK
    off = row_off // ATT_BLOCK
    past = ck.shape[2]

    def own(b, n):
        return (off + b * nb + n, 0)

    def prev(b, n):
        return (off + b * nb + jnp.maximum(n - 1, 0), 0)

    def nxt(b, n):
        return (off + b * nb + jnp.minimum(n + 1, nb - 1), 0)

    return pl.pallas_call(
        functools.partial(_lat_attn_kernel, n_blocks=nb),
        grid=(n_seq, nb),
        in_specs=[
            pl.BlockSpec(memory_space=pltpu.SMEM),
            pl.BlockSpec((ATT_BLOCK, ATT_W), own),
            pl.BlockSpec((ATT_BLOCK, ATT_W), prev),
            pl.BlockSpec((ATT_BLOCK, ATT_W), nxt),
            pl.BlockSpec((None, None, past, KV_DIM), lambda b, n: (b, layer, 0, 0)),
            pl.BlockSpec((None, None, past, KV_DIM), lambda b, n: (b, layer, 0, 0)),
        ],
        out_specs=pl.BlockSpec((ATT_BLOCK, ATT_DIM), lambda b, n: (b * nb + n, 0)),
        out_shape=jax.ShapeDtypeStruct((n_seq * seq, ATT_DIM), jnp.float32),
        compiler_params=_params("parallel", "parallel"),
        name="lat_attn",
    )(sink, p_att, p_att, p_att, ck, cv)


def _rwkv_consts():
    L, H, W = RW_CHUNK, RWKV_HEADS, RWKV_DIM
    t = np.arange(L)
    inc = np.stack([(t[None, :] <= t[:, None]), (t[None, :] >= t[:, None])]).astype(np.float32)
    stc = np.stack([(t[None, :] < t[:, None]), (t[None, :] > t[:, None])]).astype(np.float32)
    eye_h = np.eye(H, dtype=np.float32)
    inc_bd = np.stack([np.kron(eye_h, inc[d]) for d in range(2)])
    stc_bd = np.stack([np.kron(eye_h, stc[d]) for d in range(2)])
    blk = np.kron(eye_h, np.ones((L, HEAD_DIM), np.float32))
    eye = np.eye(W, dtype=np.float32)
    return inc, inc_bd, stc_bd, blk, eye


def _rwkv_kernel(p_ref, s0_ref, inc_ref, incbd_ref, stcbd_ref, blk_ref, eye_ref,
                 w0_ref, wup_ref, a0_ref, aup_ref, gup_ref, kkw_ref, kaw_ref, rkw_ref, lng_ref, lnb_ref,
                 o_ref, sfin_ref, y_scr, bon_scr, *, seq):
    L = RW_CHUNK
    nc = seq // L
    blk = _bf(blk_ref[...])
    ones_bd = blk
    eye = eye_ref[...]

    def stack(x):
        return tuple(jnp.concatenate([z] * RWKV_HEADS, axis=0) * blk for z in _split(x))

    def chunk(d, r0, s_in):
        x = p_ref[pl.ds(r0, L), :]
        rr = x[:, 0:RWKV_DIM]
        rk = x[:, RWKV_DIM:2 * RWKV_DIM]
        rv = x[:, 2 * RWKV_DIM:3 * RWKV_DIM]
        low = x[:, RW_LOW:RW_LOW + LANES]
        zw = w0_ref[d] + _mm_split(_split(jnp.tanh(low)), _split(wup_ref[d]), NN, 3)
        u = -zw
        softplus = jnp.maximum(u, 0.0) + jnp.log1p(jnp.exp(-jnp.abs(u)))
        lw = -jnp.exp(-softplus - 0.5)
        asig = jax.nn.sigmoid(a0_ref[d] + _mm_split(_split(low), _split(aup_ref[d]), NN, 3))
        kd = rk * (1.0 + (asig - 1.0) * kaw_ref[...])
        kkr = rk * kkw_ref[...]
        kk = kkr * lax.rsqrt(jnp.maximum(_mm_terms(kkr * kkr, ones_bd, 2), 1e-24))
        a_s = -kk
        b_s = kk * asig
        bon_scr[d, pl.ds(r0, L), :] = _mm_terms(rr * kd * rkw_ref[...], ones_bd, 2) * rv

        cl = _mm_terms(lw, _bf(inc_ref[d]), 3, x_is_rhs=True)
        cl_end = cl[L - 1:L, :] if d == 0 else cl[0:1, :]
        e_neg = jnp.exp(-cl)
        e_tail = jnp.exp(cl_end - cl)
        a_st = stack(a_s * jnp.exp(cl - lw))
        b_st = stack(b_s * e_neg)
        k_st = stack(kd * e_neg)
        r_st = stack(rr * jnp.exp(cl))
        v_st = stack(rv)
        bh_st = stack(b_s * e_tail)
        kh_st = stack(kd * e_tail)
        stc = stcbd_ref[d]
        inc = incbd_ref[d]
        n_m = stc * _mm_split(a_st, b_st, NT, RW_PASSES["gram_n"])
        m1 = stc * _mm_split(a_st, k_st, NT, RW_PASSES["gram"])
        p_m = inc * _mm_split(r_st, b_st, NT, RW_PASSES["gram"])
        q_m = inc * _mm_split(r_st, k_st, NT, RW_PASSES["gram"])
        t_m = eye + n_m
        n_p = _split(n_m)
        for it in range(5):
            n_sq = _mm_split(n_p, n_p, NN, RW_PASSES["inv"])
            n_p = _split(n_sq)
            t_m = t_m + _mm_split(_split(t_m), n_p, NN, RW_PASSES["inv"])
        t_s = _split(t_m)
        u_loc = _mm_split(t_s, _split(_mm_split(_split(m1), v_st, NN, RW_PASSES["loc"])), NN, RW_PASSES["loc"])
        w_t = _mm_split(t_s, a_st, NN, RW_PASSES["loc"])
        qv = _mm_split(_split(q_m), v_st, NN, RW_PASSES["loc"])
        vk = _mm_split(v_st, kh_st, TN, RW_PASSES["loc"])
        s_s = _split(s_in)
        u_m = _mm_split(_split(w_t), s_s, NT, RW_PASSES["seq"]) + u_loc
        u_s = _split(u_m)
        y_bd = (_mm_split(r_st, s_s, NT, RW_PASSES["seq"])
                + _mm_split(_split(p_m), u_s, NN, RW_PASSES["seq"]) + qv)
        y = y_bd[0:L] + y_bd[L:2 * L] + y_bd[2 * L:3 * L] + y_bd[3 * L:4 * L]
        y_scr[d, pl.ds(r0, L), :] = y
        return s_in * jnp.exp(cl_end) + _mm_split(u_s, bh_st, TN, RW_PASSES["seq"]) + vk

    def body(c, carry):
        s_f, s_b = carry
        s_f = chunk(0, pl.multiple_of(c * L, L), s_f)
        s_b = chunk(1, pl.multiple_of((nc - 1 - c) * L, L), s_b)
        return s_f, s_b

    s_f, s_b = lax.fori_loop(0, nc, body, (s0_ref[0], s0_ref[1]))
    sfin_ref[0] = s_f
    sfin_ref[1] = s_b

    inv_n = 1.0 / HEAD_DIM

    def epilogue(c, carry):
        r0 = pl.multiple_of(c * L, L)
        y = y_scr[0, pl.ds(r0, L), :] + y_scr[1, pl.ds(r0, L), :]
        mu = _mm_terms(y, ones_bd, 3) * inv_n
        yc = y - mu
        var = _mm_terms(yc * yc, ones_bd, 2) * inv_n
        yn = yc * lax.rsqrt(var + RWKV_LNX_EPS) * lng_ref[...] + lnb_ref[...]
        gd = p_ref[pl.ds(r0, L), RW_LOW + LANES:RW_LOW + 2 * LANES]
        g = _mm_split(_split(jax.nn.sigmoid(gd)), _split(gup_ref[...]), NN, 3)
        bon = bon_scr[0, pl.ds(r0, L), :] + bon_scr[1, pl.ds(r0, L), :]
        o_ref[pl.ds(r0, L), :] = (yn + bon) * g
        return carry

    lax.fori_loop(0, nc, epilogue, 0)


def _rwkv_call(p_rw, s0_bd, consts, wts, n_seq, seq, row_off):
    inc, inc_bd, stc_bd, blk, eye = consts
    W = RWKV_DIM
    off = row_off // seq
    in_specs = [
        pl.BlockSpec((seq, RW_W), lambda b: (off + b, 0)),
        pl.BlockSpec((None, 2, W, W), lambda b: (b, 0, 0, 0)),
        _full(inc.shape), _full(inc_bd.shape), _full(stc_bd.shape), _full(blk.shape), _full(eye.shape),
    ] + [_full(w.shape) for w in wts]
    return pl.pallas_call(
        functools.partial(_rwkv_kernel, seq=seq),
        grid=(n_seq,),
        in_specs=in_specs,
        out_specs=[
            pl.BlockSpec((seq, W), lambda b: (b, 0)),
            pl.BlockSpec((None, 2, W, W), lambda b: (b, 0, 0, 0)),
        ],
        out_shape=[
            jax.ShapeDtypeStruct((n_seq * seq, W), jnp.float32),
            jax.ShapeDtypeStruct((n_seq, 2, W, W), jnp.float32),
        ],
        scratch_shapes=[pltpu.VMEM((2, seq, W), jnp.float32), pltpu.VMEM((2, seq, W), jnp.float32)],
        compiler_params=_params("parallel"),
        name="rwkv",
    )(p_rw, s0_bd, inc, inc_bd, stc_bd, blk, eye, *wts)


def _rwkv_weights(w0, w_up, a0, a_up, g_up, kk, ka, rk, lng, lnb):
    W = RWKV_DIM
    wup_p = jnp.zeros((2, LANES, W), jnp.float32)
    aup_p = jnp.zeros((2, LANES, W), jnp.float32)
    for d in range(2):
        wup_p = wup_p.at[d, d * W_RANK:(d + 1) * W_RANK].set(w_up[d])
        aup_p = aup_p.at[d, 2 * W_RANK + d * A_RANK:2 * W_RANK + (d + 1) * A_RANK].set(a_up[d])
    gup_p = jnp.zeros((LANES, W), jnp.float32).at[:G_RANK].set(g_up)
    row = lambda v: v.reshape(1, W)
    return (w0.reshape(2, 1, W), wup_p, a0.reshape(2, 1, W), aup_p, gup_p,
            row(kk), row(ka), row(rk), row(lng), row(lnb))


def _out_proj_kernel(attc_ref, attl_ref, rwc_ref, rwl_ref, cv_ref, cvp_ref, cvn_ref, x_ref, mod_ref,
                     w_ref, cw_ref, g_ref, b_ref, o_ref, *, tb, ncb, bps_c, bps):
    i = pl.program_id(0)
    is_ctx = i < ncb
    pos = jnp.where(is_ctx, i % bps_c, (i - ncb) % bps)
    last = jnp.where(is_ctx, bps_c - 1, bps - 1)
    att = jnp.where(is_ctx, attc_ref[...], attl_ref[...])
    rw = jnp.where(is_ctx, rwc_ref[...], rwl_ref[...])

    cb = cv_ref[:, 0:CONV_DIM]
    z = cv_ref[:, CONV_DIM:2 * CONV_DIM] * cv_ref[:, 2 * CONV_DIM:3 * CONV_DIM]
    halo_p = cvp_ref[SUBLANES - 1:SUBLANES, CONV_DIM:2 * CONV_DIM] * cvp_ref[SUBLANES - 1:SUBLANES, 2 * CONV_DIM:]
    halo_n = cvn_ref[0:1, CONV_DIM:2 * CONV_DIM] * cvn_ref[0:1, 2 * CONV_DIM:]
    halo_p = jnp.where(pos > 0, halo_p, 0.0)
    halo_n = jnp.where(pos < last, halo_n, 0.0)
    row = lax.broadcasted_iota(jnp.int32, z.shape, 0)
    z_prev = jnp.where(row == 0, halo_p, pltpu.roll(z, 1, 0))
    z_next = jnp.where(row == tb - 1, halo_n, pltpu.roll(z, tb - 1, 0))
    conv = cb * (cw_ref[0:1, :] * z_prev + cw_ref[1:2, :] * z + cw_ref[2:3, :] * z_next)

    mix = (_mm(_bf(att), w_ref[0:ATT_DIM, :])
           + _mm(_bf(rw), w_ref[ATT_DIM:ATT_DIM + RWKV_DIM, :])
           + _mm(_bf(conv), w_ref[ATT_DIM + RWKV_DIM:, :]))
    o_ref[...] = _layer_norm(ALPHA * x_ref[...] + mod_ref[2:3, :] * mix, g_ref[...], b_ref[...], LN_EPS)


def _out_proj_call(att_c, att_l, rw_c, rw_l, p_cv, x, mod, w_out_b, conv_w, ln_g, ln_b, tb, n_ctx_rows, ctx_seq,
                   lat_seq):
    rows, d = x.shape
    ncb = n_ctx_rows // tb
    bps = lat_seq // tb
    sub = tb // SUBLANES
    n_sub = rows // SUBLANES

    def ctx_map(i):
        return (jnp.minimum(i, ncb - 1), 0)

    def lat_map(i):
        return (jnp.maximum(i - ncb, 0), 0)

    return pl.pallas_call(
        functools.partial(_out_proj_kernel, tb=tb, ncb=ncb, bps_c=ctx_seq // tb, bps=bps),
        grid=(rows // tb,),
        in_specs=[
            pl.BlockSpec((tb, ATT_DIM), ctx_map),
            pl.BlockSpec((tb, ATT_DIM), lat_map),
            pl.BlockSpec((tb, RWKV_DIM), ctx_map),
            pl.BlockSpec((tb, RWKV_DIM), lat_map),
            pl.BlockSpec((tb, CV_W), lambda i: (i, 0)),
            pl.BlockSpec((SUBLANES, CV_W), lambda i: (jnp.maximum(i * sub - 1, 0), 0)),
            pl.BlockSpec((SUBLANES, CV_W), lambda i: (jnp.minimum((i + 1) * sub, n_sub - 1), 0)),
            pl.BlockSpec((tb, d), lambda i: (i, 0)),
            pl.BlockSpec((None, 6, d), lambda i: (_group_of_block(i, ncb, bps), 0, 0)),
            _full((d, d)),
            _full((3, CONV_DIM)),
            _full((1, d)),
            _full((1, d)),
        ],
        out_specs=pl.BlockSpec((tb, d), lambda i: (i, 0)),
        out_shape=jax.ShapeDtypeStruct((rows, d), jnp.float32),
        compiler_params=_params("parallel"),
        name="out_proj",
    )(att_c, att_l, rw_c, rw_l, p_cv, p_cv, p_cv, x, mod, w_out_b, conv_w, ln_g.reshape(1, d), ln_b.reshape(1, d))


def _gelu_tanh(x):
    return 0.5 * x * (1.0 + jnp.tanh(math.sqrt(2.0 / math.pi) * (x + 0.044715 * (x * x * x))))


def _extract_top(work, iota, n_out, want_rank=False):
    vals = []
    big = work.shape[0]
    rank = jnp.full(work.shape, float(big), jnp.float32) if want_rank else None
    for k in range(n_out):
        m = jnp.max(work, axis=0, keepdims=True)
        idx = jnp.min(jnp.where(work == m, iota, big), axis=0, keepdims=True)
        sel = iota == idx
        work = jnp.where(sel, -jnp.inf, work)
        if want_rank:
            rank = jnp.where(sel, float(k), rank)
        vals.append(m)
    return vals, work, rank


def _peer_kernel(x_ref, mod_ref, wq_ref, sk_ref, u_ref, vt_ref, g_ref, b_ref, o_ref,
                 ht_scr, s0_scr, e0_scr, s1_scr, r1_scr, e1_scr, a_scr, z_scr, acc_scr, *, tb):
    e_step = pl.program_id(1)
    n_lt = tb // LANES
    K = PEER_TOPK

    @pl.when(e_step == 0)
    def _route():
        h2 = x_ref[...] * (1.0 + mod_ref[4:5, :]) + mod_ref[3:4, :]
        ht = _bf(h2.T)
        ht_scr[...] = ht
        qt = _mm(wq_ref[...], ht)
        for hp in range(2 * PEER_HEADS):
            sc = _mm_split(_split(sk_ref[hp]), _split(qt[hp * PEER_HALF:(hp + 1) * PEER_HALF, :]), NN,
                           PEER_SCORE_PASSES)
            if hp % 2 == 0:
                s0_scr[hp // 2] = sc
            else:
                s1_scr[hp // 2] = sc

        iota_k = lax.broadcasted_iota(jnp.int32, (PEER_NKEYS, LANES), 0)
        iota_c = lax.broadcasted_iota(jnp.int32, ((SUBLANES + 2) * SUBLANES, LANES), 0)

        def per_tile(j, carry):
            hd = j // n_lt
            lanes = pl.ds(pl.multiple_of((j % n_lt) * LANES, LANES), LANES)
            sc0 = s0_scr[hd, :, lanes]
            sc1 = s1_scr[hd, :, lanes]
            v0, w0, _ = _extract_top(sc0, iota_k, K)
            v1, w1, rank1 = _extract_top(sc1, iota_k, K, want_rank=True)
            cand0 = w0 != sc0
            cand1 = w1 != sc1
            lo1 = jnp.concatenate(v1[:SUBLANES], axis=0)
            hi1 = jnp.concatenate(v1[SUBLANES:], axis=0)
            hi0 = jnp.concatenate(v0[SUBLANES:], axis=0)
            pair = jnp.concatenate([v0[a] + lo1 for a in range(SUBLANES)] + [v0[0] + hi1, hi0 + v1[0]], axis=0)
            f, _, _ = _extract_top(pair, iota_c, K + 1)
            zsum = jnp.zeros_like(f[0])
            for k in range(K):
                zsum = zsum + jnp.exp(f[k] - f[0])
            th = 0.5 * (f[K - 1] + f[K]) - sc0
            count0 = jnp.zeros_like(sc0)
            for b in range(K):
                count0 = count0 + jnp.where(v1[b] > th, 1.0, 0.0)
            s0_scr[hd, :, lanes] = jnp.where(cand0, count0, 0.0)
            e0_scr[hd, :, lanes] = jnp.where(cand0, jnp.exp(sc0 - v0[0]), 0.0) / zsum
            e1_scr[hd, :, lanes] = _bf(jnp.where(cand1, jnp.exp(sc1 - v1[0]), 0.0))
            r1_scr[hd, :, lanes] = _bf(rank1)
            return carry

        lax.fori_loop(0, PEER_HEADS * n_lt, per_tile, 0)
        acc_scr[...] = jnp.zeros_like(acc_scr)

    a_scr[...] = _gelu_tanh(_mm(u_ref[...], ht_scr[...]))

    keys = pl.ds(pl.multiple_of(e_step * SUBLANES, SUBLANES), SUBLANES)
    half = PEER_NKEYS // 2
    group = SUBLANES // 2
    packed = 2 * SUBLANES
    tile3 = (half // packed, packed, LANES)
    for ln in range(n_lt):
        lanes = slice(ln * LANES, (ln + 1) * LANES)
        c0 = [s0_scr[hd, keys, lanes] for hd in range(PEER_HEADS)]
        e0 = [e0_scr[hd, keys, lanes] for hd in range(PEER_HEADS)]
        for jh in range(2):
            jrows = slice(jh * half, (jh + 1) * half)
            for ig in range(SUBLANES // group):
                g = [jnp.zeros(tile3, jnp.bfloat16) for _ in range(group)]
                for hd in range(PEER_HEADS):
                    r1 = r1_scr[hd, jrows, lanes].reshape(tile3)
                    e1 = e1_scr[hd, jrows, lanes].reshape(tile3)
                    for k in range(group):
                        ii = ig * group + k
                        cnt = _bf(jnp.broadcast_to(c0[hd][ii:ii + 1, :], tile3[1:]))[None]
                        w0 = _bf(jnp.broadcast_to(e0[hd][ii:ii + 1, :], tile3[1:]))[None]
                        g[k] = g[k] + jnp.where(r1 < cnt, e1, jnp.zeros_like(e1)) * w0
                for k in range(group):
                    r0 = (ig * group + k) * PEER_NKEYS + jh * half
                    z_scr[r0:r0 + half, lanes] = g[k].reshape(half, LANES) * _bf(a_scr[r0:r0 + half, lanes])

    acc_scr[...] += _mm(vt_ref[...], z_scr[...])

    @pl.when(e_step == pl.num_programs(1) - 1)
    def _finish():
        out = acc_scr[...].T
        o_ref[...] = _layer_norm(ALPHA * x_ref[...] + mod_ref[5:6, :] * out, g_ref[...], b_ref[...], LN_EPS)


def _peer_call(x, mod, wq_t, subkeys, u_b, vt_b, ln_g, ln_b, tb, n_ctx_rows, lat_seq):
    rows, d = x.shape
    et = SUBLANES * PEER_NKEYS
    ncb = n_ctx_rows // tb
    bps = lat_seq // tb
    nq = wq_t.shape[0]
    f32 = jnp.float32
    return pl.pallas_call(
        functools.partial(_peer_kernel, tb=tb),
        grid=(rows // tb, u_b.shape[0] // et),
        in_specs=[
            pl.BlockSpec((tb, d), lambda i, e: (i, 0)),
            pl.BlockSpec((None, 6, d), lambda i, e: (_group_of_block(i, ncb, bps), 0, 0)),
            pl.BlockSpec((nq, d), lambda i, e: (0, 0), pipeline_mode=pl.Buffered(1)),
            pl.BlockSpec(subkeys.shape, lambda i, e: (0, 0, 0), pipeline_mode=pl.Buffered(1)),
            pl.BlockSpec((et, d), lambda i, e: (e, 0)),
            pl.BlockSpec((None, d, et), lambda i, e: (e, 0, 0)),
            pl.BlockSpec((1, d), lambda i, e: (0, 0)),
            pl.BlockSpec((1, d), lambda i, e: (0, 0)),
        ],
        out_specs=pl.BlockSpec((tb, d), lambda i, e: (i, 0)),
        out_shape=jax.ShapeDtypeStruct((rows, d), f32),
        scratch_shapes=[
            pltpu.VMEM((d, tb), jnp.bfloat16),
            pltpu.VMEM((PEER_HEADS, PEER_NKEYS, tb), f32),
            pltpu.VMEM((PEER_HEADS, PEER_NKEYS, tb), f32),
            pltpu.VMEM((PEER_HEADS, PEER_NKEYS, tb), f32),
            pltpu.VMEM((PEER_HEADS, PEER_NKEYS, tb), jnp.bfloat16),
            pltpu.VMEM((PEER_HEADS, PEER_NKEYS, tb), jnp.bfloat16),
            pltpu.VMEM((et, tb), f32),
            pltpu.VMEM((et, tb), jnp.bfloat16),
            pltpu.VMEM((d, tb), f32),
        ],
        compiler_params=_params("parallel", "arbitrary"),
        name="peer",
    )(x, mod, wq_t, subkeys, u_b, vt_b, ln_g.reshape(1, d), ln_b.reshape(1, d))


def _rope_tables(seq, tb):
    half = HEAD_DIM // 4
    freqs = ROPE_THETA ** (-jnp.arange(half, dtype=jnp.float32) / half)
    t = jnp.arange(seq)
    cos_parts, sin_parts = [], []
    for pos in (t // GRID_W, t % GRID_W):
        ang = pos.astype(jnp.float32)[:, None] * freqs[None, :]
        c, s = jnp.cos(ang), jnp.sin(ang)
        cos_parts += [c, c]
        sin_parts += [-s, s]
    cos_h = jnp.concatenate(cos_parts, axis=1)
    sin_h = jnp.concatenate(sin_parts, axis=1)
    n_rot = QK_W // HEAD_DIM
    cos_t = jnp.concatenate([jnp.tile(cos_h, (1, n_rot)), jnp.ones((tb, QK_W), jnp.float32)], axis=0)
    sin_t = jnp.concatenate([jnp.tile(sin_h, (1, n_rot)), jnp.zeros((tb, QK_W), jnp.float32)], axis=0)
    return cos_t, sin_t


def _pad_w_in(w):
    split = w.shape[1] - CV_W
    z = jnp.zeros((w.shape[0], IN_PAD - w.shape[1]), w.dtype)
    return jnp.concatenate([w[:, :split], z, w[:, split:]], axis=1)


def _forward(x_prompt, x_sample, cache_k, cache_v, state_rwkv, c, c_ctx, ln_in_g, ln_in_b, w_ada, b_ada,
             w_in, w_out, att_sink, rw_w0, rw_w_up, rw_a0, rw_a_up, rw_g_up, rw_kk, rw_ka, rw_rk,
             rw_lnx_g, rw_lnx_b, conv_w, ln1_g, ln1_b, ln2_g, ln2_b, peer_wq, peer_subkeys, peer_u, peer_v,
             tb_in=512, tb_out=256, tb_peer=512):
    nb_c, seq_c, d = x_prompt.shape
    nb_l, seq_l, _ = x_sample.shape
    depth = w_in.shape[0]
    n_ctx = nb_c * seq_c
    past = cache_k.shape[2]
    W = RWKV_DIM

    x = jnp.concatenate([x_prompt.reshape(n_ctx, d), x_sample.reshape(nb_l * seq_l, d)], axis=0)
    x = _ln_call(x, ln_in_g, ln_in_b, tb_in)

    n_groups = 1 + nb_l
    g_pad = -(-n_groups // SUBLANES) * SUBLANES
    cvec = jnp.concatenate([c_ctx[None], c, jnp.zeros((g_pad - n_groups, d), jnp.float32)], axis=0)
    mod_all = _mod_call(cvec, w_ada, b_ada).reshape(depth, g_pad, 6, d)

    cos_t, sin_t = _rope_tables(seq_l, tb_in)
    consts = tuple(jnp.asarray(a) for a in _rwkv_consts())
    ck_all = cache_k.reshape(nb_l, depth, past, KV_DIM)
    cv_all = cache_v.reshape(nb_l, depth, past, KV_DIM)
    eye_h = jnp.eye(RWKV_HEADS, dtype=jnp.float32)
    s0_bd_all = jnp.einsum('bldhij,hg->bldhigj', state_rwkv, eye_h).reshape(nb_l, depth, 2, W, W)
    s0_zero = jnp.zeros((nb_c, 2, W, W), jnp.float32)

    ks, vs, sts = [], [], []
    for l in range(depth):
        mod = mod_all[l]
        p_att, p_rw, p_cv = _in_proj_call(x, mod, _bf(_pad_w_in(w_in[l])), cos_t, sin_t, tb_in, n_ctx, seq_l)
        att_c = _ctx_attn_call(p_att, att_sink[l], nb_c, seq_c)
        att_l = _lat_attn_call(p_att, ck_all, cv_all, att_sink[l], l, nb_l, seq_l, n_ctx)
        wts = _rwkv_weights(rw_w0[l], rw_w_up[l], rw_a0[l], rw_a_up[l], rw_g_up[l], rw_kk[l], rw_ka[l],
                            rw_rk[l], rw_lnx_g[l], rw_lnx_b[l])
        rw_c, sfin = _rwkv_call(p_rw, s0_zero, consts, wts, nb_c, seq_c, 0)
        rw_l, _ = _rwkv_call(p_rw, s0_bd_all[:, l], consts, wts, nb_l, seq_l, n_ctx)
        x1 = _out_proj_call(att_c, att_l, rw_c, rw_l, p_cv, x, mod, _bf(w_out[l]), conv_w[l],
                            ln1_g[l], ln1_b[l], tb_out, n_ctx, seq_c, seq_l)
        sk = peer_subkeys[l].reshape(2 * PEER_HEADS, PEER_NKEYS, PEER_HALF)
        vt = _bf(peer_v[l]).reshape(-1, SUBLANES * PEER_NKEYS, d).transpose(0, 2, 1)
        x = _peer_call(x1, mod, _bf(peer_wq[l].T), sk, _bf(peer_u[l]), vt,
                       ln2_g[l], ln2_b[l], tb_peer, n_ctx, seq_l)
        ks.append(p_att[:n_ctx, ATT_DIM:ATT_DIM + KV_DIM].reshape(nb_c, seq_c, ATT_KV_HEADS, HEAD_DIM))
        vs.append(p_att[:n_ctx, ATT_DIM + KV_DIM:].reshape(nb_c, seq_c, ATT_KV_HEADS, HEAD_DIM))
        s5 = sfin.reshape(nb_c, 2, RWKV_HEADS, HEAD_DIM, RWKV_HEADS, HEAD_DIM)
        sts.append(jnp.stack([s5[:, :, h, :, h, :] for h in range(RWKV_HEADS)], axis=2))

    y_prompt = x[:n_ctx].reshape(nb_c, seq_c, d)
    y_sample = x[n_ctx:].reshape(nb_l, seq_l, d)
    return (y_prompt, y_sample, jnp.stack(ks, axis=1), jnp.stack(vs, axis=1), jnp.stack(sts, axis=1))


def kernel(x_prompt, x_sample, cache_k, cache_v, state_rwkv, c, c_ctx, ln_in_g, ln_in_b, w_ada, b_ada, w_in, w_out, att_sink, rw_w0, rw_w_up, rw_a0, rw_a_up, rw_g_up, rw_kk, rw_ka, rw_rk, rw_lnx_g, rw_lnx_b, conv_w, ln1_g, ln1_b, ln2_g, ln2_b, peer_wq, peer_subkeys, peer_u, peer_v):
    return _forward(x_prompt, x_sample, cache_k, cache_v, state_rwkv, c, c_ctx, ln_in_g, ln_in_b, w_ada, b_ada,
                    w_in, w_out, att_sink, rw_w0, rw_w_up, rw_a0, rw_a_up, rw_g_up, rw_kk, rw_ka, rw_rk,
                    rw_lnx_g, rw_lnx_b, conv_w, ln1_g, ln1_b, ln2_g, ln2_b, peer_wq, peer_subkeys, peer_u, peer_v)
```

```python
import functools
import math

import jax
import jax.numpy as jnp
import numpy as np
from jax import lax
from jax.experimental import pallas as pl
from jax.experimental.pallas import tpu as pltpu

D_MODEL = 1024
DEPTH = 4
GRID_W = 64
HEAD_DIM = 64
ATT_DIM = 512
RWKV_DIM = 256
CONV_DIM = 256
ATT_HEADS = 8
ATT_KV_HEADS = 2
ATT_GROUP = 4
KV_DIM = 128
WINDOW = 128
ATT_BLOCK = 128
ROPE_THETA = 10000.0
RWKV_HEADS = 4
W_RANK = 32
A_RANK = 32
G_RANK = 64
RWKV_LNX_EPS = 64e-5
PEER_HEADS = 8
PEER_NKEYS = 128
PEER_TOPK = 16
PEER_HALF = 128
LN_EPS = 1e-5
ALPHA = (2 * DEPTH) ** 0.25
NEG = -1e30

LANES = 128
SUBLANES = 8
VMEM_LIMIT = 56 * 1024 * 1024

QK_W = ATT_DIM + KV_DIM
ATT_W = ATT_DIM + 2 * KV_DIM
RW_W = 1024
CV_W = 3 * CONV_DIM
IN_PAD = ATT_W + RW_W + CV_W
RW_LOW = 3 * RWKV_DIM

RW_CHUNK = 64
PEER_SCORE_PASSES = 3
HIGHEST = lax.Precision.HIGHEST
NN = (((1,), (0,)), ((), ()))
NT = (((1,), (1,)), ((), ()))
TN = (((0,), (0,)), ((), ()))


def _mm(a, b, dims=NN, precision=None):
    return lax.dot_general(a, b, dims, precision=precision, preferred_element_type=jnp.float32)


def _bf(x):
    return x.astype(jnp.bfloat16)


def _split(x):
    hi = _bf(x)
    return hi, _bf(x - hi.astype(jnp.float32))


def _mm_split(a, b, dims, passes):
    out = _mm(a[0], b[0], dims)
    if passes == 3:
        out = out + (_mm(a[0], b[1], dims) + _mm(a[1], b[0], dims))
    return out


def _mm_terms(x, exact, terms, x_is_rhs=False):
    out = None
    rest = x
    for _ in range(terms):
        piece = _bf(rest)
        rest = rest - piece.astype(jnp.float32)
        part = _mm(exact, piece) if x_is_rhs else _mm(piece, exact)
        out = part if out is None else out + part
    return out


RW_PASSES = {"gram_n": 1, "gram": 1, "inv": 1, "loc": 1, "seq": 1}


def _layer_norm(x, g, b, eps):
    mu = jnp.mean(x, axis=-1, keepdims=True)
    xc = x - mu
    var = jnp.mean(xc * xc, axis=-1, keepdims=True)
    return xc * lax.rsqrt(var + eps) * g + b


def _params(*sem):
    return pltpu.CompilerParams(dimension_semantics=sem, vmem_limit_bytes=VMEM_LIMIT)


def _full(shape):
    nd = len(shape)
    return pl.BlockSpec(shape, lambda *_: (0,) * nd)


def _mod_kernel(c_ref, w_ref, b_ref, o_ref):
    cv = c_ref[...]
    s = cv * jax.nn.sigmoid(cv)
    o_ref[...] = _mm(s, w_ref[...], precision=HIGHEST) + b_ref[...]


def _mod_call(cvec, w_ada, b_ada):
    depth, d, n6 = w_ada.shape
    rows = cvec.shape[0]
    nt = n6 // d
    return pl.pallas_call(
        _mod_kernel,
        grid=(depth, nt),
        in_specs=[
            pl.BlockSpec((rows, d), lambda l, n: (0, 0)),
            pl.BlockSpec((None, d, d), lambda l, n: (l, 0, n)),
            pl.BlockSpec((None, 1, d), lambda l, n: (l, 0, n)),
        ],
        out_specs=pl.BlockSpec((None, rows, d), lambda l, n: (l, 0, n)),
        out_shape=jax.ShapeDtypeStruct((depth, rows, n6), jnp.float32),
        compiler_params=_params("parallel", "parallel"),
        name="adaln_mod",
    )(cvec, w_ada, b_ada.reshape(depth, 1, n6))


def _ln_kernel(x_ref, g_ref, b_ref, o_ref):
    o_ref[...] = _layer_norm(x_ref[...], g_ref[...], b_ref[...], LN_EPS)


def _ln_call(x, g, b, tb):
    rows, d = x.shape
    return pl.pallas_call(
        _ln_kernel,
        grid=(rows // tb,),
        in_specs=[pl.BlockSpec((tb, d), lambda i: (i, 0)), _full((1, d)), _full((1, d))],
        out_specs=pl.BlockSpec((tb, d), lambda i: (i, 0)),
        out_shape=jax.ShapeDtypeStruct((rows, d), jnp.float32),
        compiler_params=_params("parallel"),
        name="ln_in",
    )(x, g.reshape(1, d), b.reshape(1, d))


def _group_of_block(i, n_ctx_blocks, blocks_per_lat_seq):
    return jnp.where(i < n_ctx_blocks, 0, 1 + (i - n_ctx_blocks) // blocks_per_lat_seq)


def _in_proj_kernel(x_ref, mod_ref, w_ref, cos_ref, sin_ref, att_ref, rw_ref, cv_ref):
    x = x_ref[...]
    h = x * (1.0 + mod_ref[1:2, :]) + mod_ref[0:1, :]
    p = _mm(_bf(h), w_ref[...])
    qk = p[:, :QK_W]
    lane = lax.broadcasted_iota(jnp.int32, qk.shape, 1)
    partner = jnp.where((lane % 32) < 16,
                        pltpu.roll(qk, QK_W - 16, 1),
                        pltpu.roll(qk, 16, 1))
    att_ref[:, :QK_W] = qk * cos_ref[...] + partner * sin_ref[...]
    att_ref[:, QK_W:] = p[:, QK_W:ATT_W]
    rw_ref[...] = p[:, ATT_W:ATT_W + RW_W]
    cv_ref[...] = p[:, ATT_W + RW_W:]


def _in_proj_call(x, mod, w_in_p, cos_t, sin_t, tb, n_ctx_rows, lat_seq):
    rows, d = x.shape
    ncb = n_ctx_rows // tb
    bps = lat_seq // tb

    def mod_map(i):
        return (_group_of_block(i, ncb, bps), 0, 0)

    def rope_map(i):
        return (jnp.where(i < ncb, bps, (i - ncb) % bps), 0)

    return pl.pallas_call(
        _in_proj_kernel,
        grid=(rows // tb,),
        in_specs=[
            pl.BlockSpec((tb, d), lambda i: (i, 0)),
            pl.BlockSpec((None, 6, d), mod_map),
            _full((d, IN_PAD)),
            pl.BlockSpec((tb, QK_W), rope_map),
            pl.BlockSpec((tb, QK_W), rope_map),
        ],
        out_specs=[
            pl.BlockSpec((tb, ATT_W), lambda i: (i, 0)),
            pl.BlockSpec((tb, RW_W), lambda i: (i, 0)),
            pl.BlockSpec((tb, CV_W), lambda i: (i, 0)),
        ],
        out_shape=[
            jax.ShapeDtypeStruct((rows, ATT_W), jnp.float32),
            jax.ShapeDtypeStruct((rows, RW_W), jnp.float32),
            jax.ShapeDtypeStruct((rows, CV_W), jnp.float32),
        ],
        compiler_params=_params("parallel"),
        name="in_proj",
    )(x, mod, w_in_p, cos_t, sin_t)


def _ctx_attn_kernel(sink_ref, p_ref, o_ref):
    scale = HEAD_DIM ** -0.5
    for kv in range(ATT_KV_HEADS):
        k = _bf(p_ref[:, ATT_DIM + kv * HEAD_DIM:ATT_DIM + (kv + 1) * HEAD_DIM])
        v = _bf(p_ref[:, ATT_DIM + KV_DIM + kv * HEAD_DIM:ATT_DIM + KV_DIM + (kv + 1) * HEAD_DIM])
        for g in range(ATT_GROUP):
            hd = kv * ATT_GROUP + g
            q = _bf(p_ref[:, hd * HEAD_DIM:(hd + 1) * HEAD_DIM])
            s = _mm(q, k, NT) * scale
            sink = sink_ref[hd]
            m = jnp.maximum(jnp.max(s, axis=-1, keepdims=True), sink)
            e = jnp.exp(s - m)
            den = jnp.sum(e, axis=-1, keepdims=True) + jnp.exp(sink - m)
            o_ref[:, hd * HEAD_DIM:(hd + 1) * HEAD_DIM] = _mm(_bf(e), v) / den


def _ctx_attn_call(p_att, sink, n_seq, seq):
    return pl.pallas_call(
        _ctx_attn_kernel,
        grid=(n_seq,),
        in_specs=[
            pl.BlockSpec(memory_space=pltpu.SMEM),
            pl.BlockSpec((seq, ATT_W), lambda b: (b, 0)),
        ],
        out_specs=pl.BlockSpec((seq, ATT_DIM), lambda b: (b, 0)),
        out_shape=jax.ShapeDtypeStruct((n_seq * seq, ATT_DIM), jnp.float32),
        compiler_params=_params("parallel"),
        name="ctx_attn",
    )(sink, p_att)


def _lat_attn_kernel(sink_ref, own_ref, prev_ref, next_ref, ck_ref, cv_ref, o_ref, *, n_blocks):
    n = pl.program_id(1)
    scale = HEAD_DIM ** -0.5
    qi = lax.broadcasted_iota(jnp.int32, (ATT_BLOCK, 3 * ATT_BLOCK), 0)
    kj = lax.broadcasted_iota(jnp.int32, (ATT_BLOCK, 3 * ATT_BLOCK), 1)
    rel = kj - ATT_BLOCK - qi
    kpos = (n - 1) * ATT_BLOCK + kj
    mask = (jnp.abs(rel) <= WINDOW) & (kpos >= 0) & (kpos < n_blocks * ATT_BLOCK)
    mask4 = jnp.concatenate([mask] * ATT_GROUP, axis=0)
    row_group = lax.broadcasted_iota(jnp.int32, (ATT_GROUP * ATT_BLOCK, 1), 0) // ATT_BLOCK
    for kv in range(ATT_KV_HEADS):
        ks = slice(ATT_DIM + kv * HEAD_DIM, ATT_DIM + (kv + 1) * HEAD_DIM)
        vs = slice(ATT_DIM + KV_DIM + kv * HEAD_DIM, ATT_DIM + KV_DIM + (kv + 1) * HEAD_DIM)
        kw = _bf(jnp.concatenate([prev_ref[:, ks], own_ref[:, ks], next_ref[:, ks]], axis=0))
        vw = _bf(jnp.concatenate([prev_ref[:, vs], own_ref[:, vs], next_ref[:, vs]], axis=0))
        ck = _bf(ck_ref[:, kv * HEAD_DIM:(kv + 1) * HEAD_DIM])
        cv = _bf(cv_ref[:, kv * HEAD_DIM:(kv + 1) * HEAD_DIM])
        q4 = _bf(jnp.concatenate(
            [own_ref[:, (kv * ATT_GROUP + g) * HEAD_DIM:(kv * ATT_GROUP + g + 1) * HEAD_DIM]
             for g in range(ATT_GROUP)], axis=0))
        s_ctx = _mm(q4, ck, NT) * scale
        s_win = jnp.where(mask4, _mm(q4, kw, NT) * scale, NEG)
        sink = jnp.full((ATT_GROUP * ATT_BLOCK, 1), sink_ref[kv * ATT_GROUP], jnp.float32)
        for g in range(1, ATT_GROUP):
            sink = jnp.where(row_group == g, sink_ref[kv * ATT_GROUP + g], sink)
        m = jnp.maximum(jnp.maximum(jnp.max(s_ctx, axis=-1, keepdims=True),
                                    jnp.max(s_win, axis=-1, keepdims=True)), sink)
        e_ctx = jnp.exp(s_ctx - m)
        e_win = jnp.exp(s_win - m)
        den = (jnp.sum(e_ctx, axis=-1, keepdims=True) + jnp.sum(e_win, axis=-1, keepdims=True)
               + jnp.exp(sink - m))
        o4 = (_mm(_bf(e_ctx), cv) + _mm(_bf(e_win), vw)) / den
        for g in range(ATT_GROUP):
            hd = kv * ATT_GROUP + g
            o_ref[:, hd * HEAD_DIM:(hd + 1) * HEAD_DIM] = o4[g * ATT_BLOCK:(g + 1) * ATT_BLOCK]


def _lat_attn_call(p_att, ck, cv, sink, layer, n_seq, seq, row_off):
    nb = seq // ATT_BLOCK
    off = row_off // ATT_BLOCK
    past = ck.shape[2]

    def own(b, n):
        return (off + b * nb + n, 0)

    def prev(b, n):
        return (off + b * nb + jnp.maximum(n - 1, 0), 0)

    def nxt(b, n):
        return (off + b * nb + jnp.minimum(n + 1, nb - 1), 0)

    return pl.pallas_call(
        functools.partial(_lat_attn_kernel, n_blocks=nb),
        grid=(n_seq, nb),
        in_specs=[
            pl.BlockSpec(memory_space=pltpu.SMEM),
            pl.BlockSpec((ATT_BLOCK, ATT_W), own),
            pl.BlockSpec((ATT_BLOCK, ATT_W), prev),
            pl.BlockSpec((ATT_BLOCK, ATT_W), nxt),
            pl.BlockSpec((None, None, past, KV_DIM), lambda b, n: (b, layer, 0, 0)),
            pl.BlockSpec((None, None, past, KV_DIM), lambda b, n: (b, layer, 0, 0)),
        ],
        out_specs=pl.BlockSpec((ATT_BLOCK, ATT_DIM), lambda b, n: (b * nb + n, 0)),
        out_shape=jax.ShapeDtypeStruct((n_seq * seq, ATT_DIM), jnp.float32),
        compiler_params=_params("parallel", "parallel"),
        name="lat_attn",
    )(sink, p_att, p_att, p_att, ck, cv)


def _rwkv_consts():
    L, H, W = RW_CHUNK, RWKV_HEADS, RWKV_DIM
    t = np.arange(L)
    inc = np.stack([(t[None, :] <= t[:, None]), (t[None, :] >= t[:, None])]).astype(np.float32)
    stc = np.stack([(t[None, :] < t[:, None]), (t[None, :] > t[:, None])]).astype(np.float32)
    eye_h = np.eye(H, dtype=np.float32)
    inc_bd = np.stack([np.kron(eye_h, inc[d]) for d in range(2)])
    stc_bd = np.stack([np.kron(eye_h, stc[d]) for d in range(2)])
    blk = np.kron(eye_h, np.ones((L, HEAD_DIM), np.float32))
    eye = np.eye(W, dtype=np.float32)
    return inc, inc_bd, stc_bd, blk, eye


def _rwkv_kernel(p_ref, s0_ref, inc_ref, incbd_ref, stcbd_ref, blk_ref, eye_ref,
                 w0_ref, wup_ref, a0_ref, aup_ref, gup_ref, kkw_ref, kaw_ref, rkw_ref, lng_ref, lnb_ref,
                 o_ref, sfin_ref, y_scr, bon_scr, *, seq, n_par):
    L = RW_CHUNK
    nc = seq // L
    blk = _bf(blk_ref[...])
    ones_bd = blk
    eye = eye_ref[...]
    y_scr[...] = jnp.zeros_like(y_scr)
    bon_scr[...] = jnp.zeros_like(bon_scr)

    def stack(x):
        return tuple(jnp.concatenate([z] * RWKV_HEADS, axis=0) * blk for z in _split(x))

    def chunk(idx, r0, s_in, s_out):
        d = idx % 2
        x = p_ref[pl.ds(r0, L), :]
        rr = x[:, 0:RWKV_DIM]
        rk = x[:, RWKV_DIM:2 * RWKV_DIM]
        rv = x[:, 2 * RWKV_DIM:3 * RWKV_DIM]
        low = x[:, RW_LOW:RW_LOW + LANES]
        zw = w0_ref[d] + _mm_split(_split(jnp.tanh(low)), _split(wup_ref[d]), NN, 3)
        yield
        u = -zw
        softplus = jnp.maximum(u, 0.0) + jnp.log1p(jnp.exp(-jnp.abs(u)))
        lw = -jnp.exp(-softplus - 0.5)
        asig = jax.nn.sigmoid(a0_ref[d] + _mm_split(_split(low), _split(aup_ref[d]), NN, 3))
        yield
        kd = rk * (1.0 + (asig - 1.0) * kaw_ref[...])
        kkr = rk * kkw_ref[...]
        kk = kkr * lax.rsqrt(jnp.maximum(_mm_terms(kkr * kkr, ones_bd, 2), 1e-24))
        yield
        a_s = -kk
        b_s = kk * asig
        bon_scr[pl.ds(r0, L), :] += _mm_terms(rr * kd * rkw_ref[...], ones_bd, 2) * rv
        yield

        cl = _mm_terms(lw, _bf(inc_ref[d]), 3, x_is_rhs=True)
        yield
        cl_end = cl[L - 1:L, :] if d == 0 else cl[0:1, :]
        e_neg = jnp.exp(-cl)
        e_tail = jnp.exp(cl_end - cl)
        a_st = stack(a_s * jnp.exp(cl - lw))
        b_st = stack(b_s * e_neg)
        k_st = stack(kd * e_neg)
        r_st = stack(rr * jnp.exp(cl))
        v_st = stack(rv)
        bh_st = stack(b_s * e_tail)
        kh_st = stack(kd * e_tail)
        stc = stcbd_ref[d]
        inc = incbd_ref[d]
        n_m = stc * _mm_split(a_st, b_st, NT, RW_PASSES["gram_n"])
        yield
        m1 = stc * _mm_split(a_st, k_st, NT, RW_PASSES["gram"])
        yield
        fill = {}
        fillers = [
            lambda: fill.update(p=inc * _mm_split(r_st, b_st, NT, RW_PASSES["gram"])),
            lambda: fill.update(m1v=_mm_split(_split(m1), v_st, NN, RW_PASSES["loc"])),
            lambda: fill.update(q=inc * _mm_split(r_st, k_st, NT, RW_PASSES["gram"])),
            lambda: fill.update(vk=_mm_split(v_st, kh_st, TN, RW_PASSES["loc"])),
            lambda: fill.update(rs=_mm_split(r_st, _split(s_in), NT, RW_PASSES["seq"])),
        ]
        t_m = eye + n_m
        n_p = _split(n_m)
        for it in range(5):
            n_sq = _mm_split(n_p, n_p, NN, RW_PASSES["inv"])
            yield
            fillers[it]()
            yield
            n_p = _split(n_sq)
            t_m = t_m + _mm_split(_split(t_m), n_p, NN, RW_PASSES["inv"])
            yield
        t_s = _split(t_m)
        qv = _mm_split(_split(fill["q"]), v_st, NN, RW_PASSES["loc"])
        yield
        u_loc = _mm_split(t_s, _split(fill["m1v"]), NN, RW_PASSES["loc"])
        yield
        w_t = _mm_split(t_s, a_st, NN, RW_PASSES["loc"])
        yield
        u_m = _mm_split(_split(w_t), _split(s_in), NT, RW_PASSES["seq"]) + u_loc
        yield
        u_s = _split(u_m)
        y_bd = fill["rs"] + _mm_split(_split(fill["p"]), u_s, NN, RW_PASSES["seq"]) + qv
        y = y_bd[0:L] + y_bd[L:2 * L] + y_bd[2 * L:3 * L] + y_bd[3 * L:4 * L]
        y_scr[pl.ds(r0, L), :] += y
        yield
        s_out[idx] = s_in * jnp.exp(cl_end) + _mm_split(u_s, bh_st, TN, RW_PASSES["seq"]) + fill["vk"]

    def body(c, carry):
        s_out = {}
        stages = []
        for sq in range(n_par):
            stages.append(chunk(2 * sq, pl.multiple_of(sq * seq + c * L, L), carry[2 * sq], s_out))
            stages.append(chunk(2 * sq + 1, pl.multiple_of(sq * seq + (nc - 1 - c) * L, L), carry[2 * sq + 1],
                                s_out))
        while stages:
            for gen in list(stages):
                if next(gen, StopIteration) is StopIteration:
                    stages.remove(gen)
        return tuple(s_out[i] for i in range(2 * n_par))

    s_fin = lax.fori_loop(0, nc, body, tuple(s0_ref[i // 2, i % 2] for i in range(2 * n_par)))
    for i in range(2 * n_par):
        sfin_ref[i // 2, i % 2] = s_fin[i]

    inv_n = 1.0 / HEAD_DIM

    def epilogue(c, carry):
        r0 = pl.multiple_of(c * L, L)
        y = y_scr[pl.ds(r0, L), :]
        mu = _mm_terms(y, ones_bd, 3) * inv_n
        yc = y - mu
        var = _mm_terms(yc * yc, ones_bd, 2) * inv_n
        yn = yc * lax.rsqrt(var + RWKV_LNX_EPS) * lng_ref[...] + lnb_ref[...]
        gd = p_ref[pl.ds(r0, L), RW_LOW + LANES:RW_LOW + 2 * LANES]
        g = _mm_split(_split(jax.nn.sigmoid(gd)), _split(gup_ref[...]), NN, 3)
        bon = bon_scr[pl.ds(r0, L), :]
        o_ref[pl.ds(r0, L), :] = (yn + bon) * g
        return carry

    lax.fori_loop(0, n_par * nc, epilogue, 0)


def _rwkv_call(p_rw, s0_bd, consts, wts, n_seq, seq, row_off, n_par, single_buffer_input):
    inc, inc_bd, stc_bd, blk, eye = consts
    W = RWKV_DIM
    rows = n_par * seq
    off = row_off // rows
    p_mode = dict(pipeline_mode=pl.Buffered(1)) if single_buffer_input else {}
    in_specs = [
        pl.BlockSpec((rows, RW_W), lambda b: (off + b, 0), **p_mode),
        pl.BlockSpec((n_par, 2, W, W), lambda b: (b, 0, 0, 0)),
        _full(inc.shape), _full(inc_bd.shape), _full(stc_bd.shape), _full(blk.shape), _full(eye.shape),
    ] + [_full(w.shape) for w in wts]
    return pl.pallas_call(
        functools.partial(_rwkv_kernel, seq=seq, n_par=n_par),
        grid=(n_seq // n_par,),
        in_specs=in_specs,
        out_specs=[
            pl.BlockSpec((rows, W), lambda b: (b, 0)),
            pl.BlockSpec((n_par, 2, W, W), lambda b: (b, 0, 0, 0)),
        ],
        out_shape=[
            jax.ShapeDtypeStruct((n_seq * seq, W), jnp.float32),
            jax.ShapeDtypeStruct((n_seq, 2, W, W), jnp.float32),
        ],
        scratch_shapes=[pltpu.VMEM((rows, W), jnp.float32), pltpu.VMEM((rows, W), jnp.float32)],
        compiler_params=_params("parallel"),
        name="rwkv",
    )(p_rw, s0_bd, inc, inc_bd, stc_bd, blk, eye, *wts)


def _rwkv_weights(w0, w_up, a0, a_up, g_up, kk, ka, rk, lng, lnb):
    W = RWKV_DIM
    wup_p = jnp.zeros((2, LANES, W), jnp.float32)
    aup_p = jnp.zeros((2, LANES, W), jnp.float32)
    for d in range(2):
        wup_p = wup_p.at[d, d * W_RANK:(d + 1) * W_RANK].set(w_up[d])
        aup_p = aup_p.at[d, 2 * W_RANK + d * A_RANK:2 * W_RANK + (d + 1) * A_RANK].set(a_up[d])
    gup_p = jnp.zeros((LANES, W), jnp.float32).at[:G_RANK].set(g_up)
    row = lambda v: v.reshape(1, W)
    return (w0.reshape(2, 1, W), wup_p, a0.reshape(2, 1, W), aup_p, gup_p,
            row(kk), row(ka), row(rk), row(lng), row(lnb))


def _out_proj_kernel(attc_ref, attl_ref, rwc_ref, rwl_ref, cv_ref, cvp_ref, cvn_ref, x_ref, mod_ref,
                     w_ref, cw_ref, g_ref, b_ref, o_ref, *, tb, ncb, bps_c, bps):
    i = pl.program_id(0)
    is_ctx = i < ncb
    pos = jnp.where(is_ctx, i % bps_c, (i - ncb) % bps)
    last = jnp.where(is_ctx, bps_c - 1, bps - 1)
    att = jnp.where(is_ctx, attc_ref[...], attl_ref[...])
    rw = jnp.where(is_ctx, rwc_ref[...], rwl_ref[...])

    cb = cv_ref[:, 0:CONV_DIM]
    z = cv_ref[:, CONV_DIM:2 * CONV_DIM] * cv_ref[:, 2 * CONV_DIM:3 * CONV_DIM]
    halo_p = cvp_ref[SUBLANES - 1:SUBLANES, CONV_DIM:2 * CONV_DIM] * cvp_ref[SUBLANES - 1:SUBLANES, 2 * CONV_DIM:]
    halo_n = cvn_ref[0:1, CONV_DIM:2 * CONV_DIM] * cvn_ref[0:1, 2 * CONV_DIM:]
    halo_p = jnp.where(pos > 0, halo_p, 0.0)
    halo_n = jnp.where(pos < last, halo_n, 0.0)
    row = lax.broadcasted_iota(jnp.int32, z.shape, 0)
    z_prev = jnp.where(row == 0, halo_p, pltpu.roll(z, 1, 0))
    z_next = jnp.where(row == tb - 1, halo_n, pltpu.roll(z, tb - 1, 0))
    conv = cb * (cw_ref[0:1, :] * z_prev + cw_ref[1:2, :] * z + cw_ref[2:3, :] * z_next)

    mix = (_mm(_bf(att), w_ref[0:ATT_DIM, :])
           + _mm(_bf(rw), w_ref[ATT_DIM:ATT_DIM + RWKV_DIM, :])
           + _mm(_bf(conv), w_ref[ATT_DIM + RWKV_DIM:, :]))
    o_ref[...] = _layer_norm(ALPHA * x_ref[...] + mod_ref[2:3, :] * mix, g_ref[...], b_ref[...], LN_EPS)


def _out_proj_call(att_c, att_l, rw_c, rw_l, p_cv, x, mod, w_out_b, conv_w, ln_g, ln_b, tb, n_ctx_rows, ctx_seq,
                   lat_seq):
    rows, d = x.shape
    ncb = n_ctx_rows // tb
    bps = lat_seq // tb
    sub = tb // SUBLANES
    n_sub = rows // SUBLANES

    def ctx_map(i):
        return (jnp.minimum(i, ncb - 1), 0)

    def lat_map(i):
        return (jnp.maximum(i - ncb, 0), 0)

    return pl.pallas_call(
        functools.partial(_out_proj_kernel, tb=tb, ncb=ncb, bps_c=ctx_seq // tb, bps=bps),
        grid=(rows // tb,),
        in_specs=[
            pl.BlockSpec((tb, ATT_DIM), ctx_map),
            pl.BlockSpec((tb, ATT_DIM), lat_map),
            pl.BlockSpec((tb, RWKV_DIM), ctx_map),
            pl.BlockSpec((tb, RWKV_DIM), lat_map),
            pl.BlockSpec((tb, CV_W), lambda i: (i, 0)),
            pl.BlockSpec((SUBLANES, CV_W), lambda i: (jnp.maximum(i * sub - 1, 0), 0)),
            pl.BlockSpec((SUBLANES, CV_W), lambda i: (jnp.minimum((i + 1) * sub, n_sub - 1), 0)),
            pl.BlockSpec((tb, d), lambda i: (i, 0)),
            pl.BlockSpec((None, 6, d), lambda i: (_group_of_block(i, ncb, bps), 0, 0)),
            _full((d, d)),
            _full((3, CONV_DIM)),
            _full((1, d)),
            _full((1, d)),
        ],
        out_specs=pl.BlockSpec((tb, d), lambda i: (i, 0)),
        out_shape=jax.ShapeDtypeStruct((rows, d), jnp.float32),
        compiler_params=_params("parallel"),
        name="out_proj",
    )(att_c, att_l, rw_c, rw_l, p_cv, p_cv, p_cv, x, mod, w_out_b, conv_w, ln_g.reshape(1, d), ln_b.reshape(1, d))


def _gelu_tanh(x):
    return 0.5 * x * (1.0 + jnp.tanh(math.sqrt(2.0 / math.pi) * (x + 0.044715 * (x * x * x))))


def _extract_top(work, iota, n_out, want_rank=False):
    vals = []
    big = work.shape[0]
    rank = jnp.full(work.shape, float(big), jnp.float32) if want_rank else None
    for k in range(n_out):
        m = jnp.max(work, axis=0, keepdims=True)
        idx = jnp.min(jnp.where(work == m, iota, big), axis=0, keepdims=True)
        sel = iota == idx
        work = jnp.where(sel, -jnp.inf, work)
        if want_rank:
            rank = jnp.where(sel, float(k), rank)
        vals.append(m)
    return vals, work, rank


def _peer_kernel(x_ref, mod_ref, wq_ref, sk_ref, u_ref, vt_ref, g_ref, b_ref, o_ref,
                 ht_scr, s0_scr, e0_scr, s1_scr, r1_scr, e1_scr, a_scr, z_scr, acc_scr, *, tb):
    e_step = pl.program_id(1)
    n_lt = tb // LANES
    K = PEER_TOPK

    @pl.when(e_step == 0)
    def _route():
        h2 = x_ref[...] * (1.0 + mod_ref[4:5, :]) + mod_ref[3:4, :]
        ht = _bf(h2.T)
        ht_scr[...] = ht
        qt = _mm(wq_ref[...], ht)
        for hp in range(2 * PEER_HEADS):
            sc = _mm_split(_split(sk_ref[hp]), _split(qt[hp * PEER_HALF:(hp + 1) * PEER_HALF, :]), NN,
                           PEER_SCORE_PASSES)
            if hp % 2 == 0:
                s0_scr[hp // 2] = sc
            else:
                s1_scr[hp // 2] = sc

        iota_k = lax.broadcasted_iota(jnp.int32, (PEER_NKEYS, LANES), 0)
        iota_c = lax.broadcasted_iota(jnp.int32, ((SUBLANES + 2) * SUBLANES, LANES), 0)

        def per_tile(j, carry):
            hd = j // n_lt
            lanes = pl.ds(pl.multiple_of((j % n_lt) * LANES, LANES), LANES)
            sc0 = s0_scr[hd, :, lanes]
            sc1 = s1_scr[hd, :, lanes]
            v0, w0, _ = _extract_top(sc0, iota_k, K)
            v1, w1, rank1 = _extract_top(sc1, iota_k, K, want_rank=True)
            cand0 = w0 != sc0
            cand1 = w1 != sc1
            lo1 = jnp.concatenate(v1[:SUBLANES], axis=0)
            hi1 = jnp.concatenate(v1[SUBLANES:], axis=0)
            hi0 = jnp.concatenate(v0[SUBLANES:], axis=0)
            pair = jnp.concatenate([v0[a] + lo1 for a in range(SUBLANES)] + [v0[0] + hi1, hi0 + v1[0]], axis=0)
            f, _, _ = _extract_top(pair, iota_c, K + 1)
            zsum = jnp.zeros_like(f[0])
            for k in range(K):
                zsum = zsum + jnp.exp(f[k] - f[0])
            th = 0.5 * (f[K - 1] + f[K]) - sc0
            count0 = jnp.zeros_like(sc0)
            for b in range(K):
                count0 = count0 + jnp.where(v1[b] > th, 1.0, 0.0)
            s0_scr[hd, :, lanes] = jnp.where(cand0, count0, 0.0)
            e0_scr[hd, :, lanes] = jnp.where(cand0, jnp.exp(sc0 - v0[0]), 0.0) / zsum
            e1_scr[hd, :, lanes] = _bf(jnp.where(cand1, jnp.exp(sc1 - v1[0]), 0.0))
            r1_scr[hd, :, lanes] = _bf(rank1)
            return carry

        lax.fori_loop(0, PEER_HEADS * n_lt, per_tile, 0)
        acc_scr[...] = jnp.zeros_like(acc_scr)

    a_scr[...] = _gelu_tanh(_mm(u_ref[...], ht_scr[...]))

    keys = pl.ds(pl.multiple_of(e_step * SUBLANES, SUBLANES), SUBLANES)
    half = PEER_NKEYS // 2
    group = SUBLANES // 2
    packed = 2 * SUBLANES
    tile3 = (half // packed, packed, LANES)
    for ln in range(n_lt):
        lanes = slice(ln * LANES, (ln + 1) * LANES)
        c0 = [s0_scr[hd, keys, lanes] for hd in range(PEER_HEADS)]
        e0 = [e0_scr[hd, keys, lanes] for hd in range(PEER_HEADS)]
        for jh in range(2):
            jrows = slice(jh * half, (jh + 1) * half)
            for ig in range(SUBLANES // group):
                g = [jnp.zeros(tile3, jnp.bfloat16) for _ in range(group)]
                for hd in range(PEER_HEADS):
                    r1 = r1_scr[hd, jrows, lanes].reshape(tile3)
                    e1 = e1_scr[hd, jrows, lanes].reshape(tile3)
                    for k in range(group):
                        ii = ig * group + k
                        cnt = _bf(jnp.broadcast_to(c0[hd][ii:ii + 1, :], tile3[1:]))[None]
                        w0 = _bf(jnp.broadcast_to(e0[hd][ii:ii + 1, :], tile3[1:]))[None]
                        g[k] = g[k] + jnp.where(r1 < cnt, e1, jnp.zeros_like(e1)) * w0
                for k in range(group):
                    r0 = (ig * group + k) * PEER_NKEYS + jh * half
                    z_scr[r0:r0 + half, lanes] = g[k].reshape(half, LANES) * _bf(a_scr[r0:r0 + half, lanes])

    acc_scr[...] += _mm(vt_ref[...], z_scr[...])

    @pl.when(e_step == pl.num_programs(1) - 1)
    def _finish():
        out = acc_scr[...].T
        o_ref[...] = _layer_norm(ALPHA * x_ref[...] + mod_ref[5:6, :] * out, g_ref[...], b_ref[...], LN_EPS)


def _peer_call(x, mod, wq_t, subkeys, u_b, vt_b, ln_g, ln_b, tb, n_ctx_rows, lat_seq):
    rows, d = x.shape
    et = SUBLANES * PEER_NKEYS
    ncb = n_ctx_rows // tb
    bps = lat_seq // tb
    nq = wq_t.shape[0]
    f32 = jnp.float32
    return pl.pallas_call(
        functools.partial(_peer_kernel, tb=tb),
        grid=(rows // tb, u_b.shape[0] // et),
        in_specs=[
            pl.BlockSpec((tb, d), lambda i, e: (i, 0)),
            pl.BlockSpec((None, 6, d), lambda i, e: (_group_of_block(i, ncb, bps), 0, 0)),
            pl.BlockSpec((nq, d), lambda i, e: (0, 0), pipeline_mode=pl.Buffered(1)),
            pl.BlockSpec(subkeys.shape, lambda i, e: (0, 0, 0), pipeline_mode=pl.Buffered(1)),
            pl.BlockSpec((et, d), lambda i, e: (e, 0)),
            pl.BlockSpec((None, d, et), lambda i, e: (e, 0, 0)),
            pl.BlockSpec((1, d), lambda i, e: (0, 0)),
            pl.BlockSpec((1, d), lambda i, e: (0, 0)),
        ],
        out_specs=pl.BlockSpec((tb, d), lambda i, e: (i, 0)),
        out_shape=jax.ShapeDtypeStruct((rows, d), f32),
        scratch_shapes=[
            pltpu.VMEM((d, tb), jnp.bfloat16),
            pltpu.VMEM((PEER_HEADS, PEER_NKEYS, tb), f32),
            pltpu.VMEM((PEER_HEADS, PEER_NKEYS, tb), f32),
            pltpu.VMEM((PEER_HEADS, PEER_NKEYS, tb), f32),
            pltpu.VMEM((PEER_HEADS, PEER_NKEYS, tb), jnp.bfloat16),
            pltpu.VMEM((PEER_HEADS, PEER_NKEYS, tb), jnp.bfloat16),
            pltpu.VMEM((et, tb), f32),
            pltpu.VMEM((et, tb), jnp.bfloat16),
            pltpu.VMEM((d, tb), f32),
        ],
        compiler_params=_params("parallel", "arbitrary"),
        name="peer",
    )(x, mod, wq_t, subkeys, u_b, vt_b, ln_g.reshape(1, d), ln_b.reshape(1, d))


def _rope_tables(seq, tb):
    half = HEAD_DIM // 4
    freqs = ROPE_THETA ** (-jnp.arange(half, dtype=jnp.float32) / half)
    t = jnp.arange(seq)
    cos_parts, sin_parts = [], []
    for pos in (t // GRID_W, t % GRID_W):
        ang = pos.astype(jnp.float32)[:, None] * freqs[None, :]
        c, s = jnp.cos(ang), jnp.sin(ang)
        cos_parts += [c, c]
        sin_parts += [-s, s]
    cos_h = jnp.concatenate(cos_parts, axis=1)
    sin_h = jnp.concatenate(sin_parts, axis=1)
    n_rot = QK_W // HEAD_DIM
    cos_t = jnp.concatenate([jnp.tile(cos_h, (1, n_rot)), jnp.ones((tb, QK_W), jnp.float32)], axis=0)
    sin_t = jnp.concatenate([jnp.tile(sin_h, (1, n_rot)), jnp.zeros((tb, QK_W), jnp.float32)], axis=0)
    return cos_t, sin_t


def _pad_w_in(w):
    split = w.shape[1] - CV_W
    z = jnp.zeros((w.shape[0], IN_PAD - w.shape[1]), w.dtype)
    return jnp.concatenate([w[:, :split], z, w[:, split:]], axis=1)


def _forward(x_prompt, x_sample, cache_k, cache_v, state_rwkv, c, c_ctx, ln_in_g, ln_in_b, w_ada, b_ada,
             w_in, w_out, att_sink, rw_w0, rw_w_up, rw_a0, rw_a_up, rw_g_up, rw_kk, rw_ka, rw_rk,
             rw_lnx_g, rw_lnx_b, conv_w, ln1_g, ln1_b, ln2_g, ln2_b, peer_wq, peer_subkeys, peer_u, peer_v,
             tb_in=512, tb_out=256, tb_peer=512, rw_par=2):
    nb_c, seq_c, d = x_prompt.shape
    nb_l, seq_l, _ = x_sample.shape
    depth = w_in.shape[0]
    n_ctx = nb_c * seq_c
    past = cache_k.shape[2]
    W = RWKV_DIM

    x = jnp.concatenate([x_prompt.reshape(n_ctx, d), x_sample.reshape(nb_l * seq_l, d)], axis=0)
    x = _ln_call(x, ln_in_g, ln_in_b, tb_in)

    n_groups = 1 + nb_l
    g_pad = -(-n_groups // SUBLANES) * SUBLANES
    cvec = jnp.concatenate([c_ctx[None], c, jnp.zeros((g_pad - n_groups, d), jnp.float32)], axis=0)
    mod_all = _mod_call(cvec, w_ada, b_ada).reshape(depth, g_pad, 6, d)

    cos_t, sin_t = _rope_tables(seq_l, tb_in)
    consts = tuple(jnp.asarray(a) for a in _rwkv_consts())
    ck_all = cache_k.reshape(nb_l, depth, past, KV_DIM)
    cv_all = cache_v.reshape(nb_l, depth, past, KV_DIM)
    eye_h = jnp.eye(RWKV_HEADS, dtype=jnp.float32)
    s0_bd_all = jnp.einsum('bldhij,hg->bldhigj', state_rwkv, eye_h).reshape(nb_l, depth, 2, W, W)
    s0_zero = jnp.zeros((nb_c, 2, W, W), jnp.float32)

    ks, vs, sts = [], [], []
    for l in range(depth):
        mod = mod_all[l]
        p_att, p_rw, p_cv = _in_proj_call(x, mod, _bf(_pad_w_in(w_in[l])), cos_t, sin_t, tb_in, n_ctx, seq_l)
        att_c = _ctx_attn_call(p_att, att_sink[l], nb_c, seq_c)
        att_l = _lat_attn_call(p_att, ck_all, cv_all, att_sink[l], l, nb_l, seq_l, n_ctx)
        wts = _rwkv_weights(rw_w0[l], rw_w_up[l], rw_a0[l], rw_a_up[l], rw_g_up[l], rw_kk[l], rw_ka[l],
                            rw_rk[l], rw_lnx_g[l], rw_lnx_b[l])
        rw_c, sfin = _rwkv_call(p_rw, s0_zero, consts, wts, nb_c, seq_c, 0, rw_par, False)
        rw_l, _ = _rwkv_call(p_rw, s0_bd_all[:, l], consts, wts, nb_l, seq_l, n_ctx, rw_par, True)
        x1 = _out_proj_call(att_c, att_l, rw_c, rw_l, p_cv, x, mod, _bf(w_out[l]), conv_w[l],
                            ln1_g[l], ln1_b[l], tb_out, n_ctx, seq_c, seq_l)
        sk = peer_subkeys[l].reshape(2 * PEER_HEADS, PEER_NKEYS, PEER_HALF)
        vt = _bf(peer_v[l]).reshape(-1, SUBLANES * PEER_NKEYS, d).transpose(0, 2, 1)
        x = _peer_call(x1, mod, _bf(peer_wq[l].T), sk, _bf(peer_u[l]), vt,
                       ln2_g[l], ln2_b[l], tb_peer, n_ctx, seq_l)
        ks.append(p_att[:n_ctx, ATT_DIM:ATT_DIM + KV_DIM].reshape(nb_c, seq_c, ATT_KV_HEADS, HEAD_DIM))
        vs.append(p_att[:n_ctx, ATT_DIM + KV_DIM:].reshape(nb_c, seq_c, ATT_KV_HEADS, HEAD_DIM))
        s5 = sfin.reshape(nb_c, 2, RWKV_HEADS, HEAD_DIM, RWKV_HEADS, HEAD_DIM)
        sts.append(jnp.stack([s5[:, :, h, :, h, :] for h in range(RWKV_HEADS)], axis=2))

    y_prompt = x[:n_ctx].reshape(nb_c, seq_c, d)
    y_sample = x[n_ctx:].reshape(nb_l, seq_l, d)
    return (y_prompt, y_sample, jnp.stack(ks, axis=1), jnp.stack(vs, axis=1), jnp.stack(sts, axis=1))


def kernel(x_prompt, x_sample, cache_k, cache_v, state_rwkv, c, c_ctx, ln_in_g, ln_in_b, w_ada, b_ada, w_in, w_out, att_sink, rw_w0, rw_w_up, rw_a0, rw_a_up, rw_g_up, rw_kk, rw_ka, rw_rk, rw_lnx_g, rw_lnx_b, conv_w, ln1_g, ln1_b, ln2_g, ln2_b, peer_wq, peer_subkeys, peer_u, peer_v):
    return _forward(x_prompt, x_sample, cache_k, cache_v, state_rwkv, c, c_ctx, ln_in_g, ln_in_b, w_ada, b_ada,
                    w_in, w_out, att_sink, rw_w0, rw_w_up, rw_a0, rw_a_up, rw_g_up, rw_kk, rw_ka, rw_rk,
                    rw_lnx_g, rw_lnx_b, conv_w, ln1_g, ln1_b, ln2_g, ln2_b, peer_wq, peer_subkeys, peer_u, peer_v)
```

```python
import functools
import math

import jax
import jax.numpy as jnp
import numpy as np
from jax import lax
from jax.experimental import pallas as pl
from jax.experimental.pallas import tpu as pltpu

D_MODEL = 1024
DEPTH = 4
GRID_W = 64
HEAD_DIM = 64
ATT_DIM = 512
RWKV_DIM = 256
CONV_DIM = 256
ATT_HEADS = 8
ATT_KV_HEADS = 2
ATT_GROUP = 4
KV_DIM = 128
WINDOW = 128
ATT_BLOCK = 128
ROPE_THETA = 10000.0
RWKV_HEADS = 4
W_RANK = 32
A_RANK = 32
G_RANK = 64
RWKV_LNX_EPS = 64e-5
PEER_HEADS = 8
PEER_NKEYS = 128
PEER_TOPK = 16
PEER_HALF = 128
LN_EPS = 1e-5
ALPHA = (2 * DEPTH) ** 0.25
NEG = -1e30

LANES = 128
SUBLANES = 8
VMEM_LIMIT = 56 * 1024 * 1024

QK_W = ATT_DIM + KV_DIM
ATT_W = ATT_DIM + 2 * KV_DIM
RW_W = 1024
CV_W = 3 * CONV_DIM
IN_PAD = ATT_W + RW_W + CV_W
RW_LOW = 3 * RWKV_DIM

RW_CHUNK = 64
PEER_SCORE_PASSES = 3
HIGHEST = lax.Precision.HIGHEST
NN = (((1,), (0,)), ((), ()))
NT = (((1,), (1,)), ((), ()))
TN = (((0,), (0,)), ((), ()))


def _mm(a, b, dims=NN, precision=None):
    return lax.dot_general(a, b, dims, precision=precision, preferred_element_type=jnp.float32)


def _bf(x):
    return x.astype(jnp.bfloat16)


def _split(x):
    hi = _bf(x)
    return hi, _bf(x - hi.astype(jnp.float32))


def _mm_split(a, b, dims, passes):
    out = _mm(a[0], b[0], dims)
    if passes == 3:
        out = out + (_mm(a[0], b[1], dims) + _mm(a[1], b[0], dims))
    return out


def _mm_terms(x, exact, terms, x_is_rhs=False):
    out = None
    rest = x
    for _ in range(terms):
        piece = _bf(rest)
        rest = rest - piece.astype(jnp.float32)
        part = _mm(exact, piece) if x_is_rhs else _mm(piece, exact)
        out = part if out is None else out + part
    return out


RW_PASSES = {"gram_n": 1, "gram": 1, "inv": 1, "loc": 1, "seq": 1}


def _layer_norm(x, g, b, eps):
    mu = jnp.mean(x, axis=-1, keepdims=True)
    xc = x - mu
    var = jnp.mean(xc * xc, axis=-1, keepdims=True)
    return xc * lax.rsqrt(var + eps) * g + b


def _params(*sem):
    return pltpu.CompilerParams(dimension_semantics=sem, vmem_limit_bytes=VMEM_LIMIT)


def _full(shape):
    nd = len(shape)
    return pl.BlockSpec(shape, lambda *_: (0,) * nd)


def _mod_kernel(c_ref, w_ref, b_ref, o_ref):
    cv = c_ref[...]
    s = cv * jax.nn.sigmoid(cv)
    o_ref[...] = _mm(s, w_ref[...], precision=HIGHEST) + b_ref[...]


def _mod_call(cvec, w_ada, b_ada):
    depth, d, n6 = w_ada.shape
    rows = cvec.shape[0]
    nt = n6 // d
    return pl.pallas_call(
        _mod_kernel,
        grid=(depth, nt),
        in_specs=[
            pl.BlockSpec((rows, d), lambda l, n: (0, 0)),
            pl.BlockSpec((None, d, d), lambda l, n: (l, 0, n)),
            pl.BlockSpec((None, 1, d), lambda l, n: (l, 0, n)),
        ],
        out_specs=pl.BlockSpec((None, rows, d), lambda l, n: (l, 0, n)),
        out_shape=jax.ShapeDtypeStruct((depth, rows, n6), jnp.float32),
        compiler_params=_params("parallel", "parallel"),
        name="adaln_mod",
    )(cvec, w_ada, b_ada.reshape(depth, 1, n6))


def _ln_kernel(x_ref, g_ref, b_ref, o_ref):
    o_ref[...] = _layer_norm(x_ref[...], g_ref[...], b_ref[...], LN_EPS)


def _ln_call(x, g, b, tb):
    rows, d = x.shape
    return pl.pallas_call(
        _ln_kernel,
        grid=(rows // tb,),
        in_specs=[pl.BlockSpec((tb, d), lambda i: (i, 0)), _full((1, d)), _full((1, d))],
        out_specs=pl.BlockSpec((tb, d), lambda i: (i, 0)),
        out_shape=jax.ShapeDtypeStruct((rows, d), jnp.float32),
        compiler_params=_params("parallel"),
        name="ln_in",
    )(x, g.reshape(1, d), b.reshape(1, d))


def _group_of_block(i, n_ctx_blocks, blocks_per_lat_seq):
    return jnp.where(i < n_ctx_blocks, 0, 1 + (i - n_ctx_blocks) // blocks_per_lat_seq)


def _in_proj_kernel(x_ref, mod_ref, w_ref, cos_ref, sin_ref, att_ref, rw_ref, cv_ref):
    x = x_ref[...]
    h = x * (1.0 + mod_ref[1:2, :]) + mod_ref[0:1, :]
    p = _mm(_bf(h), w_ref[...])
    qk = p[:, :QK_W]
    lane = lax.broadcasted_iota(jnp.int32, qk.shape, 1)
    partner = jnp.where((lane % 32) < 16,
                        pltpu.roll(qk, QK_W - 16, 1),
                        pltpu.roll(qk, 16, 1))
    att_ref[:, :QK_W] = qk * cos_ref[...] + partner * sin_ref[...]
    att_ref[:, QK_W:] = p[:, QK_W:ATT_W]
    rw_ref[...] = p[:, ATT_W:ATT_W + RW_W]
    cv_ref[...] = p[:, ATT_W + RW_W:]


def _in_proj_call(x, mod, w_in_p, cos_t, sin_t, tb, n_ctx_rows, lat_seq):
    rows, d = x.shape
    ncb = n_ctx_rows // tb
    bps = lat_seq // tb

    def mod_map(i):
        return (_group_of_block(i, ncb, bps), 0, 0)

    def rope_map(i):
        return (jnp.where(i < ncb, bps, (i - ncb) % bps), 0)

    return pl.pallas_call(
        _in_proj_kernel,
        grid=(rows // tb,),
        in_specs=[
            pl.BlockSpec((tb, d), lambda i: (i, 0)),
            pl.BlockSpec((None, 6, d), mod_map),
            _full((d, IN_PAD)),
            pl.BlockSpec((tb, QK_W), rope_map),
            pl.BlockSpec((tb, QK_W), rope_map),
        ],
        out_specs=[
            pl.BlockSpec((tb, ATT_W), lambda i: (i, 0)),
            pl.BlockSpec((tb, RW_W), lambda i: (i, 0)),
            pl.BlockSpec((tb, CV_W), lambda i: (i, 0)),
        ],
        out_shape=[
            jax.ShapeDtypeStruct((rows, ATT_W), jnp.float32),
            jax.ShapeDtypeStruct((rows, RW_W), jnp.float32),
            jax.ShapeDtypeStruct((rows, CV_W), jnp.float32),
        ],
        compiler_params=_params("parallel"),
        name="in_proj",
    )(x, mod, w_in_p, cos_t, sin_t)


def _ctx_attn_kernel(sink_ref, p_ref, o_ref):
    scale = HEAD_DIM ** -0.5
    for kv in range(ATT_KV_HEADS):
        k = _bf(p_ref[:, ATT_DIM + kv * HEAD_DIM:ATT_DIM + (kv + 1) * HEAD_DIM])
        v = _bf(p_ref[:, ATT_DIM + KV_DIM + kv * HEAD_DIM:ATT_DIM + KV_DIM + (kv + 1) * HEAD_DIM])
        for g in range(ATT_GROUP):
            hd = kv * ATT_GROUP + g
            q = _bf(p_ref[:, hd * HEAD_DIM:(hd + 1) * HEAD_DIM])
            s = _mm(q, k, NT) * scale
            sink = sink_ref[hd]
            m = jnp.maximum(jnp.max(s, axis=-1, keepdims=True), sink)
            e = jnp.exp(s - m)
            den = jnp.sum(e, axis=-1, keepdims=True) + jnp.exp(sink - m)
            o_ref[:, hd * HEAD_DIM:(hd + 1) * HEAD_DIM] = _mm(_bf(e), v) / den


def _ctx_attn_call(p_att, sink, n_seq, seq):
    return pl.pallas_call(
        _ctx_attn_kernel,
        grid=(n_seq,),
        in_specs=[
            pl.BlockSpec(memory_space=pltpu.SMEM),
            pl.BlockSpec((seq, ATT_W), lambda b: (b, 0)),
        ],
        out_specs=pl.BlockSpec((seq, ATT_DIM), lambda b: (b, 0)),
        out_shape=jax.ShapeDtypeStruct((n_seq * seq, ATT_DIM), jnp.float32),
        compiler_params=_params("parallel"),
        name="ctx_attn",
    )(sink, p_att)


def _lat_attn_kernel(sink_ref, own_ref, prev_ref, next_ref, ck_ref, cv_ref, o_ref, *, n_blocks):
    n = pl.program_id(1)
    scale = HEAD_DIM ** -0.5
    qi = lax.broadcasted_iota(jnp.int32, (ATT_BLOCK, 3 * ATT_BLOCK), 0)
    kj = lax.broadcasted_iota(jnp.int32, (ATT_BLOCK, 3 * ATT_BLOCK), 1)
    rel = kj - ATT_BLOCK - qi
    kpos = (n - 1) * ATT_BLOCK + kj
    mask = (jnp.abs(rel) <= WINDOW) & (kpos >= 0) & (kpos < n_blocks * ATT_BLOCK)
    mask4 = jnp.concatenate([mask] * ATT_GROUP, axis=0)
    row_group = lax.broadcasted_iota(jnp.int32, (ATT_GROUP * ATT_BLOCK, 1), 0) // ATT_BLOCK
    for kv in range(ATT_KV_HEADS):
        ks = slice(ATT_DIM + kv * HEAD_DIM, ATT_DIM + (kv + 1) * HEAD_DIM)
        vs = slice(ATT_DIM + KV_DIM + kv * HEAD_DIM, ATT_DIM + KV_DIM + (kv + 1) * HEAD_DIM)
        kw = _bf(jnp.concatenate([prev_ref[:, ks], own_ref[:, ks], next_ref[:, ks]], axis=0))
        vw = _bf(jnp.concatenate([prev_ref[:, vs], own_ref[:, vs], next_ref[:, vs]], axis=0))
        ck = _bf(ck_ref[:, kv * HEAD_DIM:(kv + 1) * HEAD_DIM])
        cv = _bf(cv_ref[:, kv * HEAD_DIM:(kv + 1) * HEAD_DIM])
        q4 = _bf(jnp.concatenate(
            [own_ref[:, (kv * ATT_GROUP + g) * HEAD_DIM:(kv * ATT_GROUP + g + 1) * HEAD_DIM]
             for g in range(ATT_GROUP)], axis=0))
        s_ctx = _mm(q4, ck, NT) * scale
        s_win = jnp.where(mask4, _mm(q4, kw, NT) * scale, NEG)
        sink = jnp.full((ATT_GROUP * ATT_BLOCK, 1), sink_ref[kv * ATT_GROUP], jnp.float32)
        for g in range(1, ATT_GROUP):
            sink = jnp.where(row_group == g, sink_ref[kv * ATT_GROUP + g], sink)
        m = jnp.maximum(jnp.maximum(jnp.max(s_ctx, axis=-1, keepdims=True),
                                    jnp.max(s_win, axis=-1, keepdims=True)), sink)
        e_ctx = jnp.exp(s_ctx - m)
        e_win = jnp.exp(s_win - m)
        den = (jnp.sum(e_ctx, axis=-1, keepdims=True) + jnp.sum(e_win, axis=-1, keepdims=True)
               + jnp.exp(sink - m))
        o4 = (_mm(_bf(e_ctx), cv) + _mm(_bf(e_win), vw)) / den
        for g in range(ATT_GROUP):
            hd = kv * ATT_GROUP + g
            o_ref[:, hd * HEAD_DIM:(hd + 1) * HEAD_DIM] = o4[g * ATT_BLOCK:(g + 1) * ATT_BLOCK]


def _lat_attn_call(p_att, ck, cv, sink, layer, n_seq, seq, row_off):
    nb = seq // ATT_BLOCK
    off = row_off // ATT_BLOCK
    past = ck.shape[2]

    def own(b, n):
        return (off + b * nb + n, 0)

    def prev(b, n):
        return (off + b * nb + jnp.maximum(n - 1, 0), 0)

    def nxt(b, n):
        return (off + b * nb + jnp.minimum(n + 1, nb - 1), 0)

    return pl.pallas_call(
        functools.partial(_lat_attn_kernel, n_blocks=nb),
        grid=(n_seq, nb),
        in_specs=[
            pl.BlockSpec(memory_space=pltpu.SMEM),
            pl.BlockSpec((ATT_BLOCK, ATT_W), own),
            pl.BlockSpec((ATT_BLOCK, ATT_W), prev),
            pl.BlockSpec((ATT_BLOCK, ATT_W), nxt),
            pl.BlockSpec((None, None, past, KV_DIM), lambda b, n: (b, layer, 0, 0)),
            pl.BlockSpec((None, None, past, KV_DIM), lambda b, n: (b, layer, 0, 0)),
        ],
        out_specs=pl.BlockSpec((ATT_BLOCK, ATT_DIM), lambda b, n: (b * nb + n, 0)),
        out_shape=jax.ShapeDtypeStruct((n_seq * seq, ATT_DIM), jnp.float32),
        compiler_params=_params("parallel", "parallel"),
        name="lat_attn",
    )(sink, p_att, p_att, p_att, ck, cv)


def _rwkv_consts():
    L, H, W = RW_CHUNK, RWKV_HEADS, RWKV_DIM
    t = np.arange(L)
    inc = np.stack([(t[None, :] <= t[:, None]), (t[None, :] >= t[:, None])]).astype(np.float32)
    stc = np.stack([(t[None, :] < t[:, None]), (t[None, :] > t[:, None])]).astype(np.float32)
    eye_h = np.eye(H, dtype=np.float32)
    inc_bd = np.stack([np.kron(eye_h, inc[d]) for d in range(2)])
    stc_bd = np.stack([np.kron(eye_h, stc[d]) for d in range(2)])
    blk = np.kron(eye_h, np.ones((L, HEAD_DIM), np.float32))
    eye = np.eye(W, dtype=np.float32)
    return inc, inc_bd, stc_bd, blk, eye


def _rwkv_kernel(p_ref, s0_ref, inc_ref, incbd_ref, stcbd_ref, blk_ref, eye_ref,
                 w0_ref, wup_ref, a0_ref, aup_ref, gup_ref, kkw_ref, kaw_ref, rkw_ref, lng_ref, lnb_ref,
                 o_ref, sfin_ref, y_scr, bon_scr, *, seq, n_par):
    L = RW_CHUNK
    nc = seq // L
    blk = _bf(blk_ref[...])
    ones_bd = blk
    eye = eye_ref[...]
    y_scr[...] = jnp.zeros_like(y_scr)
    bon_scr[...] = jnp.zeros_like(bon_scr)

    def stack(x):
        return tuple(jnp.concatenate([z] * RWKV_HEADS, axis=0) * blk for z in _split(x))

    def chunk(idx, r0, s_in, s_out):
        d = idx % 2
        x = p_ref[pl.ds(r0, L), :]
        rr = x[:, 0:RWKV_DIM]
        rk = x[:, RWKV_DIM:2 * RWKV_DIM]
        rv = x[:, 2 * RWKV_DIM:3 * RWKV_DIM]
        low = x[:, RW_LOW:RW_LOW + LANES]
        zw = w0_ref[d] + _mm_split(_split(jnp.tanh(low)), _split(wup_ref[d]), NN, 3)
        yield
        u = -zw
        softplus = jnp.maximum(u, 0.0) + jnp.log1p(jnp.exp(-jnp.abs(u)))
        lw = -jnp.exp(-softplus - 0.5)
        asig = jax.nn.sigmoid(a0_ref[d] + _mm_split(_split(low), _split(aup_ref[d]), NN, 3))
        yield
        kd = rk * (1.0 + (asig - 1.0) * kaw_ref[...])
        kkr = rk * kkw_ref[...]
        kk = kkr * lax.rsqrt(jnp.maximum(_mm_terms(kkr * kkr, ones_bd, 2), 1e-24))
        yield
        a_s = -kk
        b_s = kk * asig
        bon_scr[pl.ds(r0, L), :] += _mm_terms(rr * kd * rkw_ref[...], ones_bd, 2) * rv
        yield

        cl = _mm_terms(lw, _bf(inc_ref[d]), 3, x_is_rhs=True)
        yield
        cl_end = cl[L - 1:L, :] if d == 0 else cl[0:1, :]
        e_neg = jnp.exp(-cl)
        e_tail = jnp.exp(cl_end - cl)
        a_st = stack(a_s * jnp.exp(cl - lw))
        b_st = stack(b_s * e_neg)
        k_st = stack(kd * e_neg)
        r_st = stack(rr * jnp.exp(cl))
        v_st = stack(rv)
        bh_st = stack(b_s * e_tail)
        kh_st = stack(kd * e_tail)
        stc = stcbd_ref[d]
        inc = incbd_ref[d]
        n_m = stc * _mm_split(a_st, b_st, NT, RW_PASSES["gram_n"])
        yield
        m1 = stc * _mm_split(a_st, k_st, NT, RW_PASSES["gram"])
        yield
        fill = {}
        fillers = [
            lambda: fill.update(p=inc * _mm_split(r_st, b_st, NT, RW_PASSES["gram"])),
            lambda: fill.update(m1v=_mm_split(_split(m1), v_st, NN, RW_PASSES["loc"])),
            lambda: fill.update(q=inc * _mm_split(r_st, k_st, NT, RW_PASSES["gram"])),
            lambda: fill.update(vk=_mm_split(v_st, kh_st, TN, RW_PASSES["loc"])),
            lambda: fill.update(rs=_mm_split(r_st, _split(s_in), NT, RW_PASSES["seq"])),
        ]
        t_m = eye + n_m
        n_p = _split(n_m)
        for it in range(5):
            n_sq = _mm_split(n_p, n_p, NN, RW_PASSES["inv"])
            yield
            fillers[it]()
            yield
            n_p = _split(n_sq)
            t_m = t_m + _mm_split(_split(t_m), n_p, NN, RW_PASSES["inv"])
            yield
        t_s = _split(t_m)
        qv = _mm_split(_split(fill["q"]), v_st, NN, RW_PASSES["loc"])
        yield
        u_loc = _mm_split(t_s, _split(fill["m1v"]), NN, RW_PASSES["loc"])
        yield
        w_t = _mm_split(t_s, a_st, NN, RW_PASSES["loc"])
        yield
        u_m = _mm_split(_split(w_t), _split(s_in), NT, RW_PASSES["seq"]) + u_loc
        yield
        u_s = _split(u_m)
        y_bd = fill["rs"] + _mm_split(_split(fill["p"]), u_s, NN, RW_PASSES["seq"]) + qv
        y = y_bd[0:L] + y_bd[L:2 * L] + y_bd[2 * L:3 * L] + y_bd[3 * L:4 * L]
        y_scr[pl.ds(r0, L), :] += y
        yield
        s_out[idx] = s_in * jnp.exp(cl_end) + _mm_split(u_s, bh_st, TN, RW_PASSES["seq"]) + fill["vk"]

    def body(c, carry):
        s_out = {}
        stages = []
        for sq in range(n_par):
            stages.append(chunk(2 * sq, pl.multiple_of(sq * seq + c * L, L), carry[2 * sq], s_out))
            stages.append(chunk(2 * sq + 1, pl.multiple_of(sq * seq + (nc - 1 - c) * L, L), carry[2 * sq + 1],
                                s_out))
        while stages:
            for gen in list(stages):
                if next(gen, StopIteration) is StopIteration:
                    stages.remove(gen)
        return tuple(s_out[i] for i in range(2 * n_par))

    s_fin = lax.fori_loop(0, nc, body, tuple(s0_ref[i // 2, i % 2] for i in range(2 * n_par)))
    for i in range(2 * n_par):
        sfin_ref[i // 2, i % 2] = s_fin[i]

    inv_n = 1.0 / HEAD_DIM

    def epilogue(c, carry):
        r0 = pl.multiple_of(c * L, L)
        y = y_scr[pl.ds(r0, L), :]
        mu = _mm_terms(y, ones_bd, 3) * inv_n
        yc = y - mu
        var = _mm_terms(yc * yc, ones_bd, 2) * inv_n
        yn = yc * lax.rsqrt(var + RWKV_LNX_EPS) * lng_ref[...] + lnb_ref[...]
        gd = p_ref[pl.ds(r0, L), RW_LOW + LANES:RW_LOW + 2 * LANES]
        g = _mm_split(_split(jax.nn.sigmoid(gd)), _split(gup_ref[...]), NN, 3)
        bon = bon_scr[pl.ds(r0, L), :]
        o_ref[pl.ds(r0, L), :] = (yn + bon) * g
        return carry

    lax.fori_loop(0, n_par * nc, epilogue, 0)


def _rwkv_call(p_rw, s0_bd, consts, wts, n_seq, seq, row_off, n_par, single_buffer_input):
    inc, inc_bd, stc_bd, blk, eye = consts
    W = RWKV_DIM
    rows = n_par * seq
    off = row_off // rows
    p_mode = dict(pipeline_mode=pl.Buffered(1)) if single_buffer_input else {}
    in_specs = [
        pl.BlockSpec((rows, RW_W), lambda b: (off + b, 0), **p_mode),
        pl.BlockSpec((n_par, 2, W, W), lambda b: (b, 0, 0, 0)),
        _full(inc.shape), _full(inc_bd.shape), _full(stc_bd.shape), _full(blk.shape), _full(eye.shape),
    ] + [_full(w.shape) for w in wts]
    return pl.pallas_call(
        functools.partial(_rwkv_kernel, seq=seq, n_par=n_par),
        grid=(n_seq // n_par,),
        in_specs=in_specs,
        out_specs=[
            pl.BlockSpec((rows, W), lambda b: (b, 0)),
            pl.BlockSpec((n_par, 2, W, W), lambda b: (b, 0, 0, 0)),
        ],
        out_shape=[
            jax.ShapeDtypeStruct((n_seq * seq, W), jnp.float32),
            jax.ShapeDtypeStruct((n_seq, 2, W, W), jnp.float32),
        ],
        scratch_shapes=[pltpu.VMEM((rows, W), jnp.float32), pltpu.VMEM((rows, W), jnp.float32)],
        compiler_params=_params("parallel"),
        name="rwkv",
    )(p_rw, s0_bd, inc, inc_bd, stc_bd, blk, eye, *wts)


def _rwkv_weights(w0, w_up, a0, a_up, g_up, kk, ka, rk, lng, lnb):
    W = RWKV_DIM
    wup_p = jnp.zeros((2, LANES, W), jnp.float32)
    aup_p = jnp.zeros((2, LANES, W), jnp.float32)
    for d in range(2):
        wup_p = wup_p.at[d, d * W_RANK:(d + 1) * W_RANK].set(w_up[d])
        aup_p = aup_p.at[d, 2 * W_RANK + d * A_RANK:2 * W_RANK + (d + 1) * A_RANK].set(a_up[d])
    gup_p = jnp.zeros((LANES, W), jnp.float32).at[:G_RANK].set(g_up)
    row = lambda v: v.reshape(1, W)
    return (w0.reshape(2, 1, W), wup_p, a0.reshape(2, 1, W), aup_p, gup_p,
            row(kk), row(ka), row(rk), row(lng), row(lnb))


def _out_proj_kernel(attc_ref, attl_ref, rwc_ref, rwl_ref, cv_ref, cvp_ref, cvn_ref, x_ref, mod_ref,
                     w_ref, cw_ref, g_ref, b_ref, o_ref, *, tb, ncb, bps_c, bps):
    i = pl.program_id(0)
    is_ctx = i < ncb
    pos = jnp.where(is_ctx, i % bps_c, (i - ncb) % bps)
    last = jnp.where(is_ctx, bps_c - 1, bps - 1)
    att = jnp.where(is_ctx, attc_ref[...], attl_ref[...])
    rw = jnp.where(is_ctx, rwc_ref[...], rwl_ref[...])

    cb = cv_ref[:, 0:CONV_DIM]
    z = cv_ref[:, CONV_DIM:2 * CONV_DIM] * cv_ref[:, 2 * CONV_DIM:3 * CONV_DIM]
    halo_p = cvp_ref[SUBLANES - 1:SUBLANES, CONV_DIM:2 * CONV_DIM] * cvp_ref[SUBLANES - 1:SUBLANES, 2 * CONV_DIM:]
    halo_n = cvn_ref[0:1, CONV_DIM:2 * CONV_DIM] * cvn_ref[0:1, 2 * CONV_DIM:]
    halo_p = jnp.where(pos > 0, halo_p, 0.0)
    halo_n = jnp.where(pos < last, halo_n, 0.0)
    row = lax.broadcasted_iota(jnp.int32, z.shape, 0)
    z_prev = jnp.where(row == 0, halo_p, pltpu.roll(z, 1, 0))
    z_next = jnp.where(row == tb - 1, halo_n, pltpu.roll(z, tb - 1, 0))
    conv = cb * (cw_ref[0:1, :] * z_prev + cw_ref[1:2, :] * z + cw_ref[2:3, :] * z_next)

    mix = (_mm(_bf(att), w_ref[0:ATT_DIM, :])
           + _mm(_bf(rw), w_ref[ATT_DIM:ATT_DIM + RWKV_DIM, :])
           + _mm(_bf(conv), w_ref[ATT_DIM + RWKV_DIM:, :]))
    o_ref[...] = _layer_norm(ALPHA * x_ref[...] + mod_ref[2:3, :] * mix, g_ref[...], b_ref[...], LN_EPS)


def _out_proj_call(att_c, att_l, rw_c, rw_l, p_cv, x, mod, w_out_b, conv_w, ln_g, ln_b, tb, n_ctx_rows, ctx_seq,
                   lat_seq):
    rows, d = x.shape
    ncb = n_ctx_rows // tb
    bps = lat_seq // tb
    sub = tb // SUBLANES
    n_sub = rows // SUBLANES

    def ctx_map(i):
        return (jnp.minimum(i, ncb - 1), 0)

    def lat_map(i):
        return (jnp.maximum(i - ncb, 0), 0)

    return pl.pallas_call(
        functools.partial(_out_proj_kernel, tb=tb, ncb=ncb, bps_c=ctx_seq // tb, bps=bps),
        grid=(rows // tb,),
        in_specs=[
            pl.BlockSpec((tb, ATT_DIM), ctx_map),
            pl.BlockSpec((tb, ATT_DIM), lat_map),
            pl.BlockSpec((tb, RWKV_DIM), ctx_map),
            pl.BlockSpec((tb, RWKV_DIM), lat_map),
            pl.BlockSpec((tb, CV_W), lambda i: (i, 0)),
            pl.BlockSpec((SUBLANES, CV_W), lambda i: (jnp.maximum(i * sub - 1, 0), 0)),
            pl.BlockSpec((SUBLANES, CV_W), lambda i: (jnp.minimum((i + 1) * sub, n_sub - 1), 0)),
            pl.BlockSpec((tb, d), lambda i: (i, 0)),
            pl.BlockSpec((None, 6, d), lambda i: (_group_of_block(i, ncb, bps), 0, 0)),
            _full((d, d)),
            _full((3, CONV_DIM)),
            _full((1, d)),
            _full((1, d)),
        ],
        out_specs=pl.BlockSpec((tb, d), lambda i: (i, 0)),
        out_shape=jax.ShapeDtypeStruct((rows, d), jnp.float32),
        compiler_params=_params("parallel"),
        name="out_proj",
    )(att_c, att_l, rw_c, rw_l, p_cv, p_cv, p_cv, x, mod, w_out_b, conv_w, ln_g.reshape(1, d), ln_b.reshape(1, d))


def _gelu_tanh(x):
    return 0.5 * x * (1.0 + jnp.tanh(math.sqrt(2.0 / math.pi) * (x + 0.044715 * (x * x * x))))


def _extract_top(work, iota, n_out, want_rank=False, by_value=False):
    vals = []
    big = work.shape[0]
    rank = jnp.full(work.shape, float(big), jnp.float32) if want_rank else None
    for k in range(n_out):
        m = jnp.max(work, axis=0, keepdims=True)
        if by_value:
            sel = work == m
        else:
            idx = jnp.min(jnp.where(work == m, iota, big), axis=0, keepdims=True)
            sel = iota == idx
        work = jnp.where(sel, -jnp.inf, work)
        if want_rank:
            rank = jnp.where(sel, float(k), rank)
        vals.append(m)
    return vals, work, rank


def _peer_kernel(x_ref, mod_ref, wq_ref, sk_ref, u_ref, vt_ref, g_ref, b_ref, o_ref,
                 ht_scr, s0_scr, e0_scr, s1_scr, r1_scr, e1_scr, a_scr, z_scr, acc_scr, *, tb):
    e_step = pl.program_id(1)
    n_lt = tb // LANES
    K = PEER_TOPK

    @pl.when(e_step == 0)
    def _route():
        h2 = x_ref[...] * (1.0 + mod_ref[4:5, :]) + mod_ref[3:4, :]
        ht = _bf(h2.T)
        ht_scr[...] = ht
        qt = _mm(wq_ref[...], ht)
        for hp in range(2 * PEER_HEADS):
            sc = _mm_split(_split(sk_ref[hp]), _split(qt[hp * PEER_HALF:(hp + 1) * PEER_HALF, :]), NN,
                           PEER_SCORE_PASSES)
            if hp % 2 == 0:
                s0_scr[hp // 2] = sc
            else:
                s1_scr[hp // 2] = sc

        iota_k = lax.broadcasted_iota(jnp.int32, (PEER_NKEYS, LANES), 0)
        iota_c = lax.broadcasted_iota(jnp.int32, ((SUBLANES + 2) * SUBLANES, LANES), 0)

        def per_tile(j, carry):
            hd = j // n_lt
            lanes = pl.ds(pl.multiple_of((j % n_lt) * LANES, LANES), LANES)
            sc0 = s0_scr[hd, :, lanes]
            sc1 = s1_scr[hd, :, lanes]

            def route_tile(by_value):
                v0, w0, _ = _extract_top(sc0, iota_k, K, by_value=by_value)
                v1, w1, rank1 = _extract_top(sc1, iota_k, K, want_rank=True, by_value=by_value)
                cand0 = w0 != sc0
                cand1 = w1 != sc1
                lo1 = jnp.concatenate(v1[:SUBLANES], axis=0)
                hi1 = jnp.concatenate(v1[SUBLANES:], axis=0)
                hi0 = jnp.concatenate(v0[SUBLANES:], axis=0)
                pair = jnp.concatenate([v0[a] + lo1 for a in range(SUBLANES)] + [v0[0] + hi1, hi0 + v1[0]],
                                       axis=0)
                f, wp, _ = _extract_top(pair, iota_c, K + 1, by_value=by_value)
                zsum = jnp.zeros_like(f[0])
                for k in range(K):
                    zsum = zsum + jnp.exp(f[k] - f[0])
                th = 0.5 * (f[K - 1] + f[K]) - sc0
                count0 = jnp.zeros_like(sc0)
                for b in range(K):
                    count0 = count0 + jnp.where(v1[b] > th, 1.0, 0.0)
                s0_scr[hd, :, lanes] = jnp.where(cand0, count0, 0.0)
                e0_scr[hd, :, lanes] = jnp.where(cand0, jnp.exp(sc0 - v0[0]), 0.0) / zsum
                e1_scr[hd, :, lanes] = _bf(jnp.where(cand1, jnp.exp(sc1 - v1[0]), 0.0))
                r1_scr[hd, :, lanes] = _bf(rank1)
                if not by_value:
                    return None
                removed = [jnp.sum(jnp.where(c, 1.0, 0.0), axis=0, keepdims=True)
                           for c in (cand0, cand1, wp != pair)]
                wrong = (removed[0] != K) | (removed[1] != K) | (removed[2] != K + 1)
                return jnp.max(jnp.where(wrong, 1.0, 0.0)) > 0.0

            had_ties = route_tile(by_value=True)

            @pl.when(had_ties)
            def _exact():
                route_tile(by_value=False)

            return carry

        lax.fori_loop(0, PEER_HEADS * n_lt, per_tile, 0)
        acc_scr[...] = jnp.zeros_like(acc_scr)

    a_scr[...] = _gelu_tanh(_mm(u_ref[...], ht_scr[...]))

    keys = pl.ds(pl.multiple_of(e_step * SUBLANES, SUBLANES), SUBLANES)
    half = PEER_NKEYS // 2
    group = SUBLANES // 2
    packed = 2 * SUBLANES
    tile3 = (half // packed, packed, LANES)
    for ln in range(n_lt):
        lanes = slice(ln * LANES, (ln + 1) * LANES)
        c0 = [s0_scr[hd, keys, lanes] for hd in range(PEER_HEADS)]
        e0 = [e0_scr[hd, keys, lanes] for hd in range(PEER_HEADS)]
        for jh in range(2):
            jrows = slice(jh * half, (jh + 1) * half)
            for ig in range(SUBLANES // group):
                g = [jnp.zeros(tile3, jnp.bfloat16) for _ in range(group)]
                for hd in range(PEER_HEADS):
                    r1 = r1_scr[hd, jrows, lanes].reshape(tile3)
                    e1 = e1_scr[hd, jrows, lanes].reshape(tile3)
                    for k in range(group):
                        ii = ig * group + k
                        cnt = _bf(jnp.broadcast_to(c0[hd][ii:ii + 1, :], tile3[1:]))[None]
                        w0 = _bf(jnp.broadcast_to(e0[hd][ii:ii + 1, :], tile3[1:]))[None]
                        g[k] = g[k] + jnp.where(r1 < cnt, e1, jnp.zeros_like(e1)) * w0
                for k in range(group):
                    r0 = (ig * group + k) * PEER_NKEYS + jh * half
                    z_scr[r0:r0 + half, lanes] = g[k].reshape(half, LANES) * _bf(a_scr[r0:r0 + half, lanes])

    acc_scr[...] += _mm(vt_ref[...], z_scr[...])

    @pl.when(e_step == pl.num_programs(1) - 1)
    def _finish():
        out = acc_scr[...].T
        o_ref[...] = _layer_norm(ALPHA * x_ref[...] + mod_ref[5:6, :] * out, g_ref[...], b_ref[...], LN_EPS)


def _peer_call(x, mod, wq_t, subkeys, u_b, vt_b, ln_g, ln_b, tb, n_ctx_rows, lat_seq):
    rows, d = x.shape
    et = SUBLANES * PEER_NKEYS
    ncb = n_ctx_rows // tb
    bps = lat_seq // tb
    nq = wq_t.shape[0]
    f32 = jnp.float32
    return pl.pallas_call(
        functools.partial(_peer_kernel, tb=tb),
        grid=(rows // tb, u_b.shape[0] // et),
        in_specs=[
            pl.BlockSpec((tb, d), lambda i, e: (i, 0)),
            pl.BlockSpec((None, 6, d), lambda i, e: (_group_of_block(i, ncb, bps), 0, 0)),
            pl.BlockSpec((nq, d), lambda i, e: (0, 0), pipeline_mode=pl.Buffered(1)),
            pl.BlockSpec(subkeys.shape, lambda i, e: (0, 0, 0), pipeline_mode=pl.Buffered(1)),
            pl.BlockSpec((et, d), lambda i, e: (e, 0)),
            pl.BlockSpec((None, d, et), lambda i, e: (e, 0, 0)),
            pl.BlockSpec((1, d), lambda i, e: (0, 0)),
            pl.BlockSpec((1, d), lambda i, e: (0, 0)),
        ],
        out_specs=pl.BlockSpec((tb, d), lambda i, e: (i, 0)),
        out_shape=jax.ShapeDtypeStruct((rows, d), f32),
        scratch_shapes=[
            pltpu.VMEM((d, tb), jnp.bfloat16),
            pltpu.VMEM((PEER_HEADS, PEER_NKEYS, tb), f32),
            pltpu.VMEM((PEER_HEADS, PEER_NKEYS, tb), f32),
            pltpu.VMEM((PEER_HEADS, PEER_NKEYS, tb), f32),
            pltpu.VMEM((PEER_HEADS, PEER_NKEYS, tb), jnp.bfloat16),
            pltpu.VMEM((PEER_HEADS, PEER_NKEYS, tb), jnp.bfloat16),
            pltpu.VMEM((et, tb), f32),
            pltpu.VMEM((et, tb), jnp.bfloat16),
            pltpu.VMEM((d, tb), f32),
        ],
        compiler_params=_params("parallel", "arbitrary"),
        name="peer",
    )(x, mod, wq_t, subkeys, u_b, vt_b, ln_g.reshape(1, d), ln_b.reshape(1, d))


def _rope_tables(seq, tb):
    half = HEAD_DIM // 4
    freqs = ROPE_THETA ** (-jnp.arange(half, dtype=jnp.float32) / half)
    t = jnp.arange(seq)
    cos_parts, sin_parts = [], []
    for pos in (t // GRID_W, t % GRID_W):
        ang = pos.astype(jnp.float32)[:, None] * freqs[None, :]
        c, s = jnp.cos(ang), jnp.sin(ang)
        cos_parts += [c, c]
        sin_parts += [-s, s]
    cos_h = jnp.concatenate(cos_parts, axis=1)
    sin_h = jnp.concatenate(sin_parts, axis=1)
    n_rot = QK_W // HEAD_DIM
    cos_t = jnp.concatenate([jnp.tile(cos_h, (1, n_rot)), jnp.ones((tb, QK_W), jnp.float32)], axis=0)
    sin_t = jnp.concatenate([jnp.tile(sin_h, (1, n_rot)), jnp.zeros((tb, QK_W), jnp.float32)], axis=0)
    return cos_t, sin_t


def _pad_w_in(w):
    split = w.shape[1] - CV_W
    z = jnp.zeros((w.shape[0], IN_PAD - w.shape[1]), w.dtype)
    return jnp.concatenate([w[:, :split], z, w[:, split:]], axis=1)


def _forward(x_prompt, x_sample, cache_k, cache_v, state_rwkv, c, c_ctx, ln_in_g, ln_in_b, w_ada, b_ada,
             w_in, w_out, att_sink, rw_w0, rw_w_up, rw_a0, rw_a_up, rw_g_up, rw_kk, rw_ka, rw_rk,
             rw_lnx_g, rw_lnx_b, conv_w, ln1_g, ln1_b, ln2_g, ln2_b, peer_wq, peer_subkeys, peer_u, peer_v,
             tb_in=512, tb_out=256, tb_peer=512, rw_par=2):
    nb_c, seq_c, d = x_prompt.shape
    nb_l, seq_l, _ = x_sample.shape
    depth = w_in.shape[0]
    n_ctx = nb_c * seq_c
    past = cache_k.shape[2]
    W = RWKV_DIM

    x = jnp.concatenate([x_prompt.reshape(n_ctx, d), x_sample.reshape(nb_l * seq_l, d)], axis=0)
    x = _ln_call(x, ln_in_g, ln_in_b, tb_in)

    n_groups = 1 + nb_l
    g_pad = -(-n_groups // SUBLANES) * SUBLANES
    cvec = jnp.concatenate([c_ctx[None], c, jnp.zeros((g_pad - n_groups, d), jnp.float32)], axis=0)
    mod_all = _mod_call(cvec, w_ada, b_ada).reshape(depth, g_pad, 6, d)

    cos_t, sin_t = _rope_tables(seq_l, tb_in)
    consts = tuple(jnp.asarray(a) for a in _rwkv_consts())
    ck_all = cache_k.reshape(nb_l, depth, past, KV_DIM)
    cv_all = cache_v.reshape(nb_l, depth, past, KV_DIM)
    eye_h = jnp.eye(RWKV_HEADS, dtype=jnp.float32)
    s0_bd_all = jnp.einsum('bldhij,hg->bldhigj', state_rwkv, eye_h).reshape(nb_l, depth, 2, W, W)
    s0_zero = jnp.zeros((nb_c, 2, W, W), jnp.float32)

    ks, vs, sts = [], [], []
    for l in range(depth):
        mod = mod_all[l]
        p_att, p_rw, p_cv = _in_proj_call(x, mod, _bf(_pad_w_in(w_in[l])), cos_t, sin_t, tb_in, n_ctx, seq_l)
        att_c = _ctx_attn_call(p_att, att_sink[l], nb_c, seq_c)
        att_l = _lat_attn_call(p_att, ck_all, cv_all, att_sink[l], l, nb_l, seq_l, n_ctx)
        wts = _rwkv_weights(rw_w0[l], rw_w_up[l], rw_a0[l], rw_a_up[l], rw_g_up[l], rw_kk[l], rw_ka[l],
                            rw_rk[l], rw_lnx_g[l], rw_lnx_b[l])
        rw_c, sfin = _rwkv_call(p_rw, s0_zero, consts, wts, nb_c, seq_c, 0, rw_par, False)
        rw_l, _ = _rwkv_call(p_rw, s0_bd_all[:, l], consts, wts, nb_l, seq_l, n_ctx, rw_par, True)
        x1 = _out_proj_call(att_c, att_l, rw_c, rw_l, p_cv, x, mod, _bf(w_out[l]), conv_w[l],
                            ln1_g[l], ln1_b[l], tb_out, n_ctx, seq_c, seq_l)
        sk = peer_subkeys[l].reshape(2 * PEER_HEADS, PEER_NKEYS, PEER_HALF)
        vt = _bf(peer_v[l]).reshape(-1, SUBLANES * PEER_NKEYS, d).transpose(0, 2, 1)
        x = _peer_call(x1, mod, _bf(peer_wq[l].T), sk, _bf(peer_u[l]), vt,
                       ln2_g[l], ln2_b[l], tb_peer, n_ctx, seq_l)
        ks.append(p_att[:n_ctx, ATT_DIM:ATT_DIM + KV_DIM].reshape(nb_c, seq_c, ATT_KV_HEADS, HEAD_DIM))
        vs.append(p_att[:n_ctx, ATT_DIM + KV_DIM:].reshape(nb_c, seq_c, ATT_KV_HEADS, HEAD_DIM))
        s5 = sfin.reshape(nb_c, 2, RWKV_HEADS, HEAD_DIM, RWKV_HEADS, HEAD_DIM)
        sts.append(jnp.stack([s5[:, :, h, :, h, :] for h in range(RWKV_HEADS)], axis=2))

    y_prompt = x[:n_ctx].reshape(nb_c, seq_c, d)
    y_sample = x[n_ctx:].reshape(nb_l, seq_l, d)
    return (y_prompt, y_sample, jnp.stack(ks, axis=1), jnp.stack(vs, axis=1), jnp.stack(sts, axis=1))


def kernel(x_prompt, x_sample, cache_k, cache_v, state_rwkv, c, c_ctx, ln_in_g, ln_in_b, w_ada, b_ada, w_in, w_out, att_sink, rw_w0, rw_w_up, rw_a0, rw_a_up, rw_g_up, rw_kk, rw_ka, rw_rk, rw_lnx_g, rw_lnx_b, conv_w, ln1_g, ln1_b, ln2_g, ln2_b, peer_wq, peer_subkeys, peer_u, peer_v):
    return _forward(x_prompt, x_sample, cache_k, cache_v, state_rwkv, c, c_ctx, ln_in_g, ln_in_b, w_ada, b_ada,
                    w_in, w_out, att_sink, rw_w0, rw_w_up, rw_a0, rw_a_up, rw_g_up, rw_kk, rw_ka, rw_rk,
                    rw_lnx_g, rw_lnx_b, conv_w, ln1_g, ln1_b, ln2_g, ln2_b, peer_wq, peer_subkeys, peer_u, peer_v)
```

```python
import functools
import math

import jax
import jax.numpy as jnp
import numpy as np
from jax import lax
from jax.experimental import pallas as pl
from jax.experimental.pallas import tpu as pltpu

D_MODEL = 1024
DEPTH = 4
GRID_W = 64
HEAD_DIM = 64
ATT_DIM = 512
RWKV_DIM = 256
CONV_DIM = 256
ATT_HEADS = 8
ATT_KV_HEADS = 2
ATT_GROUP = 4
KV_DIM = 128
WINDOW = 128
ATT_BLOCK = 128
ROPE_THETA = 10000.0
RWKV_HEADS = 4
W_RANK = 32
A_RANK = 32
G_RANK = 64
RWKV_LNX_EPS = 64e-5
PEER_HEADS = 8
PEER_NKEYS = 128
PEER_TOPK = 16
PEER_HALF = 128
LN_EPS = 1e-5
ALPHA = (2 * DEPTH) ** 0.25
NEG = -1e30

LANES = 128
SUBLANES = 8
VMEM_LIMIT = 56 * 1024 * 1024

QK_W = ATT_DIM + KV_DIM
ATT_W = ATT_DIM + 2 * KV_DIM
RW_W = 1024
CV_W = 3 * CONV_DIM
IN_PAD = ATT_W + RW_W + CV_W
RW_LOW = 3 * RWKV_DIM

RW_CHUNK = 64
PEER_SCORE_PASSES = 3
HIGHEST = lax.Precision.HIGHEST
NN = (((1,), (0,)), ((), ()))
NT = (((1,), (1,)), ((), ()))
TN = (((0,), (0,)), ((), ()))


def _mm(a, b, dims=NN, precision=None):
    return lax.dot_general(a, b, dims, precision=precision, preferred_element_type=jnp.float32)


def _bf(x):
    return x.astype(jnp.bfloat16)


def _split(x):
    hi = _bf(x)
    return hi, _bf(x - hi.astype(jnp.float32))


def _mm_split(a, b, dims, passes):
    out = _mm(a[0], b[0], dims)
    if passes == 3:
        out = out + (_mm(a[0], b[1], dims) + _mm(a[1], b[0], dims))
    return out


def _mm_terms(x, exact, terms, x_is_rhs=False):
    out = None
    rest = x
    for _ in range(terms):
        piece = _bf(rest)
        rest = rest - piece.astype(jnp.float32)
        part = _mm(exact, piece) if x_is_rhs else _mm(piece, exact)
        out = part if out is None else out + part
    return out


RW_PASSES = {"gram_n": 1, "gram": 1, "inv": 1, "loc": 1, "seq": 1}


def _layer_norm(x, g, b, eps):
    mu = jnp.mean(x, axis=-1, keepdims=True)
    xc = x - mu
    var = jnp.mean(xc * xc, axis=-1, keepdims=True)
    return xc * lax.rsqrt(var + eps) * g + b


def _params(*sem):
    return pltpu.CompilerParams(dimension_semantics=sem, vmem_limit_bytes=VMEM_LIMIT)


def _full(shape):
    nd = len(shape)
    return pl.BlockSpec(shape, lambda *_: (0,) * nd)


def _mod_kernel(c_ref, w_ref, b_ref, o_ref):
    cv = c_ref[...]
    s = cv * jax.nn.sigmoid(cv)
    o_ref[...] = _mm(s, w_ref[...], precision=HIGHEST) + b_ref[...]


def _mod_call(cvec, w_ada, b_ada):
    depth, d, n6 = w_ada.shape
    rows = cvec.shape[0]
    nt = n6 // d
    return pl.pallas_call(
        _mod_kernel,
        grid=(depth, nt),
        in_specs=[
            pl.BlockSpec((rows, d), lambda l, n: (0, 0)),
            pl.BlockSpec((None, d, d), lambda l, n: (l, 0, n)),
            pl.BlockSpec((None, 1, d), lambda l, n: (l, 0, n)),
        ],
        out_specs=pl.BlockSpec((None, rows, d), lambda l, n: (l, 0, n)),
        out_shape=jax.ShapeDtypeStruct((depth, rows, n6), jnp.float32),
        compiler_params=_params("parallel", "parallel"),
        name="adaln_mod",
    )(cvec, w_ada, b_ada.reshape(depth, 1, n6))


def _ln_kernel(x_ref, g_ref, b_ref, o_ref):
    o_ref[...] = _layer_norm(x_ref[...], g_ref[...], b_ref[...], LN_EPS)


def _ln_call(x, g, b, tb):
    rows, d = x.shape
    return pl.pallas_call(
        _ln_kernel,
        grid=(rows // tb,),
        in_specs=[pl.BlockSpec((tb, d), lambda i: (i, 0)), _full((1, d)), _full((1, d))],
        out_specs=pl.BlockSpec((tb, d), lambda i: (i, 0)),
        out_shape=jax.ShapeDtypeStruct((rows, d), jnp.float32),
        compiler_params=_params("parallel"),
        name="ln_in",
    )(x, g.reshape(1, d), b.reshape(1, d))


def _group_of_block(i, n_ctx_blocks, blocks_per_lat_seq):
    return jnp.where(i < n_ctx_blocks, 0, 1 + (i - n_ctx_blocks) // blocks_per_lat_seq)


def _in_proj_kernel(x_ref, mod_ref, w_ref, cos_ref, sin_ref, att_ref, rw_ref, cv_ref):
    x = x_ref[...]
    h = x * (1.0 + mod_ref[1:2, :]) + mod_ref[0:1, :]
    p = _mm(_bf(h), w_ref[...])
    qk = p[:, :QK_W]
    lane = lax.broadcasted_iota(jnp.int32, qk.shape, 1)
    partner = jnp.where((lane % 32) < 16,
                        pltpu.roll(qk, QK_W - 16, 1),
                        pltpu.roll(qk, 16, 1))
    att_ref[:, :QK_W] = qk * cos_ref[...] + partner * sin_ref[...]
    att_ref[:, QK_W:] = p[:, QK_W:ATT_W]
    rw_ref[...] = p[:, ATT_W:ATT_W + RW_W]
    cv_ref[...] = p[:, ATT_W + RW_W:]


def _in_proj_call(x, mod, w_in_p, cos_t, sin_t, tb, n_ctx_rows, lat_seq):
    rows, d = x.shape
    ncb = n_ctx_rows // tb
    bps = lat_seq // tb

    def mod_map(i):
        return (_group_of_block(i, ncb, bps), 0, 0)

    def rope_map(i):
        return (jnp.where(i < ncb, bps, (i - ncb) % bps), 0)

    return pl.pallas_call(
        _in_proj_kernel,
        grid=(rows // tb,),
        in_specs=[
            pl.BlockSpec((tb, d), lambda i: (i, 0)),
            pl.BlockSpec((None, 6, d), mod_map),
            _full((d, IN_PAD)),
            pl.BlockSpec((tb, QK_W), rope_map),
            pl.BlockSpec((tb, QK_W), rope_map),
        ],
        out_specs=[
            pl.BlockSpec((tb, ATT_W), lambda i: (i, 0)),
            pl.BlockSpec((tb, RW_W), lambda i: (i, 0)),
            pl.BlockSpec((tb, CV_W), lambda i: (i, 0)),
        ],
        out_shape=[
            jax.ShapeDtypeStruct((rows, ATT_W), jnp.float32),
            jax.ShapeDtypeStruct((rows, RW_W), jnp.float32),
            jax.ShapeDtypeStruct((rows, CV_W), jnp.float32),
        ],
        compiler_params=_params("parallel"),
        name="in_proj",
    )(x, mod, w_in_p, cos_t, sin_t)


def _ctx_attn_kernel(sink_ref, p_ref, o_ref):
    scale = HEAD_DIM ** -0.5
    for kv in range(ATT_KV_HEADS):
        k = _bf(p_ref[:, ATT_DIM + kv * HEAD_DIM:ATT_DIM + (kv + 1) * HEAD_DIM])
        v = _bf(p_ref[:, ATT_DIM + KV_DIM + kv * HEAD_DIM:ATT_DIM + KV_DIM + (kv + 1) * HEAD_DIM])
        for g in range(ATT_GROUP):
            hd = kv * ATT_GROUP + g
            q = _bf(p_ref[:, hd * HEAD_DIM:(hd + 1) * HEAD_DIM])
            s = _mm(q, k, NT) * scale
            sink = sink_ref[hd]
            m = jnp.maximum(jnp.max(s, axis=-1, keepdims=True), sink)
            e = jnp.exp(s - m)
            den = jnp.sum(e, axis=-1, keepdims=True) + jnp.exp(sink - m)
            o_ref[:, hd * HEAD_DIM:(hd + 1) * HEAD_DIM] = _mm(_bf(e), v) / den


def _ctx_attn_call(p_att, sink, n_seq, seq):
    return pl.pallas_call(
        _ctx_attn_kernel,
        grid=(n_seq,),
        in_specs=[
            pl.BlockSpec(memory_space=pltpu.SMEM),
            pl.BlockSpec((seq, ATT_W), lambda b: (b, 0)),
        ],
        out_specs=pl.BlockSpec((seq, ATT_DIM), lambda b: (b, 0)),
        out_shape=jax.ShapeDtypeStruct((n_seq * seq, ATT_DIM), jnp.float32),
        compiler_params=_params("parallel"),
        name="ctx_attn",
    )(sink, p_att)


def _lat_attn_kernel(sink_ref, own_ref, prev_ref, next_ref, ck_ref, cv_ref, o_ref, *, n_blocks):
    n = pl.program_id(1)
    scale = HEAD_DIM ** -0.5
    qi = lax.broadcasted_iota(jnp.int32, (ATT_BLOCK, 3 * ATT_BLOCK), 0)
    kj = lax.broadcasted_iota(jnp.int32, (ATT_BLOCK, 3 * ATT_BLOCK), 1)
    rel = kj - ATT_BLOCK - qi
    kpos = (n - 1) * ATT_BLOCK + kj
    mask = (jnp.abs(rel) <= WINDOW) & (kpos >= 0) & (kpos < n_blocks * ATT_BLOCK)
    mask4 = jnp.concatenate([mask] * ATT_GROUP, axis=0)
    row_group = lax.broadcasted_iota(jnp.int32, (ATT_GROUP * ATT_BLOCK, 1), 0) // ATT_BLOCK
    for kv in range(ATT_KV_HEADS):
        ks = slice(ATT_DIM + kv * HEAD_DIM, ATT_DIM + (kv + 1) * HEAD_DIM)
        vs = slice(ATT_DIM + KV_DIM + kv * HEAD_DIM, ATT_DIM + KV_DIM + (kv + 1) * HEAD_DIM)
        kw = _bf(jnp.concatenate([prev_ref[:, ks], own_ref[:, ks], next_ref[:, ks]], axis=0))
        vw = _bf(jnp.concatenate([prev_ref[:, vs], own_ref[:, vs], next_ref[:, vs]], axis=0))
        ck = _bf(ck_ref[:, kv * HEAD_DIM:(kv + 1) * HEAD_DIM])
        cv = _bf(cv_ref[:, kv * HEAD_DIM:(kv + 1) * HEAD_DIM])
        q4 = _bf(jnp.concatenate(
            [own_ref[:, (kv * ATT_GROUP + g) * HEAD_DIM:(kv * ATT_GROUP + g + 1) * HEAD_DIM]
             for g in range(ATT_GROUP)], axis=0))
        s_ctx = _mm(q4, ck, NT) * scale
        s_win = jnp.where(mask4, _mm(q4, kw, NT) * scale, NEG)
        sink = jnp.full((ATT_GROUP * ATT_BLOCK, 1), sink_ref[kv * ATT_GROUP], jnp.float32)
        for g in range(1, ATT_GROUP):
            sink = jnp.where(row_group == g, sink_ref[kv * ATT_GROUP + g], sink)
        m = jnp.maximum(jnp.maximum(jnp.max(s_ctx, axis=-1, keepdims=True),
                                    jnp.max(s_win, axis=-1, keepdims=True)), sink)
        e_ctx = jnp.exp(s_ctx - m)
        e_win = jnp.exp(s_win - m)
        den = (jnp.sum(e_ctx, axis=-1, keepdims=True) + jnp.sum(e_win, axis=-1, keepdims=True)
               + jnp.exp(sink - m))
        o4 = (_mm(_bf(e_ctx), cv) + _mm(_bf(e_win), vw)) / den
        for g in range(ATT_GROUP):
            hd = kv * ATT_GROUP + g
            o_ref[:, hd * HEAD_DIM:(hd + 1) * HEAD_DIM] = o4[g * ATT_BLOCK:(g + 1) * ATT_BLOCK]


def _lat_attn_call(p_att, ck, cv, sink, layer, n_seq, seq, row_off):
    nb = seq // ATT_BLOCK
    off = row_off // ATT_BLOCK
    past = ck.shape[2]

    def own(b, n):
        return (off + b * nb + n, 0)

    def prev(b, n):
        return (off + b * nb + jnp.maximum(n - 1, 0), 0)

    def nxt(b, n):
        return (off + b * nb + jnp.minimum(n + 1, nb - 1), 0)

    return pl.pallas_call(
        functools.partial(_lat_attn_kernel, n_blocks=nb),
        grid=(n_seq, nb),
        in_specs=[
            pl.BlockSpec(memory_space=pltpu.SMEM),
            pl.BlockSpec((ATT_BLOCK, ATT_W), own),
            pl.BlockSpec((ATT_BLOCK, ATT_W), prev),
            pl.BlockSpec((ATT_BLOCK, ATT_W), nxt),
            pl.BlockSpec((None, None, past, KV_DIM), lambda b, n: (b, layer, 0, 0)),
            pl.BlockSpec((None, None, past, KV_DIM), lambda b, n: (b, layer, 0, 0)),
        ],
        out_specs=pl.BlockSpec((ATT_BLOCK, ATT_DIM), lambda b, n: (b * nb + n, 0)),
        out_shape=jax.ShapeDtypeStruct((n_seq * seq, ATT_DIM), jnp.float32),
        compiler_params=_params("parallel", "parallel"),
        name="lat_attn",
    )(sink, p_att, p_att, p_att, ck, cv)


def _rwkv_consts():
    L, H, W = RW_CHUNK, RWKV_HEADS, RWKV_DIM
    t = np.arange(L)
    inc = np.stack([(t[None, :] <= t[:, None]), (t[None, :] >= t[:, None])]).astype(np.float32)
    stc = np.stack([(t[None, :] < t[:, None]), (t[None, :] > t[:, None])]).astype(np.float32)
    eye_h = np.eye(H, dtype=np.float32)
    inc_bd = np.stack([np.kron(eye_h, inc[d]) for d in range(2)])
    stc_bd = np.stack([np.kron(eye_h, stc[d]) for d in range(2)])
    blk = np.kron(eye_h, np.ones((L, HEAD_DIM), np.float32))
    eye = np.eye(W, dtype=np.float32)
    return inc, inc_bd, stc_bd, blk, eye


def _rwkv_kernel(p_ref, s0_ref, inc_ref, incbd_ref, stcbd_ref, blk_ref, eye_ref,
                 w0_ref, wup_ref, a0_ref, aup_ref, gup_ref, kkw_ref, kaw_ref, rkw_ref, lng_ref, lnb_ref,
                 o_ref, sfin_ref, y_scr, bon_scr, *, seq, n_par):
    L = RW_CHUNK
    nc = seq // L
    blk = _bf(blk_ref[...])
    ones_bd = blk
    eye = eye_ref[...]
    y_scr[...] = jnp.zeros_like(y_scr)
    bon_scr[...] = jnp.zeros_like(bon_scr)

    def stack(x):
        return tuple(jnp.concatenate([z] * RWKV_HEADS, axis=0) * blk for z in _split(x))

    def chunk(idx, r0, s_in, s_out):
        d = idx % 2
        x = p_ref[pl.ds(r0, L), :]
        rr = x[:, 0:RWKV_DIM]
        rk = x[:, RWKV_DIM:2 * RWKV_DIM]
        rv = x[:, 2 * RWKV_DIM:3 * RWKV_DIM]
        low = x[:, RW_LOW:RW_LOW + LANES]
        zw = w0_ref[d] + _mm_split(_split(jnp.tanh(low)), _split(wup_ref[d]), NN, 3)
        yield
        u = -zw
        softplus = jnp.maximum(u, 0.0) + jnp.log1p(jnp.exp(-jnp.abs(u)))
        lw = -jnp.exp(-softplus - 0.5)
        asig = jax.nn.sigmoid(a0_ref[d] + _mm_split(_split(low), _split(aup_ref[d]), NN, 3))
        yield
        kd = rk * (1.0 + (asig - 1.0) * kaw_ref[...])
        kkr = rk * kkw_ref[...]
        kk = kkr * lax.rsqrt(jnp.maximum(_mm_terms(kkr * kkr, ones_bd, 2), 1e-24))
        yield
        a_s = -kk
        b_s = kk * asig
        bon_scr[pl.ds(r0, L), :] += _mm_terms(rr * kd * rkw_ref[...], ones_bd, 2) * rv
        yield

        cl = _mm_terms(lw, _bf(inc_ref[d]), 3, x_is_rhs=True)
        yield
        cl_end = cl[L - 1:L, :] if d == 0 else cl[0:1, :]
        e_neg = jnp.exp(-cl)
        e_tail = jnp.exp(cl_end - cl)
        a_st = stack(a_s * jnp.exp(cl - lw))
        b_st = stack(b_s * e_neg)
        k_st = stack(kd * e_neg)
        r_st = stack(rr * jnp.exp(cl))
        v_st = stack(rv)
        bh_st = stack(b_s * e_tail)
        kh_st = stack(kd * e_tail)
        stc = stcbd_ref[d]
        inc = incbd_ref[d]
        n_m = stc * _mm_split(a_st, b_st, NT, RW_PASSES["gram_n"])
        yield
        m1 = stc * _mm_split(a_st, k_st, NT, RW_PASSES["gram"])
        yield
        fill = {}
        fillers = [
            lambda: fill.update(p=inc * _mm_split(r_st, b_st, NT, RW_PASSES["gram"])),
            lambda: fill.update(m1v=_mm_split(_split(m1), v_st, NN, RW_PASSES["loc"])),
            lambda: fill.update(q=inc * _mm_split(r_st, k_st, NT, RW_PASSES["gram"])),
            lambda: fill.update(vk=_mm_split(v_st, kh_st, TN, RW_PASSES["loc"])),
            lambda: fill.update(rs=_mm_split(r_st, _split(s_in), NT, RW_PASSES["seq"])),
        ]
        t_m = eye + n_m
        n_p = _split(n_m)
        for it in range(5):
            n_sq = _mm_split(n_p, n_p, NN, RW_PASSES["inv"])
            yield
            fillers[it]()
            yield
            n_p = _split(n_sq)
            t_m = t_m + _mm_split(_split(t_m), n_p, NN, RW_PASSES["inv"])
            yield
        t_s = _split(t_m)
        qv = _mm_split(_split(fill["q"]), v_st, NN, RW_PASSES["loc"])
        yield
        u_loc = _mm_split(t_s, _split(fill["m1v"]), NN, RW_PASSES["loc"])
        yield
        w_t = _mm_split(t_s, a_st, NN, RW_PASSES["loc"])
        yield
        u_m = _mm_split(_split(w_t), _split(s_in), NT, RW_PASSES["seq"]) + u_loc
        yield
        u_s = _split(u_m)
        y_bd = fill["rs"] + _mm_split(_split(fill["p"]), u_s, NN, RW_PASSES["seq"]) + qv
        y = y_bd[0:L] + y_bd[L:2 * L] + y_bd[2 * L:3 * L] + y_bd[3 * L:4 * L]
        y_scr[pl.ds(r0, L), :] += y
        yield
        s_out[idx] = s_in * jnp.exp(cl_end) + _mm_split(u_s, bh_st, TN, RW_PASSES["seq"]) + fill["vk"]

    def body(c, carry):
        s_out = {}
        stages = []
        for sq in range(n_par):
            stages.append(chunk(2 * sq, pl.multiple_of(sq * seq + c * L, L), carry[2 * sq], s_out))
            stages.append(chunk(2 * sq + 1, pl.multiple_of(sq * seq + (nc - 1 - c) * L, L), carry[2 * sq + 1],
                                s_out))
        while stages:
            for gen in list(stages):
                if next(gen, StopIteration) is StopIteration:
                    stages.remove(gen)
        return tuple(s_out[i] for i in range(2 * n_par))

    s_fin = lax.fori_loop(0, nc, body, tuple(s0_ref[i // 2, i % 2] for i in range(2 * n_par)))
    for i in range(2 * n_par):
        sfin_ref[i // 2, i % 2] = s_fin[i]

    inv_n = 1.0 / HEAD_DIM

    def epilogue(c, carry):
        r0 = pl.multiple_of(c * L, L)
        y = y_scr[pl.ds(r0, L), :]
        mu = _mm_terms(y, ones_bd, 3) * inv_n
        yc = y - mu
        var = _mm_terms(yc * yc, ones_bd, 2) * inv_n
        yn = yc * lax.rsqrt(var + RWKV_LNX_EPS) * lng_ref[...] + lnb_ref[...]
        gd = p_ref[pl.ds(r0, L), RW_LOW + LANES:RW_LOW + 2 * LANES]
        g = _mm_split(_split(jax.nn.sigmoid(gd)), _split(gup_ref[...]), NN, 3)
        bon = bon_scr[pl.ds(r0, L), :]
        o_ref[pl.ds(r0, L), :] = (yn + bon) * g
        return carry

    lax.fori_loop(0, n_par * nc, epilogue, 0)


def _rwkv_call(p_rw, s0_bd, consts, wts, n_seq, seq, row_off, n_par, single_buffer_input):
    inc, inc_bd, stc_bd, blk, eye = consts
    W = RWKV_DIM
    rows = n_par * seq
    off = row_off // rows
    p_mode = dict(pipeline_mode=pl.Buffered(1)) if single_buffer_input else {}
    in_specs = [
        pl.BlockSpec((rows, RW_W), lambda b: (off + b, 0), **p_mode),
        pl.BlockSpec((n_par, 2, W, W), lambda b: (b, 0, 0, 0)),
        _full(inc.shape), _full(inc_bd.shape), _full(stc_bd.shape), _full(blk.shape), _full(eye.shape),
    ] + [_full(w.shape) for w in wts]
    return pl.pallas_call(
        functools.partial(_rwkv_kernel, seq=seq, n_par=n_par),
        grid=(n_seq // n_par,),
        in_specs=in_specs,
        out_specs=[
            pl.BlockSpec((rows, W), lambda b: (b, 0)),
            pl.BlockSpec((n_par, 2, W, W), lambda b: (b, 0, 0, 0)),
        ],
        out_shape=[
            jax.ShapeDtypeStruct((n_seq * seq, W), jnp.float32),
            jax.ShapeDtypeStruct((n_seq, 2, W, W), jnp.float32),
        ],
        scratch_shapes=[pltpu.VMEM((rows, W), jnp.float32), pltpu.VMEM((rows, W), jnp.float32)],
        compiler_params=_params("parallel"),
        name="rwkv",
    )(p_rw, s0_bd, inc, inc_bd, stc_bd, blk, eye, *wts)


def _rwkv_weights(w0, w_up, a0, a_up, g_up, kk, ka, rk, lng, lnb):
    W = RWKV_DIM
    wup_p = jnp.zeros((2, LANES, W), jnp.float32)
    aup_p = jnp.zeros((2, LANES, W), jnp.float32)
    for d in range(2):
        wup_p = wup_p.at[d, d * W_RANK:(d + 1) * W_RANK].set(w_up[d])
        aup_p = aup_p.at[d, 2 * W_RANK + d * A_RANK:2 * W_RANK + (d + 1) * A_RANK].set(a_up[d])
    gup_p = jnp.zeros((LANES, W), jnp.float32).at[:G_RANK].set(g_up)
    row = lambda v: v.reshape(1, W)
    return (w0.reshape(2, 1, W), wup_p, a0.reshape(2, 1, W), aup_p, gup_p,
            row(kk), row(ka), row(rk), row(lng), row(lnb))


def _out_proj_kernel(attc_ref, attl_ref, rwc_ref, rwl_ref, cv_ref, cvp_ref, cvn_ref, x_ref, mod_ref,
                     w_ref, cw_ref, g_ref, b_ref, o_ref, *, tb, ncb, bps_c, bps):
    i = pl.program_id(0)
    is_ctx = i < ncb
    pos = jnp.where(is_ctx, i % bps_c, (i - ncb) % bps)
    last = jnp.where(is_ctx, bps_c - 1, bps - 1)
    att = jnp.where(is_ctx, attc_ref[...], attl_ref[...])
    rw = jnp.where(is_ctx, rwc_ref[...], rwl_ref[...])

    cb = cv_ref[:, 0:CONV_DIM]
    z = cv_ref[:, CONV_DIM:2 * CONV_DIM] * cv_ref[:, 2 * CONV_DIM:3 * CONV_DIM]
    halo_p = cvp_ref[SUBLANES - 1:SUBLANES, CONV_DIM:2 * CONV_DIM] * cvp_ref[SUBLANES - 1:SUBLANES, 2 * CONV_DIM:]
    halo_n = cvn_ref[0:1, CONV_DIM:2 * CONV_DIM] * cvn_ref[0:1, 2 * CONV_DIM:]
    halo_p = jnp.where(pos > 0, halo_p, 0.0)
    halo_n = jnp.where(pos < last, halo_n, 0.0)
    row = lax.broadcasted_iota(jnp.int32, z.shape, 0)
    z_prev = jnp.where(row == 0, halo_p, pltpu.roll(z, 1, 0))
    z_next = jnp.where(row == tb - 1, halo_n, pltpu.roll(z, tb - 1, 0))
    conv = cb * (cw_ref[0:1, :] * z_prev + cw_ref[1:2, :] * z + cw_ref[2:3, :] * z_next)

    mix = (_mm(_bf(att), w_ref[0:ATT_DIM, :])
           + _mm(_bf(rw), w_ref[ATT_DIM:ATT_DIM + RWKV_DIM, :])
           + _mm(_bf(conv), w_ref[ATT_DIM + RWKV_DIM:, :]))
    o_ref[...] = _layer_norm(ALPHA * x_ref[...] + mod_ref[2:3, :] * mix, g_ref[...], b_ref[...], LN_EPS)


def _out_proj_call(att_c, att_l, rw_c, rw_l, p_cv, x, mod, w_out_b, conv_w, ln_g, ln_b, tb, n_ctx_rows, ctx_seq,
                   lat_seq):
    rows, d = x.shape
    ncb = n_ctx_rows // tb
    bps = lat_seq // tb
    sub = tb // SUBLANES
    n_sub = rows // SUBLANES

    def ctx_map(i):
        return (jnp.minimum(i, ncb - 1), 0)

    def lat_map(i):
        return (jnp.maximum(i - ncb, 0), 0)

    return pl.pallas_call(
        functools.partial(_out_proj_kernel, tb=tb, ncb=ncb, bps_c=ctx_seq // tb, bps=bps),
        grid=(rows // tb,),
        in_specs=[
            pl.BlockSpec((tb, ATT_DIM), ctx_map),
            pl.BlockSpec((tb, ATT_DIM), lat_map),
            pl.BlockSpec((tb, RWKV_DIM), ctx_map),
            pl.BlockSpec((tb, RWKV_DIM), lat_map),
            pl.BlockSpec((tb, CV_W), lambda i: (i, 0)),
            pl.BlockSpec((SUBLANES, CV_W), lambda i: (jnp.maximum(i * sub - 1, 0), 0)),
            pl.BlockSpec((SUBLANES, CV_W), lambda i: (jnp.minimum((i + 1) * sub, n_sub - 1), 0)),
            pl.BlockSpec((tb, d), lambda i: (i, 0)),
            pl.BlockSpec((None, 6, d), lambda i: (_group_of_block(i, ncb, bps), 0, 0)),
            _full((d, d)),
            _full((3, CONV_DIM)),
            _full((1, d)),
            _full((1, d)),
        ],
        out_specs=pl.BlockSpec((tb, d), lambda i: (i, 0)),
        out_shape=jax.ShapeDtypeStruct((rows, d), jnp.float32),
        compiler_params=_params("parallel"),
        name="out_proj",
    )(att_c, att_l, rw_c, rw_l, p_cv, p_cv, p_cv, x, mod, w_out_b, conv_w, ln_g.reshape(1, d), ln_b.reshape(1, d))


def _gelu_tanh(x):
    return 0.5 * x * (1.0 + jnp.tanh(math.sqrt(2.0 / math.pi) * (x + 0.044715 * (x * x * x))))


def _extract_top(work, iota, n_out, want_rank=False, by_value=False):
    vals = []
    big = work.shape[0]
    rank = jnp.full(work.shape, float(big), jnp.float32) if want_rank else None
    for k in range(n_out):
        m = jnp.max(work, axis=0, keepdims=True)
        if by_value:
            sel = work == m
        else:
            idx = jnp.min(jnp.where(work == m, iota, big), axis=0, keepdims=True)
            sel = iota == idx
        work = jnp.where(sel, -jnp.inf, work)
        if want_rank:
            rank = jnp.where(sel, float(k), rank)
        vals.append(m)
        yield
    return vals, work, rank


def _lockstep(*gens):
    results = [None] * len(gens)
    live = list(range(len(gens)))
    while live:
        for i in list(live):
            try:
                next(gens[i])
            except StopIteration as stop:
                results[i] = stop.value
                live.remove(i)
        yield
    return results


def _run_lockstep(*gens):
    stepper = _lockstep(*gens)
    while True:
        try:
            next(stepper)
        except StopIteration as stop:
            return stop.value


def _peer_kernel(x_ref, mod_ref, wq_ref, sk_ref, u_ref, vt_ref, g_ref, b_ref, o_ref,
                 ht_scr, s0_scr, e0_scr, s1_scr, r1_scr, e1_scr, a_scr, z_scr, acc_scr, *, tb):
    e_step = pl.program_id(1)
    n_lt = tb // LANES
    K = PEER_TOPK

    @pl.when(e_step == 0)
    def _route():
        h2 = x_ref[...] * (1.0 + mod_ref[4:5, :]) + mod_ref[3:4, :]
        ht = _bf(h2.T)
        ht_scr[...] = ht
        qt = _mm(wq_ref[...], ht)
        for hp in range(2 * PEER_HEADS):
            sc = _mm_split(_split(sk_ref[hp]), _split(qt[hp * PEER_HALF:(hp + 1) * PEER_HALF, :]), NN,
                           PEER_SCORE_PASSES)
            if hp % 2 == 0:
                s0_scr[hp // 2] = sc
            else:
                s1_scr[hp // 2] = sc

        iota_k = lax.broadcasted_iota(jnp.int32, (PEER_NKEYS, LANES), 0)
        iota_c = lax.broadcasted_iota(jnp.int32, ((SUBLANES + 2) * SUBLANES, LANES), 0)

        def route_tile(tile, by_value):
            hd, lanes, sc0, sc1 = tile
            (v0, w0, _), (v1, w1, rank1) = yield from _lockstep(
                _extract_top(sc0, iota_k, K, by_value=by_value),
                _extract_top(sc1, iota_k, K, want_rank=True, by_value=by_value))
            cand0 = w0 != sc0
            cand1 = w1 != sc1
            lo1 = jnp.concatenate(v1[:SUBLANES], axis=0)
            hi1 = jnp.concatenate(v1[SUBLANES:], axis=0)
            hi0 = jnp.concatenate(v0[SUBLANES:], axis=0)
            pair = jnp.concatenate([v0[a] + lo1 for a in range(SUBLANES)] + [v0[0] + hi1, hi0 + v1[0]], axis=0)
            f, wp, _ = yield from _extract_top(pair, iota_c, K + 1, by_value=by_value)
            zsum = jnp.zeros_like(f[0])
            for k in range(K):
                zsum = zsum + jnp.exp(f[k] - f[0])
            th = 0.5 * (f[K - 1] + f[K]) - sc0
            count0 = jnp.zeros_like(sc0)
            for b in range(K):
                count0 = count0 + jnp.where(v1[b] > th, 1.0, 0.0)
            s0_scr[hd, :, lanes] = jnp.where(cand0, count0, 0.0)
            e0_scr[hd, :, lanes] = jnp.where(cand0, jnp.exp(sc0 - v0[0]), 0.0) / zsum
            e1_scr[hd, :, lanes] = _bf(jnp.where(cand1, jnp.exp(sc1 - v1[0]), 0.0))
            r1_scr[hd, :, lanes] = _bf(rank1)
            if not by_value:
                return None
            removed = [jnp.sum(jnp.where(c, 1.0, 0.0), axis=0, keepdims=True) for c in (cand0, cand1, wp != pair)]
            wrong = (removed[0] != K) | (removed[1] != K) | (removed[2] != K + 1)
            return jnp.max(jnp.where(wrong, 1.0, 0.0)) > 0.0

        def per_tile_pair(j, carry):
            tiles = []
            for t in (2 * j, 2 * j + 1):
                hd = t // n_lt
                lanes = pl.ds(pl.multiple_of((t % n_lt) * LANES, LANES), LANES)
                tiles.append((hd, lanes, s0_scr[hd, :, lanes], s1_scr[hd, :, lanes]))
            had_ties = _run_lockstep(*[route_tile(tile, by_value=True) for tile in tiles])
            for tile, redo in zip(tiles, had_ties):
                @pl.when(redo)
                def _exact(tile=tile):
                    _run_lockstep(route_tile(tile, by_value=False))
            return carry

        lax.fori_loop(0, PEER_HEADS * n_lt // 2, per_tile_pair, 0)
        acc_scr[...] = jnp.zeros_like(acc_scr)

    a_scr[...] = _gelu_tanh(_mm(u_ref[...], ht_scr[...]))

    keys = pl.ds(pl.multiple_of(e_step * SUBLANES, SUBLANES), SUBLANES)
    half = PEER_NKEYS // 2
    group = SUBLANES // 2
    packed = 2 * SUBLANES
    tile3 = (half // packed, packed, LANES)
    for ln in range(n_lt):
        lanes = slice(ln * LANES, (ln + 1) * LANES)
        c0 = [s0_scr[hd, keys, lanes] for hd in range(PEER_HEADS)]
        e0 = [e0_scr[hd, keys, lanes] for hd in range(PEER_HEADS)]
        for jh in range(2):
            jrows = slice(jh * half, (jh + 1) * half)
            for ig in range(SUBLANES // group):
                g = [jnp.zeros(tile3, jnp.bfloat16) for _ in range(group)]
                for hd in range(PEER_HEADS):
                    r1 = r1_scr[hd, jrows, lanes].reshape(tile3)
                    e1 = e1_scr[hd, jrows, lanes].reshape(tile3)
                    for k in range(group):
                        ii = ig * group + k
                        cnt = _bf(jnp.broadcast_to(c0[hd][ii:ii + 1, :], tile3[1:]))[None]
                        w0 = _bf(jnp.broadcast_to(e0[hd][ii:ii + 1, :], tile3[1:]))[None]
                        g[k] = g[k] + jnp.where(r1 < cnt, e1, jnp.zeros_like(e1)) * w0
                for k in range(group):
                    r0 = (ig * group + k) * PEER_NKEYS + jh * half
                    z_scr[r0:r0 + half, lanes] = g[k].reshape(half, LANES) * _bf(a_scr[r0:r0 + half, lanes])

    acc_scr[...] += _mm(vt_ref[...], z_scr[...])

    @pl.when(e_step == pl.num_programs(1) - 1)
    def _finish():
        out = acc_scr[...].T
        o_ref[...] = _layer_norm(ALPHA * x_ref[...] + mod_ref[5:6, :] * out, g_ref[...], b_ref[...], LN_EPS)


def _peer_call(x, mod, wq_t, subkeys, u_b, vt_b, ln_g, ln_b, tb, n_ctx_rows, lat_seq):
    rows, d = x.shape
    et = SUBLANES * PEER_NKEYS
    ncb = n_ctx_rows // tb
    bps = lat_seq // tb
    nq = wq_t.shape[0]
    f32 = jnp.float32
    return pl.pallas_call(
        functools.partial(_peer_kernel, tb=tb),
        grid=(rows // tb, u_b.shape[0] // et),
        in_specs=[
            pl.BlockSpec((tb, d), lambda i, e: (i, 0)),
            pl.BlockSpec((None, 6, d), lambda i, e: (_group_of_block(i, ncb, bps), 0, 0)),
            pl.BlockSpec((nq, d), lambda i, e: (0, 0), pipeline_mode=pl.Buffered(1)),
            pl.BlockSpec(subkeys.shape, lambda i, e: (0, 0, 0), pipeline_mode=pl.Buffered(1)),
            pl.BlockSpec((et, d), lambda i, e: (e, 0)),
            pl.BlockSpec((None, d, et), lambda i, e: (e, 0, 0)),
            pl.BlockSpec((1, d), lambda i, e: (0, 0)),
            pl.BlockSpec((1, d), lambda i, e: (0, 0)),
        ],
        out_specs=pl.BlockSpec((tb, d), lambda i, e: (i, 0)),
        out_shape=jax.ShapeDtypeStruct((rows, d), f32),
        scratch_shapes=[
            pltpu.VMEM((d, tb), jnp.bfloat16),
            pltpu.VMEM((PEER_HEADS, PEER_NKEYS, tb), f32),
            pltpu.VMEM((PEER_HEADS, PEER_NKEYS, tb), f32),
            pltpu.VMEM((PEER_HEADS, PEER_NKEYS, tb), f32),
            pltpu.VMEM((PEER_HEADS, PEER_NKEYS, tb), jnp.bfloat16),
            pltpu.VMEM((PEER_HEADS, PEER_NKEYS, tb), jnp.bfloat16),
            pltpu.VMEM((et, tb), f32),
            pltpu.VMEM((et, tb), jnp.bfloat16),
            pltpu.VMEM((d, tb), f32),
        ],
        compiler_params=_params("parallel", "arbitrary"),
        name="peer",
    )(x, mod, wq_t, subkeys, u_b, vt_b, ln_g.reshape(1, d), ln_b.reshape(1, d))


def _rope_tables(seq, tb):
    half = HEAD_DIM // 4
    freqs = ROPE_THETA ** (-jnp.arange(half, dtype=jnp.float32) / half)
    t = jnp.arange(seq)
    cos_parts, sin_parts = [], []
    for pos in (t // GRID_W, t % GRID_W):
        ang = pos.astype(jnp.float32)[:, None] * freqs[None, :]
        c, s = jnp.cos(ang), jnp.sin(ang)
        cos_parts += [c, c]
        sin_parts += [-s, s]
    cos_h = jnp.concatenate(cos_parts, axis=1)
    sin_h = jnp.concatenate(sin_parts, axis=1)
    n_rot = QK_W // HEAD_DIM
    cos_t = jnp.concatenate([jnp.tile(cos_h, (1, n_rot)), jnp.ones((tb, QK_W), jnp.float32)], axis=0)
    sin_t = jnp.concatenate([jnp.tile(sin_h, (1, n_rot)), jnp.zeros((tb, QK_W), jnp.float32)], axis=0)
    return cos_t, sin_t


def _pad_w_in(w):
    split = w.shape[1] - CV_W
    z = jnp.zeros((w.shape[0], IN_PAD - w.shape[1]), w.dtype)
    return jnp.concatenate([w[:, :split], z, w[:, split:]], axis=1)


def _forward(x_prompt, x_sample, cache_k, cache_v, state_rwkv, c, c_ctx, ln_in_g, ln_in_b, w_ada, b_ada,
             w_in, w_out, att_sink, rw_w0, rw_w_up, rw_a0, rw_a_up, rw_g_up, rw_kk, rw_ka, rw_rk,
             rw_lnx_g, rw_lnx_b, conv_w, ln1_g, ln1_b, ln2_g, ln2_b, peer_wq, peer_subkeys, peer_u, peer_v,
             tb_in=512, tb_out=256, tb_peer=512, rw_par=2):
    nb_c, seq_c, d = x_prompt.shape
    nb_l, seq_l, _ = x_sample.shape
    depth = w_in.shape[0]
    n_ctx = nb_c * seq_c
    past = cache_k.shape[2]
    W = RWKV_DIM

    x = jnp.concatenate([x_prompt.reshape(n_ctx, d), x_sample.reshape(nb_l * seq_l, d)], axis=0)
    x = _ln_call(x, ln_in_g, ln_in_b, tb_in)

    n_groups = 1 + nb_l
    g_pad = -(-n_groups // SUBLANES) * SUBLANES
    cvec = jnp.concatenate([c_ctx[None], c, jnp.zeros((g_pad - n_groups, d), jnp.float32)], axis=0)
    mod_all = _mod_call(cvec, w_ada, b_ada).reshape(depth, g_pad, 6, d)

    cos_t, sin_t = _rope_tables(seq_l, tb_in)
    consts = tuple(jnp.asarray(a) for a in _rwkv_consts())
    ck_all = cache_k.reshape(nb_l, depth, past, KV_DIM)
    cv_all = cache_v.reshape(nb_l, depth, past, KV_DIM)
    eye_h = jnp.eye(RWKV_HEADS, dtype=jnp.float32)
    s0_bd_all = jnp.einsum('bldhij,hg->bldhigj', state_rwkv, eye_h).reshape(nb_l, depth, 2, W, W)
    s0_zero = jnp.zeros((nb_c, 2, W, W), jnp.float32)

    ks, vs, sts = [], [], []
    for l in range(depth):
        mod = mod_all[l]
        p_att, p_rw, p_cv = _in_proj_call(x, mod, _bf(_pad_w_in(w_in[l])), cos_t, sin_t, tb_in, n_ctx, seq_l)
        att_c = _ctx_attn_call(p_att, att_sink[l], nb_c, seq_c)
        att_l = _lat_attn_call(p_att, ck_all, cv_all, att_sink[l], l, nb_l, seq_l, n_ctx)
        wts = _rwkv_weights(rw_w0[l], rw_w_up[l], rw_a0[l], rw_a_up[l], rw_g_up[l], rw_kk[l], rw_ka[l],
                            rw_rk[l], rw_lnx_g[l], rw_lnx_b[l])
        rw_c, sfin = _rwkv_call(p_rw, s0_zero, consts, wts, nb_c, seq_c, 0, rw_par, False)
        rw_l, _ = _rwkv_call(p_rw, s0_bd_all[:, l], consts, wts, nb_l, seq_l, n_ctx, rw_par, True)
        x1 = _out_proj_call(att_c, att_l, rw_c, rw_l, p_cv, x, mod, _bf(w_out[l]), conv_w[l],
                            ln1_g[l], ln1_b[l], tb_out, n_ctx, seq_c, seq_l)
        sk = peer_subkeys[l].reshape(2 * PEER_HEADS, PEER_NKEYS, PEER_HALF)
        vt = _bf(peer_v[l]).reshape(-1, SUBLANES * PEER_NKEYS, d).transpose(0, 2, 1)
        x = _peer_call(x1, mod, _bf(peer_wq[l].T), sk, _bf(peer_u[l]), vt,
                       ln2_g[l], ln2_b[l], tb_peer, n_ctx, seq_l)
        ks.append(p_att[:n_ctx, ATT_DIM:ATT_DIM + KV_DIM].reshape(nb_c, seq_c, ATT_KV_HEADS, HEAD_DIM))
        vs.append(p_att[:n_ctx, ATT_DIM + KV_DIM:].reshape(nb_c, seq_c, ATT_KV_HEADS, HEAD_DIM))
        s5 = sfin.reshape(nb_c, 2, RWKV_HEADS, HEAD_DIM, RWKV_HEADS, HEAD_DIM)
        sts.append(jnp.stack([s5[:, :, h, :, h, :] for h in range(RWKV_HEADS)], axis=2))

    y_prompt = x[:n_ctx].reshape(nb_c, seq_c, d)
    y_sample = x[n_ctx:].reshape(nb_l, seq_l, d)
    return (y_prompt, y_sample, jnp.stack(ks, axis=1), jnp.stack(vs, axis=1), jnp.stack(sts, axis=1))


def kernel(x_prompt, x_sample, cache_k, cache_v, state_rwkv, c, c_ctx, ln_in_g, ln_in_b, w_ada, b_ada, w_in, w_out, att_sink, rw_w0, rw_w_up, rw_a0, rw_a_up, rw_g_up, rw_kk, rw_ka, rw_rk, rw_lnx_g, rw_lnx_b, conv_w, ln1_g, ln1_b, ln2_g, ln2_b, peer_wq, peer_subkeys, peer_u, peer_v):
    return _forward(x_prompt, x_sample, cache_k, cache_v, state_rwkv, c, c_ctx, ln_in_g, ln_in_b, w_ada, b_ada,
                    w_in, w_out, att_sink, rw_w0, rw_w_up, rw_a0, rw_a_up, rw_g_up, rw_kk, rw_ka, rw_rk,
                    rw_lnx_g, rw_lnx_b, conv_w, ln1_g, ln1_b, ln2_g, ln2_b, peer_wq, peer_subkeys, peer_u, peer_v)
```

```python
import functools
import math

import jax
import jax.numpy as jnp
import numpy as np
from jax import lax
from jax.experimental import pallas as pl
from jax.experimental.pallas import tpu as pltpu

D_MODEL = 1024
DEPTH = 4
GRID_W = 64
HEAD_DIM = 64
ATT_DIM = 512
RWKV_DIM = 256
CONV_DIM = 256
ATT_HEADS = 8
ATT_KV_HEADS = 2
ATT_GROUP = 4
KV_DIM = 128
WINDOW = 128
ATT_BLOCK = 128
ROPE_THETA = 10000.0
RWKV_HEADS = 4
W_RANK = 32
A_RANK = 32
G_RANK = 64
RWKV_LNX_EPS = 64e-5
PEER_HEADS = 8
PEER_NKEYS = 128
PEER_TOPK = 16
PEER_HALF = 128
LN_EPS = 1e-5
ALPHA = (2 * DEPTH) ** 0.25
NEG = -1e30

LANES = 128
SUBLANES = 8
VMEM_LIMIT = 56 * 1024 * 1024

QK_W = ATT_DIM + KV_DIM
ATT_W = ATT_DIM + 2 * KV_DIM
RW_W = 1024
CV_W = 3 * CONV_DIM
IN_PAD = ATT_W + RW_W + CV_W
RW_LOW = 3 * RWKV_DIM

RW_CHUNK = 64
PEER_SCORE_PASSES = 3
PEER_ROUTE_TILES = 4
PEER_GATE_ROWS = 32
HIGHEST = lax.Precision.HIGHEST
NN = (((1,), (0,)), ((), ()))
NT = (((1,), (1,)), ((), ()))
TN = (((0,), (0,)), ((), ()))


def _mm(a, b, dims=NN, precision=None):
    return lax.dot_general(a, b, dims, precision=precision, preferred_element_type=jnp.float32)


def _bf(x):
    return x.astype(jnp.bfloat16)


def _split(x):
    hi = _bf(x)
    return hi, _bf(x - hi.astype(jnp.float32))


def _mm_split(a, b, dims, passes):
    out = _mm(a[0], b[0], dims)
    if passes == 3:
        out = out + (_mm(a[0], b[1], dims) + _mm(a[1], b[0], dims))
    return out


def _mm_terms(x, exact, terms, x_is_rhs=False):
    out = None
    rest = x
    for _ in range(terms):
        piece = _bf(rest)
        rest = rest - piece.astype(jnp.float32)
        part = _mm(exact, piece) if x_is_rhs else _mm(piece, exact)
        out = part if out is None else out + part
    return out


RW_PASSES = {"gram_n": 1, "gram": 1, "inv": 1, "loc": 1, "seq": 1}


def _layer_norm(x, g, b, eps):
    mu = jnp.mean(x, axis=-1, keepdims=True)
    xc = x - mu
    var = jnp.mean(xc * xc, axis=-1, keepdims=True)
    return xc * lax.rsqrt(var + eps) * g + b


def _params(*sem):
    return pltpu.CompilerParams(dimension_semantics=sem, vmem_limit_bytes=VMEM_LIMIT)


def _full(shape):
    nd = len(shape)
    return pl.BlockSpec(shape, lambda *_: (0,) * nd)


def _mod_kernel(c_ref, w_ref, b_ref, o_ref):
    cv = c_ref[...]
    s = cv * jax.nn.sigmoid(cv)
    o_ref[...] = _mm(s, w_ref[...], precision=HIGHEST) + b_ref[...]


def _mod_call(cvec, w_ada, b_ada):
    depth, d, n6 = w_ada.shape
    rows = cvec.shape[0]
    nt = n6 // d
    return pl.pallas_call(
        _mod_kernel,
        grid=(depth, nt),
        in_specs=[
            pl.BlockSpec((rows, d), lambda l, n: (0, 0)),
            pl.BlockSpec((None, d, d), lambda l, n: (l, 0, n)),
            pl.BlockSpec((None, 1, d), lambda l, n: (l, 0, n)),
        ],
        out_specs=pl.BlockSpec((None, rows, d), lambda l, n: (l, 0, n)),
        out_shape=jax.ShapeDtypeStruct((depth, rows, n6), jnp.float32),
        compiler_params=_params("parallel", "parallel"),
        name="adaln_mod",
    )(cvec, w_ada, b_ada.reshape(depth, 1, n6))


def _ln_kernel(x_ref, g_ref, b_ref, o_ref):
    o_ref[...] = _layer_norm(x_ref[...], g_ref[...], b_ref[...], LN_EPS)


def _ln_call(x, g, b, tb):
    rows, d = x.shape
    return pl.pallas_call(
        _ln_kernel,
        grid=(rows // tb,),
        in_specs=[pl.BlockSpec((tb, d), lambda i: (i, 0)), _full((1, d)), _full((1, d))],
        out_specs=pl.BlockSpec((tb, d), lambda i: (i, 0)),
        out_shape=jax.ShapeDtypeStruct((rows, d), jnp.float32),
        compiler_params=_params("parallel"),
        name="ln_in",
    )(x, g.reshape(1, d), b.reshape(1, d))


def _group_of_block(i, n_ctx_blocks, blocks_per_lat_seq):
    return jnp.where(i < n_ctx_blocks, 0, 1 + (i - n_ctx_blocks) // blocks_per_lat_seq)


def _in_proj_kernel(x_ref, mod_ref, w_ref, cos_ref, sin_ref, att_ref, rw_ref, cv_ref):
    x = x_ref[...]
    h = x * (1.0 + mod_ref[1:2, :]) + mod_ref[0:1, :]
    p = _mm(_bf(h), w_ref[...])
    qk = p[:, :QK_W]
    lane = lax.broadcasted_iota(jnp.int32, qk.shape, 1)
    partner = jnp.where((lane % 32) < 16,
                        pltpu.roll(qk, QK_W - 16, 1),
                        pltpu.roll(qk, 16, 1))
    att_ref[:, :QK_W] = qk * cos_ref[...] + partner * sin_ref[...]
    att_ref[:, QK_W:] = p[:, QK_W:ATT_W]
    rw_ref[...] = p[:, ATT_W:ATT_W + RW_W]
    cv_ref[...] = p[:, ATT_W + RW_W:]


def _in_proj_call(x, mod, w_in_p, cos_t, sin_t, tb, n_ctx_rows, lat_seq):
    rows, d = x.shape
    ncb = n_ctx_rows // tb
    bps = lat_seq // tb

    def mod_map(i):
        return (_group_of_block(i, ncb, bps), 0, 0)

    def rope_map(i):
        return (jnp.where(i < ncb, bps, (i - ncb) % bps), 0)

    return pl.pallas_call(
        _in_proj_kernel,
        grid=(rows // tb,),
        in_specs=[
            pl.BlockSpec((tb, d), lambda i: (i, 0)),
            pl.BlockSpec((None, 6, d), mod_map),
            _full((d, IN_PAD)),
            pl.BlockSpec((tb, QK_W), rope_map),
            pl.BlockSpec((tb, QK_W), rope_map),
        ],
        out_specs=[
            pl.BlockSpec((tb, ATT_W), lambda i: (i, 0)),
            pl.BlockSpec((tb, RW_W), lambda i: (i, 0)),
            pl.BlockSpec((tb, CV_W), lambda i: (i, 0)),
        ],
        out_shape=[
            jax.ShapeDtypeStruct((rows, ATT_W), jnp.float32),
            jax.ShapeDtypeStruct((rows, RW_W), jnp.float32),
            jax.ShapeDtypeStruct((rows, CV_W), jnp.float32),
        ],
        compiler_params=_params("parallel"),
        name="in_proj",
    )(x, mod, w_in_p, cos_t, sin_t)


def _ctx_attn_kernel(sink_ref, p_ref, o_ref):
    scale = HEAD_DIM ** -0.5
    for kv in range(ATT_KV_HEADS):
        k = _bf(p_ref[:, ATT_DIM + kv * HEAD_DIM:ATT_DIM + (kv + 1) * HEAD_DIM])
        v = _bf(p_ref[:, ATT_DIM + KV_DIM + kv * HEAD_DIM:ATT_DIM + KV_DIM + (kv + 1) * HEAD_DIM])
        for g in range(ATT_GROUP):
            hd = kv * ATT_GROUP + g
            q = _bf(p_ref[:, hd * HEAD_DIM:(hd + 1) * HEAD_DIM])
            s = _mm(q, k, NT) * scale
            sink = sink_ref[hd]
            m = jnp.maximum(jnp.max(s, axis=-1, keepdims=True), sink)
            e = jnp.exp(s - m)
            den = jnp.sum(e, axis=-1, keepdims=True) + jnp.exp(sink - m)
            o_ref[:, hd * HEAD_DIM:(hd + 1) * HEAD_DIM] = _mm(_bf(e), v) / den


def _ctx_attn_call(p_att, sink, n_seq, seq):
    return pl.pallas_call(
        _ctx_attn_kernel,
        grid=(n_seq,),
        in_specs=[
            pl.BlockSpec(memory_space=pltpu.SMEM),
            pl.BlockSpec((seq, ATT_W), lambda b: (b, 0)),
        ],
        out_specs=pl.BlockSpec((seq, ATT_DIM), lambda b: (b, 0)),
        out_shape=jax.ShapeDtypeStruct((n_seq * seq, ATT_DIM), jnp.float32),
        compiler_params=_params("parallel"),
        name="ctx_attn",
    )(sink, p_att)


def _lat_attn_kernel(sink_ref, own_ref, prev_ref, next_ref, ck_ref, cv_ref, o_ref, *, n_blocks):
    n = pl.program_id(1)
    scale = HEAD_DIM ** -0.5
    qi = lax.broadcasted_iota(jnp.int32, (ATT_BLOCK, 3 * ATT_BLOCK), 0)
    kj = lax.broadcasted_iota(jnp.int32, (ATT_BLOCK, 3 * ATT_BLOCK), 1)
    rel = kj - ATT_BLOCK - qi
    kpos = (n - 1) * ATT_BLOCK + kj
    mask = (jnp.abs(rel) <= WINDOW) & (kpos >= 0) & (kpos < n_blocks * ATT_BLOCK)
    mask4 = jnp.concatenate([mask] * ATT_GROUP, axis=0)
    row_group = lax.broadcasted_iota(jnp.int32, (ATT_GROUP * ATT_BLOCK, 1), 0) // ATT_BLOCK
    for kv in range(ATT_KV_HEADS):
        ks = slice(ATT_DIM + kv * HEAD_DIM, ATT_DIM + (kv + 1) * HEAD_DIM)
        vs = slice(ATT_DIM + KV_DIM + kv * HEAD_DIM, ATT_DIM + KV_DIM + (kv + 1) * HEAD_DIM)
        kw = _bf(jnp.concatenate([prev_ref[:, ks], own_ref[:, ks], next_ref[:, ks]], axis=0))
        vw = _bf(jnp.concatenate([prev_ref[:, vs], own_ref[:, vs], next_ref[:, vs]], axis=0))
        ck = _bf(ck_ref[:, kv * HEAD_DIM:(kv + 1) * HEAD_DIM])
        cv = _bf(cv_ref[:, kv * HEAD_DIM:(kv + 1) * HEAD_DIM])
        q4 = _bf(jnp.concatenate(
            [own_ref[:, (kv * ATT_GROUP + g) * HEAD_DIM:(kv * ATT_GROUP + g + 1) * HEAD_DIM]
             for g in range(ATT_GROUP)], axis=0))
        s_ctx = _mm(q4, ck, NT) * scale
        s_win = jnp.where(mask4, _mm(q4, kw, NT) * scale, NEG)
        sink = jnp.full((ATT_GROUP * ATT_BLOCK, 1), sink_ref[kv * ATT_GROUP], jnp.float32)
        for g in range(1, ATT_GROUP):
            sink = jnp.where(row_group == g, sink_ref[kv * ATT_GROUP + g], sink)
        m = jnp.maximum(jnp.maximum(jnp.max(s_ctx, axis=-1, keepdims=True),
                                    jnp.max(s_win, axis=-1, keepdims=True)), sink)
        e_ctx = jnp.exp(s_ctx - m)
        e_win = jnp.exp(s_win - m)
        den = (jnp.sum(e_ctx, axis=-1, keepdims=True) + jnp.sum(e_win, axis=-1, keepdims=True)
               + jnp.exp(sink - m))
        o4 = (_mm(_bf(e_ctx), cv) + _mm(_bf(e_win), vw)) / den
        for g in range(ATT_GROUP):
            hd = kv * ATT_GROUP + g
            o_ref[:, hd * HEAD_DIM:(hd + 1) * HEAD_DIM] = o4[g * ATT_BLOCK:(g + 1) * ATT_BLOCK]


def _lat_attn_call(p_att, ck, cv, sink, layer, n_seq, seq, row_off):
    nb = seq // ATT_BLOCK
    off = row_off // ATT_BLOCK
    past = ck.shape[2]

    def own(b, n):
        return (off + b * nb + n, 0)

    def prev(b, n):
        return (off + b * nb + jnp.maximum(n - 1, 0), 0)

    def nxt(b, n):
        return (off + b * nb + jnp.minimum(n + 1, nb - 1), 0)

    return pl.pallas_call(
        functools.partial(_lat_attn_kernel, n_blocks=nb),
        grid=(n_seq, nb),
        in_specs=[
            pl.BlockSpec(memory_space=pltpu.SMEM),
            pl.BlockSpec((ATT_BLOCK, ATT_W), own),
            pl.BlockSpec((ATT_BLOCK, ATT_W), prev),
            pl.BlockSpec((ATT_BLOCK, ATT_W), nxt),
            pl.BlockSpec((None, None, past, KV_DIM), lambda b, n: (b, layer, 0, 0)),
            pl.BlockSpec((None, None, past, KV_DIM), lambda b, n: (b, layer, 0, 0)),
        ],
        out_specs=pl.BlockSpec((ATT_BLOCK, ATT_DIM), lambda b, n: (b * nb + n, 0)),
        out_shape=jax.ShapeDtypeStruct((n_seq * seq, ATT_DIM), jnp.float32),
        compiler_params=_params("parallel", "parallel"),
        name="lat_attn",
    )(sink, p_att, p_att, p_att, ck, cv)


def _rwkv_consts():
    L, H, W = RW_CHUNK, RWKV_HEADS, RWKV_DIM
    t = np.arange(L)
    inc = np.stack([(t[None, :] <= t[:, None]), (t[None, :] >= t[:, None])]).astype(np.float32)
    stc = np.stack([(t[None, :] < t[:, None]), (t[None, :] > t[:, None])]).astype(np.float32)
    eye_h = np.eye(H, dtype=np.float32)
    inc_bd = np.stack([np.kron(eye_h, inc[d]) for d in range(2)])
    stc_bd = np.stack([np.kron(eye_h, stc[d]) for d in range(2)])
    blk = np.kron(eye_h, np.ones((L, HEAD_DIM), np.float32))
    eye = np.eye(W, dtype=np.float32)
    return inc, inc_bd, stc_bd, blk, eye


def _rwkv_kernel(p_ref, s0_ref, inc_ref, incbd_ref, stcbd_ref, blk_ref, eye_ref,
                 w0_ref, wup_ref, a0_ref, aup_ref, gup_ref, kkw_ref, kaw_ref, rkw_ref, lng_ref, lnb_ref,
                 o_ref, sfin_ref, y_scr, bon_scr, *, seq, n_par):
    L = RW_CHUNK
    nc = seq // L
    blk = _bf(blk_ref[...])
    ones_bd = blk
    eye = eye_ref[...]
    y_scr[...] = jnp.zeros_like(y_scr)
    bon_scr[...] = jnp.zeros_like(bon_scr)

    def stack(x):
        return tuple(jnp.concatenate([z] * RWKV_HEADS, axis=0) * blk for z in _split(x))

    def chunk(idx, r0, s_in, s_out):
        d = idx % 2
        x = p_ref[pl.ds(r0, L), :]
        rr = x[:, 0:RWKV_DIM]
        rk = x[:, RWKV_DIM:2 * RWKV_DIM]
        rv = x[:, 2 * RWKV_DIM:3 * RWKV_DIM]
        low = x[:, RW_LOW:RW_LOW + LANES]
        zw = w0_ref[d] + _mm_split(_split(jnp.tanh(low)), _split(wup_ref[d]), NN, 3)
        yield
        u = -zw
        softplus = jnp.maximum(u, 0.0) + jnp.log1p(jnp.exp(-jnp.abs(u)))
        lw = -jnp.exp(-softplus - 0.5)
        asig = jax.nn.sigmoid(a0_ref[d] + _mm_split(_split(low), _split(aup_ref[d]), NN, 3))
        yield
        kd = rk * (1.0 + (asig - 1.0) * kaw_ref[...])
        kkr = rk * kkw_ref[...]
        kk = kkr * lax.rsqrt(jnp.maximum(_mm_terms(kkr * kkr, ones_bd, 2), 1e-24))
        yield
        a_s = -kk
        b_s = kk * asig
        bon_scr[pl.ds(r0, L), :] += _mm_terms(rr * kd * rkw_ref[...], ones_bd, 2) * rv
        yield

        cl = _mm_terms(lw, _bf(inc_ref[d]), 3, x_is_rhs=True)
        yield
        cl_end = cl[L - 1:L, :] if d == 0 else cl[0:1, :]
        e_neg = jnp.exp(-cl)
        e_tail = jnp.exp(cl_end - cl)
        a_st = stack(a_s * jnp.exp(cl - lw))
        b_st = stack(b_s * e_neg)
        k_st = stack(kd * e_neg)
        r_st = stack(rr * jnp.exp(cl))
        v_st = stack(rv)
        bh_st = stack(b_s * e_tail)
        kh_st = stack(kd * e_tail)
        stc = stcbd_ref[d]
        inc = incbd_ref[d]
        n_m = stc * _mm_split(a_st, b_st, NT, RW_PASSES["gram_n"])
        yield
        m1 = stc * _mm_split(a_st, k_st, NT, RW_PASSES["gram"])
        yield
        fill = {}
        fillers = [
            lambda: fill.update(p=inc * _mm_split(r_st, b_st, NT, RW_PASSES["gram"])),
            lambda: fill.update(m1v=_mm_split(_split(m1), v_st, NN, RW_PASSES["loc"])),
            lambda: fill.update(q=inc * _mm_split(r_st, k_st, NT, RW_PASSES["gram"])),
            lambda: fill.update(vk=_mm_split(v_st, kh_st, TN, RW_PASSES["loc"])),
            lambda: fill.update(rs=_mm_split(r_st, _split(s_in), NT, RW_PASSES["seq"])),
        ]
        t_m = eye + n_m
        n_p = _split(n_m)
        for it in range(5):
            n_sq = _mm_split(n_p, n_p, NN, RW_PASSES["inv"])
            yield
            fillers[it]()
            yield
            n_p = _split(n_sq)
            t_m = t_m + _mm_split(_split(t_m), n_p, NN, RW_PASSES["inv"])
            yield
        t_s = _split(t_m)
        qv = _mm_split(_split(fill["q"]), v_st, NN, RW_PASSES["loc"])
        yield
        u_loc = _mm_split(t_s, _split(fill["m1v"]), NN, RW_PASSES["loc"])
        yield
        w_t = _mm_split(t_s, a_st, NN, RW_PASSES["loc"])
        yield
        u_m = _mm_split(_split(w_t), _split(s_in), NT, RW_PASSES["seq"]) + u_loc
        yield
        u_s = _split(u_m)
        y_bd = fill["rs"] + _mm_split(_split(fill["p"]), u_s, NN, RW_PASSES["seq"]) + qv
        y = y_bd[0:L] + y_bd[L:2 * L] + y_bd[2 * L:3 * L] + y_bd[3 * L:4 * L]
        y_scr[pl.ds(r0, L), :] += y
        yield
        s_out[idx] = s_in * jnp.exp(cl_end) + _mm_split(u_s, bh_st, TN, RW_PASSES["seq"]) + fill["vk"]

    def body(c, carry):
        s_out = {}
        stages = []
        for sq in range(n_par):
            stages.append(chunk(2 * sq, pl.multiple_of(sq * seq + c * L, L), carry[2 * sq], s_out))
            stages.append(chunk(2 * sq + 1, pl.multiple_of(sq * seq + (nc - 1 - c) * L, L), carry[2 * sq + 1],
                                s_out))
        while stages:
            for gen in list(stages):
                if next(gen, StopIteration) is StopIteration:
                    stages.remove(gen)
        return tuple(s_out[i] for i in range(2 * n_par))

    s_fin = lax.fori_loop(0, nc, body, tuple(s0_ref[i // 2, i % 2] for i in range(2 * n_par)))
    for i in range(2 * n_par):
        sfin_ref[i // 2, i % 2] = s_fin[i]

    inv_n = 1.0 / HEAD_DIM

    def epilogue(c, carry):
        r0 = pl.multiple_of(c * L, L)
        y = y_scr[pl.ds(r0, L), :]
        mu = _mm_terms(y, ones_bd, 3) * inv_n
        yc = y - mu
        var = _mm_terms(yc * yc, ones_bd, 2) * inv_n
        yn = yc * lax.rsqrt(var + RWKV_LNX_EPS) * lng_ref[...] + lnb_ref[...]
        gd = p_ref[pl.ds(r0, L), RW_LOW + LANES:RW_LOW + 2 * LANES]
        g = _mm_split(_split(jax.nn.sigmoid(gd)), _split(gup_ref[...]), NN, 3)
        bon = bon_scr[pl.ds(r0, L), :]
        o_ref[pl.ds(r0, L), :] = (yn + bon) * g
        return carry

    lax.fori_loop(0, n_par * nc, epilogue, 0)


def _rwkv_call(p_rw, s0_bd, consts, wts, n_seq, seq, row_off, n_par, single_buffer_input):
    inc, inc_bd, stc_bd, blk, eye = consts
    W = RWKV_DIM
    rows = n_par * seq
    off = row_off // rows
    p_mode = dict(pipeline_mode=pl.Buffered(1)) if single_buffer_input else {}
    in_specs = [
        pl.BlockSpec((rows, RW_W), lambda b: (off + b, 0), **p_mode),
        pl.BlockSpec((n_par, 2, W, W), lambda b: (b, 0, 0, 0)),
        _full(inc.shape), _full(inc_bd.shape), _full(stc_bd.shape), _full(blk.shape), _full(eye.shape),
    ] + [_full(w.shape) for w in wts]
    return pl.pallas_call(
        functools.partial(_rwkv_kernel, seq=seq, n_par=n_par),
        grid=(n_seq // n_par,),
        in_specs=in_specs,
        out_specs=[
            pl.BlockSpec((rows, W), lambda b: (b, 0)),
            pl.BlockSpec((n_par, 2, W, W), lambda b: (b, 0, 0, 0)),
        ],
        out_shape=[
            jax.ShapeDtypeStruct((n_seq * seq, W), jnp.float32),
            jax.ShapeDtypeStruct((n_seq, 2, W, W), jnp.float32),
        ],
        scratch_shapes=[pltpu.VMEM((rows, W), jnp.float32), pltpu.VMEM((rows, W), jnp.float32)],
        compiler_params=_params("parallel"),
        name="rwkv",
    )(p_rw, s0_bd, inc, inc_bd, stc_bd, blk, eye, *wts)


def _rwkv_weights(w0, w_up, a0, a_up, g_up, kk, ka, rk, lng, lnb):
    W = RWKV_DIM
    wup_p = jnp.zeros((2, LANES, W), jnp.float32)
    aup_p = jnp.zeros((2, LANES, W), jnp.float32)
    for d in range(2):
        wup_p = wup_p.at[d, d * W_RANK:(d + 1) * W_RANK].set(w_up[d])
        aup_p = aup_p.at[d, 2 * W_RANK + d * A_RANK:2 * W_RANK + (d + 1) * A_RANK].set(a_up[d])
    gup_p = jnp.zeros((LANES, W), jnp.float32).at[:G_RANK].set(g_up)
    row = lambda v: v.reshape(1, W)
    return (w0.reshape(2, 1, W), wup_p, a0.reshape(2, 1, W), aup_p, gup_p,
            row(kk), row(ka), row(rk), row(lng), row(lnb))


def _out_proj_kernel(attc_ref, attl_ref, rwc_ref, rwl_ref, cv_ref, cvp_ref, cvn_ref, x_ref, mod_ref,
                     w_ref, cw_ref, g_ref, b_ref, o_ref, *, tb, ncb, bps_c, bps):
    i = pl.program_id(0)
    is_ctx = i < ncb
    pos = jnp.where(is_ctx, i % bps_c, (i - ncb) % bps)
    last = jnp.where(is_ctx, bps_c - 1, bps - 1)
    att = jnp.where(is_ctx, attc_ref[...], attl_ref[...])
    rw = jnp.where(is_ctx, rwc_ref[...], rwl_ref[...])

    cb = cv_ref[:, 0:CONV_DIM]
    z = cv_ref[:, CONV_DIM:2 * CONV_DIM] * cv_ref[:, 2 * CONV_DIM:3 * CONV_DIM]
    halo_p = cvp_ref[SUBLANES - 1:SUBLANES, CONV_DIM:2 * CONV_DIM] * cvp_ref[SUBLANES - 1:SUBLANES, 2 * CONV_DIM:]
    halo_n = cvn_ref[0:1, CONV_DIM:2 * CONV_DIM] * cvn_ref[0:1, 2 * CONV_DIM:]
    halo_p = jnp.where(pos > 0, halo_p, 0.0)
    halo_n = jnp.where(pos < last, halo_n, 0.0)
    row = lax.broadcasted_iota(jnp.int32, z.shape, 0)
    z_prev = jnp.where(row == 0, halo_p, pltpu.roll(z, 1, 0))
    z_next = jnp.where(row == tb - 1, halo_n, pltpu.roll(z, tb - 1, 0))
    conv = cb * (cw_ref[0:1, :] * z_prev + cw_ref[1:2, :] * z + cw_ref[2:3, :] * z_next)

    mix = (_mm(_bf(att), w_ref[0:ATT_DIM, :])
           + _mm(_bf(rw), w_ref[ATT_DIM:ATT_DIM + RWKV_DIM, :])
           + _mm(_bf(conv), w_ref[ATT_DIM + RWKV_DIM:, :]))
    o_ref[...] = _layer_norm(ALPHA * x_ref[...] + mod_ref[2:3, :] * mix, g_ref[...], b_ref[...], LN_EPS)


def _out_proj_call(att_c, att_l, rw_c, rw_l, p_cv, x, mod, w_out_b, conv_w, ln_g, ln_b, tb, n_ctx_rows, ctx_seq,
                   lat_seq):
    rows, d = x.shape
    ncb = n_ctx_rows // tb
    bps = lat_seq // tb
    sub = tb // SUBLANES
    n_sub = rows // SUBLANES

    def ctx_map(i):
        return (jnp.minimum(i, ncb - 1), 0)

    def lat_map(i):
        return (jnp.maximum(i - ncb, 0), 0)

    return pl.pallas_call(
        functools.partial(_out_proj_kernel, tb=tb, ncb=ncb, bps_c=ctx_seq // tb, bps=bps),
        grid=(rows // tb,),
        in_specs=[
            pl.BlockSpec((tb, ATT_DIM), ctx_map),
            pl.BlockSpec((tb, ATT_DIM), lat_map),
            pl.BlockSpec((tb, RWKV_DIM), ctx_map),
            pl.BlockSpec((tb, RWKV_DIM), lat_map),
            pl.BlockSpec((tb, CV_W), lambda i: (i, 0)),
            pl.BlockSpec((SUBLANES, CV_W), lambda i: (jnp.maximum(i * sub - 1, 0), 0)),
            pl.BlockSpec((SUBLANES, CV_W), lambda i: (jnp.minimum((i + 1) * sub, n_sub - 1), 0)),
            pl.BlockSpec((tb, d), lambda i: (i, 0)),
            pl.BlockSpec((None, 6, d), lambda i: (_group_of_block(i, ncb, bps), 0, 0)),
            _full((d, d)),
            _full((3, CONV_DIM)),
            _full((1, d)),
            _full((1, d)),
        ],
        out_specs=pl.BlockSpec((tb, d), lambda i: (i, 0)),
        out_shape=jax.ShapeDtypeStruct((rows, d), jnp.float32),
        compiler_params=_params("parallel"),
        name="out_proj",
    )(att_c, att_l, rw_c, rw_l, p_cv, p_cv, p_cv, x, mod, w_out_b, conv_w, ln_g.reshape(1, d), ln_b.reshape(1, d))


def _gelu_tanh(x):
    return 0.5 * x * (1.0 + jnp.tanh(math.sqrt(2.0 / math.pi) * (x + 0.044715 * (x * x * x))))


def _extract_top(work, iota, n_out, want_rank=False, by_value=False):
    vals = []
    big = work.shape[0]
    rank = jnp.full(work.shape, float(big), jnp.float32) if want_rank else None
    for k in range(n_out):
        m = jnp.max(work, axis=0, keepdims=True)
        if by_value:
            sel = work == m
        else:
            idx = jnp.min(jnp.where(work == m, iota, big), axis=0, keepdims=True)
            sel = iota == idx
        work = jnp.where(sel, -jnp.inf, work)
        if want_rank:
            rank = jnp.where(sel, float(k), rank)
        vals.append(m)
        yield
    return vals, work, rank


def _lockstep(*gens):
    results = [None] * len(gens)
    live = list(range(len(gens)))
    while live:
        for i in list(live):
            try:
                next(gens[i])
            except StopIteration as stop:
                results[i] = stop.value
                live.remove(i)
        yield
    return results


def _run_lockstep(*gens):
    stepper = _lockstep(*gens)
    while True:
        try:
            next(stepper)
        except StopIteration as stop:
            return stop.value


def _peer_kernel(x_ref, mod_ref, wq_ref, sk_ref, u_ref, vt_ref, g_ref, b_ref, o_ref,
                 ht_scr, s0_scr, e0_scr, s1_scr, r1_scr, e1_scr, a_scr, z_scr, acc_scr, *, tb):
    e_step = pl.program_id(1)
    n_lt = tb // LANES
    K = PEER_TOPK

    @pl.when(e_step == 0)
    def _route():
        h2 = x_ref[...] * (1.0 + mod_ref[4:5, :]) + mod_ref[3:4, :]
        ht = _bf(h2.T)
        ht_scr[...] = ht
        qt = _mm(wq_ref[...], ht)
        for hp in range(2 * PEER_HEADS):
            sc = _mm_split(_split(sk_ref[hp]), _split(qt[hp * PEER_HALF:(hp + 1) * PEER_HALF, :]), NN,
                           PEER_SCORE_PASSES)
            if hp % 2 == 0:
                s0_scr[hp // 2] = sc
            else:
                s1_scr[hp // 2] = sc

        iota_k = lax.broadcasted_iota(jnp.int32, (PEER_NKEYS, LANES), 0)
        iota_c = lax.broadcasted_iota(jnp.int32, ((SUBLANES + 2) * SUBLANES, LANES), 0)

        def route_tile(tile, by_value):
            hd, lanes, sc0, sc1 = tile
            (v0, w0, _), (v1, w1, rank1) = yield from _lockstep(
                _extract_top(sc0, iota_k, K, by_value=by_value),
                _extract_top(sc1, iota_k, K, want_rank=True, by_value=by_value))
            cand0 = w0 != sc0
            cand1 = w1 != sc1
            lo1 = jnp.concatenate(v1[:SUBLANES], axis=0)
            hi1 = jnp.concatenate(v1[SUBLANES:], axis=0)
            hi0 = jnp.concatenate(v0[SUBLANES:], axis=0)
            pair = jnp.concatenate([v0[a] + lo1 for a in range(SUBLANES)] + [v0[0] + hi1, hi0 + v1[0]], axis=0)
            f, wp, _ = yield from _extract_top(pair, iota_c, K + 1, by_value=by_value)
            zsum = jnp.zeros_like(f[0])
            for k in range(K):
                zsum = zsum + jnp.exp(f[k] - f[0])
            th = 0.5 * (f[K - 1] + f[K]) - sc0
            count0 = jnp.zeros_like(sc0)
            for b in range(K):
                count0 = count0 + jnp.where(v1[b] > th, 1.0, 0.0)
            s0_scr[hd, :, lanes] = jnp.where(cand0, count0, 0.0)
            e0_scr[hd, :, lanes] = jnp.where(cand0, jnp.exp(sc0 - v0[0]), 0.0) / zsum
            e1_scr[hd, :, lanes] = jnp.where(cand1, jnp.exp(sc1 - v1[0]), 0.0)
            r1_scr[hd, :, lanes] = rank1
            if not by_value:
                return None
            removed = [jnp.sum(jnp.where(c, 1.0, 0.0), axis=0, keepdims=True) for c in (cand0, cand1, wp != pair)]
            wrong = (removed[0] != K) | (removed[1] != K) | (removed[2] != K + 1)
            return jnp.max(jnp.where(wrong, 1.0, 0.0)) > 0.0

        def per_tile_group(j, carry):
            tiles = []
            for t in [PEER_ROUTE_TILES * j + s for s in range(PEER_ROUTE_TILES)]:
                hd = t // n_lt
                lanes = pl.ds(pl.multiple_of((t % n_lt) * LANES, LANES), LANES)
                tiles.append((hd, lanes, s0_scr[hd, :, lanes], s1_scr[hd, :, lanes]))
            had_ties = _run_lockstep(*[route_tile(tile, by_value=True) for tile in tiles])
            for tile, redo in zip(tiles, had_ties):
                @pl.when(redo)
                def _exact(tile=tile):
                    _run_lockstep(route_tile(tile, by_value=False))
            return carry

        lax.fori_loop(0, PEER_HEADS * n_lt // PEER_ROUTE_TILES, per_tile_group, 0)
        acc_scr[...] = jnp.zeros_like(acc_scr)

    a_scr[...] = _gelu_tanh(_mm(u_ref[...], ht_scr[...]))

    keys = pl.ds(pl.multiple_of(e_step * SUBLANES, SUBLANES), SUBLANES)
    group = SUBLANES // 2
    jblk = PEER_GATE_ROWS
    tile3 = (jblk // SUBLANES, SUBLANES, LANES)
    for ln in range(n_lt):
        lanes = slice(ln * LANES, (ln + 1) * LANES)
        c0 = [s0_scr[hd, keys, lanes] for hd in range(PEER_HEADS)]
        e0 = [e0_scr[hd, keys, lanes] for hd in range(PEER_HEADS)]
        for jb in range(PEER_NKEYS // jblk):
            jrows = slice(jb * jblk, (jb + 1) * jblk)
            for ig in range(SUBLANES // group):
                g = [jnp.zeros(tile3, jnp.float32) for _ in range(group)]
                for hd in range(PEER_HEADS):
                    r1 = r1_scr[hd, jrows, lanes].reshape(tile3)
                    e1 = e1_scr[hd, jrows, lanes].reshape(tile3)
                    for k in range(group):
                        ii = ig * group + k
                        cnt = jnp.broadcast_to(c0[hd][ii:ii + 1, :], tile3[1:])[None]
                        w0 = jnp.broadcast_to(e0[hd][ii:ii + 1, :], tile3[1:])[None]
                        g[k] = g[k] + jnp.where(r1 < cnt, e1, 0.0) * w0
                for k in range(group):
                    r0 = (ig * group + k) * PEER_NKEYS + jb * jblk
                    z_scr[r0:r0 + jblk, lanes] = _bf(g[k].reshape(jblk, LANES) * a_scr[r0:r0 + jblk, lanes])

    acc_scr[...] += _mm(vt_ref[...], z_scr[...])

    @pl.when(e_step == pl.num_programs(1) - 1)
    def _finish():
        out = acc_scr[...].T
        o_ref[...] = _layer_norm(ALPHA * x_ref[...] + mod_ref[5:6, :] * out, g_ref[...], b_ref[...], LN_EPS)


def _peer_call(x, mod, wq_t, subkeys, u_b, vt_b, ln_g, ln_b, tb, n_ctx_rows, lat_seq):
    rows, d = x.shape
    et = SUBLANES * PEER_NKEYS
    ncb = n_ctx_rows // tb
    bps = lat_seq // tb
    nq = wq_t.shape[0]
    f32 = jnp.float32
    return pl.pallas_call(
        functools.partial(_peer_kernel, tb=tb),
        grid=(rows // tb, u_b.shape[0] // et),
        in_specs=[
            pl.BlockSpec((tb, d), lambda i, e: (i, 0)),
            pl.BlockSpec((None, 6, d), lambda i, e: (_group_of_block(i, ncb, bps), 0, 0)),
            pl.BlockSpec((nq, d), lambda i, e: (0, 0), pipeline_mode=pl.Buffered(1)),
            pl.BlockSpec(subkeys.shape, lambda i, e: (0, 0, 0), pipeline_mode=pl.Buffered(1)),
            pl.BlockSpec((et, d), lambda i, e: (e, 0)),
            pl.BlockSpec((None, d, et), lambda i, e: (e, 0, 0)),
            pl.BlockSpec((1, d), lambda i, e: (0, 0)),
            pl.BlockSpec((1, d), lambda i, e: (0, 0)),
        ],
        out_specs=pl.BlockSpec((tb, d), lambda i, e: (i, 0)),
        out_shape=jax.ShapeDtypeStruct((rows, d), f32),
        scratch_shapes=[
            pltpu.VMEM((d, tb), jnp.bfloat16),
            pltpu.VMEM((PEER_HEADS, PEER_NKEYS, tb), f32),
            pltpu.VMEM((PEER_HEADS, PEER_NKEYS, tb), f32),
            pltpu.VMEM((PEER_HEADS, PEER_NKEYS, tb), f32),
            pltpu.VMEM((PEER_HEADS, PEER_NKEYS, tb), f32),
            pltpu.VMEM((PEER_HEADS, PEER_NKEYS, tb), f32),
            pltpu.VMEM((et, tb), f32),
            pltpu.VMEM((et, tb), jnp.bfloat16),
            pltpu.VMEM((d, tb), f32),
        ],
        compiler_params=_params("parallel", "arbitrary"),
        name="peer",
    )(x, mod, wq_t, subkeys, u_b, vt_b, ln_g.reshape(1, d), ln_b.reshape(1, d))


def _rope_tables(seq, tb):
    half = HEAD_DIM // 4
    freqs = ROPE_THETA ** (-jnp.arange(half, dtype=jnp.float32) / half)
    t = jnp.arange(seq)
    cos_parts, sin_parts = [], []
    for pos in (t // GRID_W, t % GRID_W):
        ang = pos.astype(jnp.float32)[:, None] * freqs[None, :]
        c, s = jnp.cos(ang), jnp.sin(ang)
        cos_parts += [c, c]
        sin_parts += [-s, s]
    cos_h = jnp.concatenate(cos_parts, axis=1)
    sin_h = jnp.concatenate(sin_parts, axis=1)
    n_rot = QK_W // HEAD_DIM
    cos_t = jnp.concatenate([jnp.tile(cos_h, (1, n_rot)), jnp.ones((tb, QK_W), jnp.float32)], axis=0)
    sin_t = jnp.concatenate([jnp.tile(sin_h, (1, n_rot)), jnp.zeros((tb, QK_W), jnp.float32)], axis=0)
    return cos_t, sin_t


def _pad_w_in(w):
    split = w.shape[1] - CV_W
    z = jnp.zeros((w.shape[0], IN_PAD - w.shape[1]), w.dtype)
    return jnp.concatenate([w[:, :split], z, w[:, split:]], axis=1)


def _forward(x_prompt, x_sample, cache_k, cache_v, state_rwkv, c, c_ctx, ln_in_g, ln_in_b, w_ada, b_ada,
             w_in, w_out, att_sink, rw_w0, rw_w_up, rw_a0, rw_a_up, rw_g_up, rw_kk, rw_ka, rw_rk,
             rw_lnx_g, rw_lnx_b, conv_w, ln1_g, ln1_b, ln2_g, ln2_b, peer_wq, peer_subkeys, peer_u, peer_v,
             tb_in=512, tb_out=256, tb_peer=512, rw_par=2):
    nb_c, seq_c, d = x_prompt.shape
    nb_l, seq_l, _ = x_sample.shape
    depth = w_in.shape[0]
    n_ctx = nb_c * seq_c
    past = cache_k.shape[2]
    W = RWKV_DIM

    x = jnp.concatenate([x_prompt.reshape(n_ctx, d), x_sample.reshape(nb_l * seq_l, d)], axis=0)
    x = _ln_call(x, ln_in_g, ln_in_b, tb_in)

    n_groups = 1 + nb_l
    g_pad = -(-n_groups // SUBLANES) * SUBLANES
    cvec = jnp.concatenate([c_ctx[None], c, jnp.zeros((g_pad - n_groups, d), jnp.float32)], axis=0)
    mod_all = _mod_call(cvec, w_ada, b_ada).reshape(depth, g_pad, 6, d)

    cos_t, sin_t = _rope_tables(seq_l, tb_in)
    consts = tuple(jnp.asarray(a) for a in _rwkv_consts())
    ck_all = cache_k.reshape(nb_l, depth, past, KV_DIM)
    cv_all = cache_v.reshape(nb_l, depth, past, KV_DIM)
    eye_h = jnp.eye(RWKV_HEADS, dtype=jnp.float32)
    s0_bd_all = jnp.einsum('bldhij,hg->bldhigj', state_rwkv, eye_h).reshape(nb_l, depth, 2, W, W)
    s0_zero = jnp.zeros((nb_c, 2, W, W), jnp.float32)

    ks, vs, sts = [], [], []
    for l in range(depth):
        mod = mod_all[l]
        p_att, p_rw, p_cv = _in_proj_call(x, mod, _bf(_pad_w_in(w_in[l])), cos_t, sin_t, tb_in, n_ctx, seq_l)
        att_c = _ctx_attn_call(p_att, att_sink[l], nb_c, seq_c)
        att_l = _lat_attn_call(p_att, ck_all, cv_all, att_sink[l], l, nb_l, seq_l, n_ctx)
        wts = _rwkv_weights(rw_w0[l], rw_w_up[l], rw_a0[l], rw_a_up[l], rw_g_up[l], rw_kk[l], rw_ka[l],
                            rw_rk[l], rw_lnx_g[l], rw_lnx_b[l])
        rw_c, sfin = _rwkv_call(p_rw, s0_zero, consts, wts, nb_c, seq_c, 0, rw_par, False)
        rw_l, _ = _rwkv_call(p_rw, s0_bd_all[:, l], consts, wts, nb_l, seq_l, n_ctx, rw_par, True)
        x1 = _out_proj_call(att_c, att_l, rw_c, rw_l, p_cv, x, mod, _bf(w_out[l]), conv_w[l],
                            ln1_g[l], ln1_b[l], tb_out, n_ctx, seq_c, seq_l)
        sk = peer_subkeys[l].reshape(2 * PEER_HEADS, PEER_NKEYS, PEER_HALF)
        vt = _bf(peer_v[l]).reshape(-1, SUBLANES * PEER_NKEYS, d).transpose(0, 2, 1)
        x = _peer_call(x1, mod, _bf(peer_wq[l].T), sk, _bf(peer_u[l]), vt,
                       ln2_g[l], ln2_b[l], tb_peer, n_ctx, seq_l)
        ks.append(p_att[:n_ctx, ATT_DIM:ATT_DIM + KV_DIM].reshape(nb_c, seq_c, ATT_KV_HEADS, HEAD_DIM))
        vs.append(p_att[:n_ctx, ATT_DIM + KV_DIM:].reshape(nb_c, seq_c, ATT_KV_HEADS, HEAD_DIM))
        s5 = sfin.reshape(nb_c, 2, RWKV_HEADS, HEAD_DIM, RWKV_HEADS, HEAD_DIM)
        sts.append(jnp.stack([s5[:, :, h, :, h, :] for h in range(RWKV_HEADS)], axis=2))

    y_prompt = x[:n_ctx].reshape(nb_c, seq_c, d)
    y_sample = x[n_ctx:].reshape(nb_l, seq_l, d)
    return (y_prompt, y_sample, jnp.stack(ks, axis=1), jnp.stack(vs, axis=1), jnp.stack(sts, axis=1))


def kernel(x_prompt, x_sample, cache_k, cache_v, state_rwkv, c, c_ctx, ln_in_g, ln_in_b, w_ada, b_ada, w_in, w_out, att_sink, rw_w0, rw_w_up, rw_a0, rw_a_up, rw_g_up, rw_kk, rw_ka, rw_rk, rw_lnx_g, rw_lnx_b, conv_w, ln1_g, ln1_b, ln2_g, ln2_b, peer_wq, peer_subkeys, peer_u, peer_v):
    return _forward(x_prompt, x_sample, cache_k, cache_v, state_rwkv, c, c_ctx, ln_in_g, ln_in_b, w_ada, b_ada,
                    w_in, w_out, att_sink, rw_w0, rw_w_up, rw_a0, rw_a_up, rw_g_up, rw_kk, rw_ka, rw_rk,
                    rw_lnx_g, rw_lnx_b, conv_w, ln1_g, ln1_b, ln2_g, ln2_b, peer_wq, peer_subkeys, peer_u, peer_v)
```

```python
import functools
import math

import jax
import jax.numpy as jnp
import numpy as np
from jax import lax
from jax.experimental import pallas as pl
from jax.experimental.pallas import tpu as pltpu

D_MODEL = 1024
DEPTH = 4
GRID_W = 64
HEAD_DIM = 64
ATT_DIM = 512
RWKV_DIM = 256
CONV_DIM = 256
ATT_HEADS = 8
ATT_KV_HEADS = 2
ATT_GROUP = 4
KV_DIM = 128
WINDOW = 128
ATT_BLOCK = 128
ROPE_THETA = 10000.0
RWKV_HEADS = 4
W_RANK = 32
A_RANK = 32
G_RANK = 64
RWKV_LNX_EPS = 64e-5
PEER_HEADS = 8
PEER_NKEYS = 128
PEER_TOPK = 16
PEER_HALF = 128
LN_EPS = 1e-5
ALPHA = (2 * DEPTH) ** 0.25
NEG = -1e30

LANES = 128
SUBLANES = 8
VMEM_LIMIT = 60 * 1024 * 1024

QK_W = ATT_DIM + KV_DIM
ATT_W = ATT_DIM + 2 * KV_DIM
RW_W = 1024
CV_W = 3 * CONV_DIM
IN_PAD = ATT_W + RW_W + CV_W
RW_LOW = 3 * RWKV_DIM

RW_CHUNK = 64
PEER_SCORE_PASSES = 3
PEER_ROUTE_TILES = 4
PEER_GATE_ROWS = 32
PEER_GROUPS_PER_STEP = 2
HIGHEST = lax.Precision.HIGHEST
NN = (((1,), (0,)), ((), ()))
NT = (((1,), (1,)), ((), ()))
TN = (((0,), (0,)), ((), ()))


def _mm(a, b, dims=NN, precision=None):
    return lax.dot_general(a, b, dims, precision=precision, preferred_element_type=jnp.float32)


def _bf(x):
    return x.astype(jnp.bfloat16)


def _split(x):
    hi = _bf(x)
    return hi, _bf(x - hi.astype(jnp.float32))


def _mm_split(a, b, dims, passes):
    out = _mm(a[0], b[0], dims)
    if passes == 3:
        out = out + (_mm(a[0], b[1], dims) + _mm(a[1], b[0], dims))
    return out


def _mm_terms(x, exact, terms, x_is_rhs=False):
    out = None
    rest = x
    for _ in range(terms):
        piece = _bf(rest)
        rest = rest - piece.astype(jnp.float32)
        part = _mm(exact, piece) if x_is_rhs else _mm(piece, exact)
        out = part if out is None else out + part
    return out


RW_PASSES = {"gram_n": 1, "gram": 1, "inv": 1, "loc": 1, "seq": 1}


def _layer_norm(x, g, b, eps):
    mu = jnp.mean(x, axis=-1, keepdims=True)
    xc = x - mu
    var = jnp.mean(xc * xc, axis=-1, keepdims=True)
    return xc * lax.rsqrt(var + eps) * g + b


def _params(*sem):
    return pltpu.CompilerParams(dimension_semantics=sem, vmem_limit_bytes=VMEM_LIMIT)


def _full(shape):
    nd = len(shape)
    return pl.BlockSpec(shape, lambda *_: (0,) * nd)


def _mod_kernel(c_ref, w_ref, b_ref, o_ref):
    cv = c_ref[...]
    s = cv * jax.nn.sigmoid(cv)
    o_ref[...] = _mm(s, w_ref[...], precision=HIGHEST) + b_ref[...]


def _mod_call(cvec, w_ada, b_ada):
    depth, d, n6 = w_ada.shape
    rows = cvec.shape[0]
    nt = n6 // d
    return pl.pallas_call(
        _mod_kernel,
        grid=(depth, nt),
        in_specs=[
            pl.BlockSpec((rows, d), lambda l, n: (0, 0)),
            pl.BlockSpec((None, d, d), lambda l, n: (l, 0, n)),
            pl.BlockSpec((None, 1, d), lambda l, n: (l, 0, n)),
        ],
        out_specs=pl.BlockSpec((None, rows, d), lambda l, n: (l, 0, n)),
        out_shape=jax.ShapeDtypeStruct((depth, rows, n6), jnp.float32),
        compiler_params=_params("parallel", "parallel"),
        name="adaln_mod",
    )(cvec, w_ada, b_ada.reshape(depth, 1, n6))


def _ln_kernel(x_ref, g_ref, b_ref, o_ref):
    o_ref[...] = _layer_norm(x_ref[...], g_ref[...], b_ref[...], LN_EPS)


def _ln_call(x, g, b, tb):
    rows, d = x.shape
    return pl.pallas_call(
        _ln_kernel,
        grid=(rows // tb,),
        in_specs=[pl.BlockSpec((tb, d), lambda i: (i, 0)), _full((1, d)), _full((1, d))],
        out_specs=pl.BlockSpec((tb, d), lambda i: (i, 0)),
        out_shape=jax.ShapeDtypeStruct((rows, d), jnp.float32),
        compiler_params=_params("parallel"),
        name="ln_in",
    )(x, g.reshape(1, d), b.reshape(1, d))


def _group_of_block(i, n_ctx_blocks, blocks_per_lat_seq):
    return jnp.where(i < n_ctx_blocks, 0, 1 + (i - n_ctx_blocks) // blocks_per_lat_seq)


def _in_proj_kernel(x_ref, mod_ref, w_ref, cos_ref, sin_ref, att_ref, rw_ref, cv_ref):
    x = x_ref[...]
    h = x * (1.0 + mod_ref[1:2, :]) + mod_ref[0:1, :]
    p = _mm(_bf(h), w_ref[...])
    qk = p[:, :QK_W]
    lane = lax.broadcasted_iota(jnp.int32, qk.shape, 1)
    partner = jnp.where((lane % 32) < 16,
                        pltpu.roll(qk, QK_W - 16, 1),
                        pltpu.roll(qk, 16, 1))
    att_ref[:, :QK_W] = qk * cos_ref[...] + partner * sin_ref[...]
    att_ref[:, QK_W:] = p[:, QK_W:ATT_W]
    rw_ref[...] = p[:, ATT_W:ATT_W + RW_W]
    cv_ref[...] = p[:, ATT_W + RW_W:]


def _in_proj_call(x, mod, w_in_p, cos_t, sin_t, tb, n_ctx_rows, lat_seq):
    rows, d = x.shape
    ncb = n_ctx_rows // tb
    bps = lat_seq // tb

    def mod_map(i):
        return (_group_of_block(i, ncb, bps), 0, 0)

    def rope_map(i):
        return (jnp.where(i < ncb, bps, (i - ncb) % bps), 0)

    return pl.pallas_call(
        _in_proj_kernel,
        grid=(rows // tb,),
        in_specs=[
            pl.BlockSpec((tb, d), lambda i: (i, 0)),
            pl.BlockSpec((None, 6, d), mod_map),
            _full((d, IN_PAD)),
            pl.BlockSpec((tb, QK_W), rope_map),
            pl.BlockSpec((tb, QK_W), rope_map),
        ],
        out_specs=[
            pl.BlockSpec((tb, ATT_W), lambda i: (i, 0)),
            pl.BlockSpec((tb, RW_W), lambda i: (i, 0)),
            pl.BlockSpec((tb, CV_W), lambda i: (i, 0)),
        ],
        out_shape=[
            jax.ShapeDtypeStruct((rows, ATT_W), jnp.float32),
            jax.ShapeDtypeStruct((rows, RW_W), jnp.float32),
            jax.ShapeDtypeStruct((rows, CV_W), jnp.float32),
        ],
        compiler_params=_params("parallel"),
        name="in_proj",
    )(x, mod, w_in_p, cos_t, sin_t)


def _ctx_attn_kernel(sink_ref, p_ref, o_ref):
    scale = HEAD_DIM ** -0.5
    for kv in range(ATT_KV_HEADS):
        k = _bf(p_ref[:, ATT_DIM + kv * HEAD_DIM:ATT_DIM + (kv + 1) * HEAD_DIM])
        v = _bf(p_ref[:, ATT_DIM + KV_DIM + kv * HEAD_DIM:ATT_DIM + KV_DIM + (kv + 1) * HEAD_DIM])
        for g in range(ATT_GROUP):
            hd = kv * ATT_GROUP + g
            q = _bf(p_ref[:, hd * HEAD_DIM:(hd + 1) * HEAD_DIM])
            s = _mm(q, k, NT) * scale
            sink = sink_ref[hd]
            m = jnp.maximum(jnp.max(s, axis=-1, keepdims=True), sink)
            e = jnp.exp(s - m)
            den = jnp.sum(e, axis=-1, keepdims=True) + jnp.exp(sink - m)
            o_ref[:, hd * HEAD_DIM:(hd + 1) * HEAD_DIM] = _mm(_bf(e), v) / den


def _ctx_attn_call(p_att, sink, n_seq, seq):
    return pl.pallas_call(
        _ctx_attn_kernel,
        grid=(n_seq,),
        in_specs=[
            pl.BlockSpec(memory_space=pltpu.SMEM),
            pl.BlockSpec((seq, ATT_W), lambda b: (b, 0)),
        ],
        out_specs=pl.BlockSpec((seq, ATT_DIM), lambda b: (b, 0)),
        out_shape=jax.ShapeDtypeStruct((n_seq * seq, ATT_DIM), jnp.float32),
        compiler_params=_params("parallel"),
        name="ctx_attn",
    )(sink, p_att)


def _lat_attn_kernel(sink_ref, own_ref, prev_ref, next_ref, ck_ref, cv_ref, o_ref, *, n_blocks):
    n = pl.program_id(1)
    scale = HEAD_DIM ** -0.5
    qi = lax.broadcasted_iota(jnp.int32, (ATT_BLOCK, 3 * ATT_BLOCK), 0)
    kj = lax.broadcasted_iota(jnp.int32, (ATT_BLOCK, 3 * ATT_BLOCK), 1)
    rel = kj - ATT_BLOCK - qi
    kpos = (n - 1) * ATT_BLOCK + kj
    mask = (jnp.abs(rel) <= WINDOW) & (kpos >= 0) & (kpos < n_blocks * ATT_BLOCK)
    mask4 = jnp.concatenate([mask] * ATT_GROUP, axis=0)
    row_group = lax.broadcasted_iota(jnp.int32, (ATT_GROUP * ATT_BLOCK, 1), 0) // ATT_BLOCK
    for kv in range(ATT_KV_HEADS):
        ks = slice(ATT_DIM + kv * HEAD_DIM, ATT_DIM + (kv + 1) * HEAD_DIM)
        vs = slice(ATT_DIM + KV_DIM + kv * HEAD_DIM, ATT_DIM + KV_DIM + (kv + 1) * HEAD_DIM)
        kw = _bf(jnp.concatenate([prev_ref[:, ks], own_ref[:, ks], next_ref[:, ks]], axis=0))
        vw = _bf(jnp.concatenate([prev_ref[:, vs], own_ref[:, vs], next_ref[:, vs]], axis=0))
        ck = _bf(ck_ref[:, kv * HEAD_DIM:(kv + 1) * HEAD_DIM])
        cv = _bf(cv_ref[:, kv * HEAD_DIM:(kv + 1) * HEAD_DIM])
        q4 = _bf(jnp.concatenate(
            [own_ref[:, (kv * ATT_GROUP + g) * HEAD_DIM:(kv * ATT_GROUP + g + 1) * HEAD_DIM]
             for g in range(ATT_GROUP)], axis=0))
        s_ctx = _mm(q4, ck, NT) * scale
        s_win = jnp.where(mask4, _mm(q4, kw, NT) * scale, NEG)
        sink = jnp.full((ATT_GROUP * ATT_BLOCK, 1), sink_ref[kv * ATT_GROUP], jnp.float32)
        for g in range(1, ATT_GROUP):
            sink = jnp.where(row_group == g, sink_ref[kv * ATT_GROUP + g], sink)
        m = jnp.maximum(jnp.maximum(jnp.max(s_ctx, axis=-1, keepdims=True),
                                    jnp.max(s_win, axis=-1, keepdims=True)), sink)
        e_ctx = jnp.exp(s_ctx - m)
        e_win = jnp.exp(s_win - m)
        den = (jnp.sum(e_ctx, axis=-1, keepdims=True) + jnp.sum(e_win, axis=-1, keepdims=True)
               + jnp.exp(sink - m))
        o4 = (_mm(_bf(e_ctx), cv) + _mm(_bf(e_win), vw)) / den
        for g in range(ATT_GROUP):
            hd = kv * ATT_GROUP + g
            o_ref[:, hd * HEAD_DIM:(hd + 1) * HEAD_DIM] = o4[g * ATT_BLOCK:(g + 1) * ATT_BLOCK]


def _lat_attn_call(p_att, ck, cv, sink, layer, n_seq, seq, row_off):
    nb = seq // ATT_BLOCK
    off = row_off // ATT_BLOCK
    past = ck.shape[2]

    def own(b, n):
        return (off + b * nb + n, 0)

    def prev(b, n):
        return (off + b * nb + jnp.maximum(n - 1, 0), 0)

    def nxt(b, n):
        return (off + b * nb + jnp.minimum(n + 1, nb - 1), 0)

    return pl.pallas_call(
        functools.partial(_lat_attn_kernel, n_blocks=nb),
        grid=(n_seq, nb),
        in_specs=[
            pl.BlockSpec(memory_space=pltpu.SMEM),
            pl.BlockSpec((ATT_BLOCK, ATT_W), own),
            pl.BlockSpec((ATT_BLOCK, ATT_W), prev),
            pl.BlockSpec((ATT_BLOCK, ATT_W), nxt),
            pl.BlockSpec((None, None, past, KV_DIM), lambda b, n: (b, layer, 0, 0)),
            pl.BlockSpec((None, None, past, KV_DIM), lambda b, n: (b, layer, 0, 0)),
        ],
        out_specs=pl.BlockSpec((ATT_BLOCK, ATT_DIM), lambda b, n: (b * nb + n, 0)),
        out_shape=jax.ShapeDtypeStruct((n_seq * seq, ATT_DIM), jnp.float32),
        compiler_params=_params("parallel", "parallel"),
        name="lat_attn",
    )(sink, p_att, p_att, p_att, ck, cv)


def _rwkv_consts():
    L, H, W = RW_CHUNK, RWKV_HEADS, RWKV_DIM
    t = np.arange(L)
    inc = np.stack([(t[None, :] <= t[:, None]), (t[None, :] >= t[:, None])]).astype(np.float32)
    stc = np.stack([(t[None, :] < t[:, None]), (t[None, :] > t[:, None])]).astype(np.float32)
    eye_h = np.eye(H, dtype=np.float32)
    inc_bd = np.stack([np.kron(eye_h, inc[d]) for d in range(2)])
    stc_bd = np.stack([np.kron(eye_h, stc[d]) for d in range(2)])
    blk = np.kron(eye_h, np.ones((L, HEAD_DIM), np.float32))
    eye = np.eye(W, dtype=np.float32)
    return inc, inc_bd, stc_bd, blk, eye


def _rwkv_kernel(p_ref, s0_ref, inc_ref, incbd_ref, stcbd_ref, blk_ref, eye_ref,
                 w0_ref, wup_ref, a0_ref, aup_ref, gup_ref, kkw_ref, kaw_ref, rkw_ref, lng_ref, lnb_ref,
                 o_ref, sfin_ref, y_scr, bon_scr, *, seq, n_par):
    L = RW_CHUNK
    nc = seq // L
    blk = _bf(blk_ref[...])
    ones_bd = blk
    eye = eye_ref[...]
    y_scr[...] = jnp.zeros_like(y_scr)
    bon_scr[...] = jnp.zeros_like(bon_scr)

    def stack(x):
        return tuple(jnp.concatenate([z] * RWKV_HEADS, axis=0) * blk for z in _split(x))

    def chunk(idx, r0, s_in, s_out):
        d = idx % 2
        x = p_ref[pl.ds(r0, L), :]
        rr = x[:, 0:RWKV_DIM]
        rk = x[:, RWKV_DIM:2 * RWKV_DIM]
        rv = x[:, 2 * RWKV_DIM:3 * RWKV_DIM]
        low = x[:, RW_LOW:RW_LOW + LANES]
        zw = w0_ref[d] + _mm_split(_split(jnp.tanh(low)), _split(wup_ref[d]), NN, 3)
        yield
        u = -zw
        softplus = jnp.maximum(u, 0.0) + jnp.log1p(jnp.exp(-jnp.abs(u)))
        lw = -jnp.exp(-softplus - 0.5)
        asig = jax.nn.sigmoid(a0_ref[d] + _mm_split(_split(low), _split(aup_ref[d]), NN, 3))
        yield
        kd = rk * (1.0 + (asig - 1.0) * kaw_ref[...])
        kkr = rk * kkw_ref[...]
        kk = kkr * lax.rsqrt(jnp.maximum(_mm_terms(kkr * kkr, ones_bd, 2), 1e-24))
        yield
        a_s = -kk
        b_s = kk * asig
        bon_scr[pl.ds(r0, L), :] += _mm_terms(rr * kd * rkw_ref[...], ones_bd, 2) * rv
        yield

        cl = _mm_terms(lw, _bf(inc_ref[d]), 3, x_is_rhs=True)
        yield
        cl_end = cl[L - 1:L, :] if d == 0 else cl[0:1, :]
        e_neg = jnp.exp(-cl)
        e_tail = jnp.exp(cl_end - cl)
        a_st = stack(a_s * jnp.exp(cl - lw))
        b_st = stack(b_s * e_neg)
        k_st = stack(kd * e_neg)
        r_st = stack(rr * jnp.exp(cl))
        v_st = stack(rv)
        bh_st = stack(b_s * e_tail)
        kh_st = stack(kd * e_tail)
        stc = stcbd_ref[d]
        inc = incbd_ref[d]
        n_m = stc * _mm_split(a_st, b_st, NT, RW_PASSES["gram_n"])
        yield
        m1 = stc * _mm_split(a_st, k_st, NT, RW_PASSES["gram"])
        yield
        fill = {}
        fillers = [
            lambda: fill.update(p=inc * _mm_split(r_st, b_st, NT, RW_PASSES["gram"])),
            lambda: fill.update(m1v=_mm_split(_split(m1), v_st, NN, RW_PASSES["loc"])),
            lambda: fill.update(q=inc * _mm_split(r_st, k_st, NT, RW_PASSES["gram"])),
            lambda: fill.update(vk=_mm_split(v_st, kh_st, TN, RW_PASSES["loc"])),
            lambda: fill.update(rs=_mm_split(r_st, _split(s_in), NT, RW_PASSES["seq"])),
        ]
        t_m = eye + n_m
        n_p = _split(n_m)
        for it in range(5):
            n_sq = _mm_split(n_p, n_p, NN, RW_PASSES["inv"])
            yield
            fillers[it]()
            yield
            n_p = _split(n_sq)
            t_m = t_m + _mm_split(_split(t_m), n_p, NN, RW_PASSES["inv"])
            yield
        t_s = _split(t_m)
        qv = _mm_split(_split(fill["q"]), v_st, NN, RW_PASSES["loc"])
        yield
        u_loc = _mm_split(t_s, _split(fill["m1v"]), NN, RW_PASSES["loc"])
        yield
        w_t = _mm_split(t_s, a_st, NN, RW_PASSES["loc"])
        yield
        u_m = _mm_split(_split(w_t), _split(s_in), NT, RW_PASSES["seq"]) + u_loc
        yield
        u_s = _split(u_m)
        y_bd = fill["rs"] + _mm_split(_split(fill["p"]), u_s, NN, RW_PASSES["seq"]) + qv
        y = y_bd[0:L] + y_bd[L:2 * L] + y_bd[2 * L:3 * L] + y_bd[3 * L:4 * L]
        y_scr[pl.ds(r0, L), :] += y
        yield
        s_out[idx] = s_in * jnp.exp(cl_end) + _mm_split(u_s, bh_st, TN, RW_PASSES["seq"]) + fill["vk"]

    def body(c, carry):
        s_out = {}
        stages = []
        for sq in range(n_par):
            stages.append(chunk(2 * sq, pl.multiple_of(sq * seq + c * L, L), carry[2 * sq], s_out))
            stages.append(chunk(2 * sq + 1, pl.multiple_of(sq * seq + (nc - 1 - c) * L, L), carry[2 * sq + 1],
                                s_out))
        while stages:
            for gen in list(stages):
                if next(gen, StopIteration) is StopIteration:
                    stages.remove(gen)
        return tuple(s_out[i] for i in range(2 * n_par))

    s_fin = lax.fori_loop(0, nc, body, tuple(s0_ref[i // 2, i % 2] for i in range(2 * n_par)))
    for i in range(2 * n_par):
        sfin_ref[i // 2, i % 2] = s_fin[i]

    inv_n = 1.0 / HEAD_DIM

    def epilogue(c, carry):
        r0 = pl.multiple_of(c * L, L)
        y = y_scr[pl.ds(r0, L), :]
        mu = _mm_terms(y, ones_bd, 3) * inv_n
        yc = y - mu
        var = _mm_terms(yc * yc, ones_bd, 2) * inv_n
        yn = yc * lax.rsqrt(var + RWKV_LNX_EPS) * lng_ref[...] + lnb_ref[...]
        gd = p_ref[pl.ds(r0, L), RW_LOW + LANES:RW_LOW + 2 * LANES]
        g = _mm_split(_split(jax.nn.sigmoid(gd)), _split(gup_ref[...]), NN, 3)
        bon = bon_scr[pl.ds(r0, L), :]
        o_ref[pl.ds(r0, L), :] = (yn + bon) * g
        return carry

    lax.fori_loop(0, n_par * nc, epilogue, 0)


def _rwkv_call(p_rw, s0_bd, consts, wts, n_seq, seq, row_off, n_par, single_buffer_input):
    inc, inc_bd, stc_bd, blk, eye = consts
    W = RWKV_DIM
    rows = n_par * seq
    off = row_off // rows
    p_mode = dict(pipeline_mode=pl.Buffered(1)) if single_buffer_input else {}
    in_specs = [
        pl.BlockSpec((rows, RW_W), lambda b: (off + b, 0), **p_mode),
        pl.BlockSpec((n_par, 2, W, W), lambda b: (b, 0, 0, 0)),
        _full(inc.shape), _full(inc_bd.shape), _full(stc_bd.shape), _full(blk.shape), _full(eye.shape),
    ] + [_full(w.shape) for w in wts]
    return pl.pallas_call(
        functools.partial(_rwkv_kernel, seq=seq, n_par=n_par),
        grid=(n_seq // n_par,),
        in_specs=in_specs,
        out_specs=[
            pl.BlockSpec((rows, W), lambda b: (b, 0)),
            pl.BlockSpec((n_par, 2, W, W), lambda b: (b, 0, 0, 0)),
        ],
        out_shape=[
            jax.ShapeDtypeStruct((n_seq * seq, W), jnp.float32),
            jax.ShapeDtypeStruct((n_seq, 2, W, W), jnp.float32),
        ],
        scratch_shapes=[pltpu.VMEM((rows, W), jnp.float32), pltpu.VMEM((rows, W), jnp.float32)],
        compiler_params=_params("parallel"),
        name="rwkv",
    )(p_rw, s0_bd, inc, inc_bd, stc_bd, blk, eye, *wts)


def _rwkv_weights(w0, w_up, a0, a_up, g_up, kk, ka, rk, lng, lnb):
    W = RWKV_DIM
    wup_p = jnp.zeros((2, LANES, W), jnp.float32)
    aup_p = jnp.zeros((2, LANES, W), jnp.float32)
    for d in range(2):
        wup_p = wup_p.at[d, d * W_RANK:(d + 1) * W_RANK].set(w_up[d])
        aup_p = aup_p.at[d, 2 * W_RANK + d * A_RANK:2 * W_RANK + (d + 1) * A_RANK].set(a_up[d])
    gup_p = jnp.zeros((LANES, W), jnp.float32).at[:G_RANK].set(g_up)
    row = lambda v: v.reshape(1, W)
    return (w0.reshape(2, 1, W), wup_p, a0.reshape(2, 1, W), aup_p, gup_p,
            row(kk), row(ka), row(rk), row(lng), row(lnb))


def _out_proj_kernel(attc_ref, attl_ref, rwc_ref, rwl_ref, cv_ref, cvp_ref, cvn_ref, x_ref, mod_ref,
                     w_ref, cw_ref, g_ref, b_ref, o_ref, *, tb, ncb, bps_c, bps):
    i = pl.program_id(0)
    is_ctx = i < ncb
    pos = jnp.where(is_ctx, i % bps_c, (i - ncb) % bps)
    last = jnp.where(is_ctx, bps_c - 1, bps - 1)
    att = jnp.where(is_ctx, attc_ref[...], attl_ref[...])
    rw = jnp.where(is_ctx, rwc_ref[...], rwl_ref[...])

    cb = cv_ref[:, 0:CONV_DIM]
    z = cv_ref[:, CONV_DIM:2 * CONV_DIM] * cv_ref[:, 2 * CONV_DIM:3 * CONV_DIM]
    halo_p = cvp_ref[SUBLANES - 1:SUBLANES, CONV_DIM:2 * CONV_DIM] * cvp_ref[SUBLANES - 1:SUBLANES, 2 * CONV_DIM:]
    halo_n = cvn_ref[0:1, CONV_DIM:2 * CONV_DIM] * cvn_ref[0:1, 2 * CONV_DIM:]
    halo_p = jnp.where(pos > 0, halo_p, 0.0)
    halo_n = jnp.where(pos < last, halo_n, 0.0)
    row = lax.broadcasted_iota(jnp.int32, z.shape, 0)
    z_prev = jnp.where(row == 0, halo_p, pltpu.roll(z, 1, 0))
    z_next = jnp.where(row == tb - 1, halo_n, pltpu.roll(z, tb - 1, 0))
    conv = cb * (cw_ref[0:1, :] * z_prev + cw_ref[1:2, :] * z + cw_ref[2:3, :] * z_next)

    mix = (_mm(_bf(att), w_ref[0:ATT_DIM, :])
           + _mm(_bf(rw), w_ref[ATT_DIM:ATT_DIM + RWKV_DIM, :])
           + _mm(_bf(conv), w_ref[ATT_DIM + RWKV_DIM:, :]))
    o_ref[...] = _layer_norm(ALPHA * x_ref[...] + mod_ref[2:3, :] * mix, g_ref[...], b_ref[...], LN_EPS)


def _out_proj_call(att_c, att_l, rw_c, rw_l, p_cv, x, mod, w_out_b, conv_w, ln_g, ln_b, tb, n_ctx_rows, ctx_seq,
                   lat_seq):
    rows, d = x.shape
    ncb = n_ctx_rows // tb
    bps = lat_seq // tb
    sub = tb // SUBLANES
    n_sub = rows // SUBLANES

    def ctx_map(i):
        return (jnp.minimum(i, ncb - 1), 0)

    def lat_map(i):
        return (jnp.maximum(i - ncb, 0), 0)

    return pl.pallas_call(
        functools.partial(_out_proj_kernel, tb=tb, ncb=ncb, bps_c=ctx_seq // tb, bps=bps),
        grid=(rows // tb,),
        in_specs=[
            pl.BlockSpec((tb, ATT_DIM), ctx_map),
            pl.BlockSpec((tb, ATT_DIM), lat_map),
            pl.BlockSpec((tb, RWKV_DIM), ctx_map),
            pl.BlockSpec((tb, RWKV_DIM), lat_map),
            pl.BlockSpec((tb, CV_W), lambda i: (i, 0)),
            pl.BlockSpec((SUBLANES, CV_W), lambda i: (jnp.maximum(i * sub - 1, 0), 0)),
            pl.BlockSpec((SUBLANES, CV_W), lambda i: (jnp.minimum((i + 1) * sub, n_sub - 1), 0)),
            pl.BlockSpec((tb, d), lambda i: (i, 0)),
            pl.BlockSpec((None, 6, d), lambda i: (_group_of_block(i, ncb, bps), 0, 0)),
            _full((d, d)),
            _full((3, CONV_DIM)),
            _full((1, d)),
            _full((1, d)),
        ],
        out_specs=pl.BlockSpec((tb, d), lambda i: (i, 0)),
        out_shape=jax.ShapeDtypeStruct((rows, d), jnp.float32),
        compiler_params=_params("parallel"),
        name="out_proj",
    )(att_c, att_l, rw_c, rw_l, p_cv, p_cv, p_cv, x, mod, w_out_b, conv_w, ln_g.reshape(1, d), ln_b.reshape(1, d))


def _gelu_tanh(x):
    return 0.5 * x * (1.0 + jnp.tanh(math.sqrt(2.0 / math.pi) * (x + 0.044715 * (x * x * x))))


def _extract_top(work, iota, n_out, want_rank=False, by_value=False):
    vals = []
    big = work.shape[0]
    rank = jnp.full(work.shape, float(big), jnp.float32) if want_rank else None
    for k in range(n_out):
        m = jnp.max(work, axis=0, keepdims=True)
        if by_value:
            sel = work == m
        else:
            idx = jnp.min(jnp.where(work == m, iota, big), axis=0, keepdims=True)
            sel = iota == idx
        work = jnp.where(sel, -jnp.inf, work)
        if want_rank:
            rank = jnp.where(sel, float(k), rank)
        vals.append(m)
        yield
    return vals, work, rank


def _lockstep(*gens):
    results = [None] * len(gens)
    live = list(range(len(gens)))
    while live:
        for i in list(live):
            try:
                next(gens[i])
            except StopIteration as stop:
                results[i] = stop.value
                live.remove(i)
        yield
    return results


def _run_lockstep(*gens):
    stepper = _lockstep(*gens)
    while True:
        try:
            next(stepper)
        except StopIteration as stop:
            return stop.value


def _peer_kernel(x_ref, mod_ref, wq_ref, sk_ref, u_ref, vt_ref, g_ref, b_ref, o_ref,
                 ht_scr, s0_scr, e0_scr, s1_scr, r1_scr, e1_scr, a_scr, z_scr, acc_scr, *, tb):
    e_step = pl.program_id(1)
    n_lt = tb // LANES
    K = PEER_TOPK

    @pl.when(e_step == 0)
    def _route():
        h2 = x_ref[...] * (1.0 + mod_ref[4:5, :]) + mod_ref[3:4, :]
        ht = _bf(h2.T)
        ht_scr[...] = ht
        qt = _mm(wq_ref[...], ht)
        for hp in range(2 * PEER_HEADS):
            sc = _mm_split(_split(sk_ref[hp]), _split(qt[hp * PEER_HALF:(hp + 1) * PEER_HALF, :]), NN,
                           PEER_SCORE_PASSES)
            if hp % 2 == 0:
                s0_scr[hp // 2] = sc
            else:
                s1_scr[hp // 2] = sc

        iota_k = lax.broadcasted_iota(jnp.int32, (PEER_NKEYS, LANES), 0)
        iota_c = lax.broadcasted_iota(jnp.int32, ((SUBLANES + 2) * SUBLANES, LANES), 0)

        def route_tile(tile, by_value):
            hd, lanes, sc0, sc1 = tile
            (v0, w0, _), (v1, w1, rank1) = yield from _lockstep(
                _extract_top(sc0, iota_k, K, by_value=by_value),
                _extract_top(sc1, iota_k, K, want_rank=True, by_value=by_value))
            cand0 = w0 != sc0
            cand1 = w1 != sc1
            lo1 = jnp.concatenate(v1[:SUBLANES], axis=0)
            hi1 = jnp.concatenate(v1[SUBLANES:], axis=0)
            hi0 = jnp.concatenate(v0[SUBLANES:], axis=0)
            pair = jnp.concatenate([v0[a] + lo1 for a in range(SUBLANES)] + [v0[0] + hi1, hi0 + v1[0]], axis=0)
            f, wp, _ = yield from _extract_top(pair, iota_c, K + 1, by_value=by_value)
            zsum = jnp.zeros_like(f[0])
            for k in range(K):
                zsum = zsum + jnp.exp(f[k] - f[0])
            th = 0.5 * (f[K - 1] + f[K]) - sc0
            count0 = jnp.zeros_like(sc0)
            for b in range(K):
                count0 = count0 + jnp.where(v1[b] > th, 1.0, 0.0)
            s0_scr[hd, :, lanes] = jnp.where(cand0, count0, 0.0)
            e0_scr[hd, :, lanes] = jnp.where(cand0, jnp.exp(sc0 - v0[0]), 0.0) / zsum
            e1_scr[hd, :, lanes] = jnp.where(cand1, jnp.exp(sc1 - v1[0]), 0.0)
            r1_scr[hd, :, lanes] = rank1
            if not by_value:
                return None
            removed = [jnp.sum(jnp.where(c, 1.0, 0.0), axis=0, keepdims=True) for c in (cand0, cand1, wp != pair)]
            wrong = (removed[0] != K) | (removed[1] != K) | (removed[2] != K + 1)
            return jnp.max(jnp.where(wrong, 1.0, 0.0)) > 0.0

        def per_tile_group(j, carry):
            tiles = []
            for t in [PEER_ROUTE_TILES * j + s for s in range(PEER_ROUTE_TILES)]:
                hd = t // n_lt
                lanes = pl.ds(pl.multiple_of((t % n_lt) * LANES, LANES), LANES)
                tiles.append((hd, lanes, s0_scr[hd, :, lanes], s1_scr[hd, :, lanes]))
            had_ties = _run_lockstep(*[route_tile(tile, by_value=True) for tile in tiles])
            for tile, redo in zip(tiles, had_ties):
                @pl.when(redo)
                def _exact(tile=tile):
                    _run_lockstep(route_tile(tile, by_value=False))
            return carry

        lax.fori_loop(0, PEER_HEADS * n_lt // PEER_ROUTE_TILES, per_tile_group, 0)
        acc_scr[...] = jnp.zeros_like(acc_scr)

    et = SUBLANES * PEER_NKEYS
    group = SUBLANES // 2
    jblk = PEER_GATE_ROWS
    tile3 = (jblk // SUBLANES, SUBLANES, LANES)
    for sub in range(PEER_GROUPS_PER_STEP):
        a_scr[...] = _gelu_tanh(_mm(u_ref[sub * et:(sub + 1) * et, :], ht_scr[...]))
        grp = e_step * PEER_GROUPS_PER_STEP + sub
        keys = pl.ds(pl.multiple_of(grp * SUBLANES, SUBLANES), SUBLANES)
        for ln in range(n_lt):
            lanes = slice(ln * LANES, (ln + 1) * LANES)
            c0 = [s0_scr[hd, keys, lanes] for hd in range(PEER_HEADS)]
            e0 = [e0_scr[hd, keys, lanes] for hd in range(PEER_HEADS)]
            for jb in range(PEER_NKEYS // jblk):
                jrows = slice(jb * jblk, (jb + 1) * jblk)
                for ig in range(SUBLANES // group):
                    g = [jnp.zeros(tile3, jnp.float32) for _ in range(group)]
                    for hd in range(PEER_HEADS):
                        r1 = r1_scr[hd, jrows, lanes].reshape(tile3)
                        e1 = e1_scr[hd, jrows, lanes].reshape(tile3)
                        for k in range(group):
                            ii = ig * group + k
                            cnt = jnp.broadcast_to(c0[hd][ii:ii + 1, :], tile3[1:])[None]
                            w0 = jnp.broadcast_to(e0[hd][ii:ii + 1, :], tile3[1:])[None]
                            g[k] = g[k] + jnp.where(r1 < cnt, e1, 0.0) * w0
                    for k in range(group):
                        r0 = (ig * group + k) * PEER_NKEYS + jb * jblk
                        z_scr[r0:r0 + jblk, lanes] = _bf(g[k].reshape(jblk, LANES) * a_scr[r0:r0 + jblk, lanes])
        acc_scr[...] += _mm(vt_ref[sub], z_scr[...])

    @pl.when(e_step == pl.num_programs(1) - 1)
    def _finish():
        out = acc_scr[...].T
        o_ref[...] = _layer_norm(ALPHA * x_ref[...] + mod_ref[5:6, :] * out, g_ref[...], b_ref[...], LN_EPS)


def _peer_call(x, mod, wq_t, subkeys, u_b, vt_b, ln_g, ln_b, tb, n_ctx_rows, lat_seq):
    rows, d = x.shape
    et = SUBLANES * PEER_NKEYS
    gps = PEER_GROUPS_PER_STEP
    ncb = n_ctx_rows // tb
    bps = lat_seq // tb
    nq = wq_t.shape[0]
    f32 = jnp.float32
    return pl.pallas_call(
        functools.partial(_peer_kernel, tb=tb),
        grid=(rows // tb, u_b.shape[0] // (gps * et)),
        in_specs=[
            pl.BlockSpec((tb, d), lambda i, e: (i, 0)),
            pl.BlockSpec((None, 6, d), lambda i, e: (_group_of_block(i, ncb, bps), 0, 0)),
            pl.BlockSpec((nq, d), lambda i, e: (0, 0), pipeline_mode=pl.Buffered(1)),
            pl.BlockSpec(subkeys.shape, lambda i, e: (0, 0, 0), pipeline_mode=pl.Buffered(1)),
            pl.BlockSpec((gps * et, d), lambda i, e: (e, 0)),
            pl.BlockSpec((gps, d, et), lambda i, e: (e, 0, 0)),
            pl.BlockSpec((1, d), lambda i, e: (0, 0)),
            pl.BlockSpec((1, d), lambda i, e: (0, 0)),
        ],
        out_specs=pl.BlockSpec((tb, d), lambda i, e: (i, 0)),
        out_shape=jax.ShapeDtypeStruct((rows, d), f32),
        scratch_shapes=[
            pltpu.VMEM((d, tb), jnp.bfloat16),
            pltpu.VMEM((PEER_HEADS, PEER_NKEYS, tb), f32),
            pltpu.VMEM((PEER_HEADS, PEER_NKEYS, tb), f32),
            pltpu.VMEM((PEER_HEADS, PEER_NKEYS, tb), f32),
            pltpu.VMEM((PEER_HEADS, PEER_NKEYS, tb), f32),
            pltpu.VMEM((PEER_HEADS, PEER_NKEYS, tb), f32),
            pltpu.VMEM((et, tb), f32),
            pltpu.VMEM((et, tb), jnp.bfloat16),
            pltpu.VMEM((d, tb), f32),
        ],
        compiler_params=_params("parallel", "arbitrary"),
        name="peer",
    )(x, mod, wq_t, subkeys, u_b, vt_b, ln_g.reshape(1, d), ln_b.reshape(1, d))


def _rope_tables(seq, tb):
    half = HEAD_DIM // 4
    freqs = ROPE_THETA ** (-jnp.arange(half, dtype=jnp.float32) / half)
    t = jnp.arange(seq)
    cos_parts, sin_parts = [], []
    for pos in (t // GRID_W, t % GRID_W):
        ang = pos.astype(jnp.float32)[:, None] * freqs[None, :]
        c, s = jnp.cos(ang), jnp.sin(ang)
        cos_parts += [c, c]
        sin_parts += [-s, s]
    cos_h = jnp.concatenate(cos_parts, axis=1)
    sin_h = jnp.concatenate(sin_parts, axis=1)
    n_rot = QK_W // HEAD_DIM
    cos_t = jnp.concatenate([jnp.tile(cos_h, (1, n_rot)), jnp.ones((tb, QK_W), jnp.float32)], axis=0)
    sin_t = jnp.concatenate([jnp.tile(sin_h, (1, n_rot)), jnp.zeros((tb, QK_W), jnp.float32)], axis=0)
    return cos_t, sin_t


def _pad_w_in(w):
    split = w.shape[1] - CV_W
    z = jnp.zeros((w.shape[0], IN_PAD - w.shape[1]), w.dtype)
    return jnp.concatenate([w[:, :split], z, w[:, split:]], axis=1)


def _forward(x_prompt, x_sample, cache_k, cache_v, state_rwkv, c, c_ctx, ln_in_g, ln_in_b, w_ada, b_ada,
             w_in, w_out, att_sink, rw_w0, rw_w_up, rw_a0, rw_a_up, rw_g_up, rw_kk, rw_ka, rw_rk,
             rw_lnx_g, rw_lnx_b, conv_w, ln1_g, ln1_b, ln2_g, ln2_b, peer_wq, peer_subkeys, peer_u, peer_v,
             tb_in=512, tb_out=256, tb_peer=512, rw_par=2):
    nb_c, seq_c, d = x_prompt.shape
    nb_l, seq_l, _ = x_sample.shape
    depth = w_in.shape[0]
    n_ctx = nb_c * seq_c
    past = cache_k.shape[2]
    W = RWKV_DIM

    x = jnp.concatenate([x_prompt.reshape(n_ctx, d), x_sample.reshape(nb_l * seq_l, d)], axis=0)
    x = _ln_call(x, ln_in_g, ln_in_b, tb_in)

    n_groups = 1 + nb_l
    g_pad = -(-n_groups // SUBLANES) * SUBLANES
    cvec = jnp.concatenate([c_ctx[None], c, jnp.zeros((g_pad - n_groups, d), jnp.float32)], axis=0)
    mod_all = _mod_call(cvec, w_ada, b_ada).reshape(depth, g_pad, 6, d)

    cos_t, sin_t = _rope_tables(seq_l, tb_in)
    consts = tuple(jnp.asarray(a) for a in _rwkv_consts())
    ck_all = cache_k.reshape(nb_l, depth, past, KV_DIM)
    cv_all = cache_v.reshape(nb_l, depth, past, KV_DIM)
    eye_h = jnp.eye(RWKV_HEADS, dtype=jnp.float32)
    s0_bd_all = jnp.einsum('bldhij,hg->bldhigj', state_rwkv, eye_h).reshape(nb_l, depth, 2, W, W)
    s0_zero = jnp.zeros((nb_c, 2, W, W), jnp.float32)

    ks, vs, sts = [], [], []
    for l in range(depth):
        mod = mod_all[l]
        p_att, p_rw, p_cv = _in_proj_call(x, mod, _bf(_pad_w_in(w_in[l])), cos_t, sin_t, tb_in, n_ctx, seq_l)
        att_c = _ctx_attn_call(p_att, att_sink[l], nb_c, seq_c)
        att_l = _lat_attn_call(p_att, ck_all, cv_all, att_sink[l], l, nb_l, seq_l, n_ctx)
        wts = _rwkv_weights(rw_w0[l], rw_w_up[l], rw_a0[l], rw_a_up[l], rw_g_up[l], rw_kk[l], rw_ka[l],
                            rw_rk[l], rw_lnx_g[l], rw_lnx_b[l])
        rw_c, sfin = _rwkv_call(p_rw, s0_zero, consts, wts, nb_c, seq_c, 0, rw_par, False)
        rw_l, _ = _rwkv_call(p_rw, s0_bd_all[:, l], consts, wts, nb_l, seq_l, n_ctx, rw_par, True)
        x1 = _out_proj_call(att_c, att_l, rw_c, rw_l, p_cv, x, mod, _bf(w_out[l]), conv_w[l],
                            ln1_g[l], ln1_b[l], tb_out, n_ctx, seq_c, seq_l)
        sk = peer_subkeys[l].reshape(2 * PEER_HEADS, PEER_NKEYS, PEER_HALF)
        vt = _bf(peer_v[l]).reshape(-1, SUBLANES * PEER_NKEYS, d).transpose(0, 2, 1)
        x = _peer_call(x1, mod, _bf(peer_wq[l].T), sk, _bf(peer_u[l]), vt,
                       ln2_g[l], ln2_b[l], tb_peer, n_ctx, seq_l)
        ks.append(p_att[:n_ctx, ATT_DIM:ATT_DIM + KV_DIM].reshape(nb_c, seq_c, ATT_KV_HEADS, HEAD_DIM))
        vs.append(p_att[:n_ctx, ATT_DIM + KV_DIM:].reshape(nb_c, seq_c, ATT_KV_HEADS, HEAD_DIM))
        s5 = sfin.reshape(nb_c, 2, RWKV_HEADS, HEAD_DIM, RWKV_HEADS, HEAD_DIM)
        sts.append(jnp.stack([s5[:, :, h, :, h, :] for h in range(RWKV_HEADS)], axis=2))

    y_prompt = x[:n_ctx].reshape(nb_c, seq_c, d)
    y_sample = x[n_ctx:].reshape(nb_l, seq_l, d)
    return (y_prompt, y_sample, jnp.stack(ks, axis=1), jnp.stack(vs, axis=1), jnp.stack(sts, axis=1))


def kernel(x_prompt, x_sample, cache_k, cache_v, state_rwkv, c, c_ctx, ln_in_g, ln_in_b, w_ada, b_ada, w_in, w_out, att_sink, rw_w0, rw_w_up, rw_a0, rw_a_up, rw_g_up, rw_kk, rw_ka, rw_rk, rw_lnx_g, rw_lnx_b, conv_w, ln1_g, ln1_b, ln2_g, ln2_b, peer_wq, peer_subkeys, peer_u, peer_v):
    return _forward(x_prompt, x_sample, cache_k, cache_v, state_rwkv, c, c_ctx, ln_in_g, ln_in_b, w_ada, b_ada,
                    w_in, w_out, att_sink, rw_w0, rw_w_up, rw_a0, rw_a_up, rw_g_up, rw_kk, rw_ka, rw_rk,
                    rw_lnx_g, rw_lnx_b, conv_w, ln1_g, ln1_b, ln2_g, ln2_b, peer_wq, peer_subkeys, peer_u, peer_v)
```

```python
import functools
import math

import jax
import jax.numpy as jnp
import numpy as np
from jax import lax
from jax.experimental import pallas as pl
from jax.experimental.pallas import tpu as pltpu

D_MODEL = 1024
DEPTH = 4
GRID_W = 64
HEAD_DIM = 64
ATT_DIM = 512
RWKV_DIM = 256
CONV_DIM = 256
ATT_HEADS = 8
ATT_KV_HEADS = 2
ATT_GROUP = 4
KV_DIM = 128
WINDOW = 128
ATT_BLOCK = 128
ROPE_THETA = 10000.0
ROPE_HALF = HEAD_DIM // 4
RWKV_HEADS = 4
W_RANK = 32
A_RANK = 32
G_RANK = 64
RWKV_LNX_EPS = 64e-5
PEER_HEADS = 8
PEER_NKEYS = 128
PEER_TOPK = 16
PEER_HALF = 128
LN_EPS = 1e-5
ALPHA = (2 * DEPTH) ** 0.25
NEG = -1e30

LANES = 128
SUBLANES = 8
VMEM_LIMIT = 60 * 1024 * 1024

QK_W = ATT_DIM + KV_DIM
ATT_W = ATT_DIM + 2 * KV_DIM
RW_W = 1024
CV_W = 3 * CONV_DIM
IN_PAD = ATT_W + RW_W + CV_W
RW_LOW = 3 * RWKV_DIM

RW_CHUNK = 64
PEER_SCORE_PASSES = 3
PEER_ROUTE_TILES = 4
PEER_GATE_ROWS = 32
PEER_GROUPS_PER_STEP = 2
HIGHEST = lax.Precision.HIGHEST
NN = (((1,), (0,)), ((), ()))
NT = (((1,), (1,)), ((), ()))
TN = (((0,), (0,)), ((), ()))


def _mm(a, b, dims=NN, precision=None):
    return lax.dot_general(a, b, dims, precision=precision, preferred_element_type=jnp.float32)


def _bf(x):
    return x.astype(jnp.bfloat16)


def _split(x):
    hi = _bf(x)
    return hi, _bf(x - hi.astype(jnp.float32))


def _mm_split(a, b, dims, passes):
    out = _mm(a[0], b[0], dims)
    if passes == 3:
        out = out + (_mm(a[0], b[1], dims) + _mm(a[1], b[0], dims))
    return out


def _mm_terms(x, exact, terms, x_is_rhs=False):
    out = None
    rest = x
    for _ in range(terms):
        piece = _bf(rest)
        rest = rest - piece.astype(jnp.float32)
        part = _mm(exact, piece) if x_is_rhs else _mm(piece, exact)
        out = part if out is None else out + part
    return out


RW_PASSES = {"gram_n": 1, "gram": 1, "inv": 1, "loc": 1, "seq": 1}


def _layer_norm(x, g, b, eps):
    mu = jnp.mean(x, axis=-1, keepdims=True)
    xc = x - mu
    var = jnp.mean(xc * xc, axis=-1, keepdims=True)
    return xc * lax.rsqrt(var + eps) * g + b


def _params(*sem):
    return pltpu.CompilerParams(dimension_semantics=sem, vmem_limit_bytes=VMEM_LIMIT)


def _full(shape):
    nd = len(shape)
    return pl.BlockSpec(shape, lambda *_: (0,) * nd)


def _mod_kernel(c_ref, w_ref, b_ref, o_ref):
    cv = c_ref[...]
    s = cv * jax.nn.sigmoid(cv)
    o_ref[...] = _mm(s, w_ref[...], precision=HIGHEST) + b_ref[...]


def _mod_call(cvec, w_ada, b_ada):
    depth, d, n6 = w_ada.shape
    rows = cvec.shape[0]
    nt = n6 // d
    return pl.pallas_call(
        _mod_kernel,
        grid=(depth, nt),
        in_specs=[
            pl.BlockSpec((rows, d), lambda l, n: (0, 0)),
            pl.BlockSpec((None, d, d), lambda l, n: (l, 0, n)),
            pl.BlockSpec((None, 1, d), lambda l, n: (l, 0, n)),
        ],
        out_specs=pl.BlockSpec((None, rows, d), lambda l, n: (l, 0, n)),
        out_shape=jax.ShapeDtypeStruct((depth, rows, n6), jnp.float32),
        compiler_params=_params("parallel", "parallel"),
        name="adaln_mod",
    )(cvec, w_ada, b_ada.reshape(depth, 1, n6))


def _ln_kernel(x_ref, g_ref, b_ref, o_ref):
    o_ref[...] = _layer_norm(x_ref[...], g_ref[...], b_ref[...], LN_EPS)


def _ln_call(x, g, b, tb):
    rows, d = x.shape
    return pl.pallas_call(
        _ln_kernel,
        grid=(rows // tb,),
        in_specs=[pl.BlockSpec((tb, d), lambda i: (i, 0)), _full((1, d)), _full((1, d))],
        out_specs=pl.BlockSpec((tb, d), lambda i: (i, 0)),
        out_shape=jax.ShapeDtypeStruct((rows, d), jnp.float32),
        compiler_params=_params("parallel"),
        name="ln_in",
    )(x, g.reshape(1, d), b.reshape(1, d))


def _group_of_block(i, n_ctx_blocks, blocks_per_lat_seq):
    return jnp.where(i < n_ctx_blocks, 0, 1 + (i - n_ctx_blocks) // blocks_per_lat_seq)


def _in_proj_kernel(x_ref, mod_ref, w_ref, cos_ref, sin_ref, att_ref, rw_ref, cv_ref):
    x = x_ref[...]
    h = x * (1.0 + mod_ref[1:2, :]) + mod_ref[0:1, :]
    p = _mm(_bf(h), w_ref[...])
    qk = p[:, :QK_W]
    lane = lax.broadcasted_iota(jnp.int32, qk.shape, 1)
    partner = jnp.where((lane % (2 * ROPE_HALF)) < ROPE_HALF,
                        pltpu.roll(qk, QK_W - ROPE_HALF, 1),
                        pltpu.roll(qk, ROPE_HALF, 1))
    att_ref[:, :QK_W] = qk * cos_ref[...] + partner * sin_ref[...]
    att_ref[:, QK_W:] = p[:, QK_W:ATT_W]
    rw_ref[...] = p[:, ATT_W:ATT_W + RW_W]
    cv_ref[...] = p[:, ATT_W + RW_W:]


def _in_proj_call(x, mod, w_in_p, cos_t, sin_t, tb, n_ctx_rows, lat_seq):
    rows, d = x.shape
    ncb = n_ctx_rows // tb
    bps = lat_seq // tb

    def mod_map(i):
        return (_group_of_block(i, ncb, bps), 0, 0)

    def rope_map(i):
        return (jnp.where(i < ncb, bps, (i - ncb) % bps), 0)

    return pl.pallas_call(
        _in_proj_kernel,
        grid=(rows // tb,),
        in_specs=[
            pl.BlockSpec((tb, d), lambda i: (i, 0)),
            pl.BlockSpec((None, 6, d), mod_map),
            _full((d, IN_PAD)),
            pl.BlockSpec((tb, QK_W), rope_map),
            pl.BlockSpec((tb, QK_W), rope_map),
        ],
        out_specs=[
            pl.BlockSpec((tb, ATT_W), lambda i: (i, 0)),
            pl.BlockSpec((tb, RW_W), lambda i: (i, 0)),
            pl.BlockSpec((tb, CV_W), lambda i: (i, 0)),
        ],
        out_shape=[
            jax.ShapeDtypeStruct((rows, ATT_W), jnp.float32),
            jax.ShapeDtypeStruct((rows, RW_W), jnp.float32),
            jax.ShapeDtypeStruct((rows, CV_W), jnp.float32),
        ],
        compiler_params=_params("parallel"),
        name="in_proj",
    )(x, mod, w_in_p, cos_t, sin_t)


def _ctx_attn_kernel(sink_ref, p_ref, o_ref):
    scale = HEAD_DIM ** -0.5
    for kv in range(ATT_KV_HEADS):
        k = _bf(p_ref[:, ATT_DIM + kv * HEAD_DIM:ATT_DIM + (kv + 1) * HEAD_DIM])
        v = _bf(p_ref[:, ATT_DIM + KV_DIM + kv * HEAD_DIM:ATT_DIM + KV_DIM + (kv + 1) * HEAD_DIM])
        for g in range(ATT_GROUP):
            hd = kv * ATT_GROUP + g
            q = _bf(p_ref[:, hd * HEAD_DIM:(hd + 1) * HEAD_DIM])
            s = _mm(q, k, NT) * scale
            sink = sink_ref[hd]
            m = jnp.maximum(jnp.max(s, axis=-1, keepdims=True), sink)
            e = jnp.exp(s - m)
            den = jnp.sum(e, axis=-1, keepdims=True) + jnp.exp(sink - m)
            o_ref[:, hd * HEAD_DIM:(hd + 1) * HEAD_DIM] = _mm(_bf(e), v) / den


def _ctx_attn_call(p_att, sink, n_seq, seq):
    return pl.pallas_call(
        _ctx_attn_kernel,
        grid=(n_seq,),
        in_specs=[
            pl.BlockSpec(memory_space=pltpu.SMEM),
            pl.BlockSpec((seq, ATT_W), lambda b: (b, 0)),
        ],
        out_specs=pl.BlockSpec((seq, ATT_DIM), lambda b: (b, 0)),
        out_shape=jax.ShapeDtypeStruct((n_seq * seq, ATT_DIM), jnp.float32),
        compiler_params=_params("parallel"),
        name="ctx_attn",
    )(sink, p_att)


def _lat_attn_kernel(sink_ref, own_ref, prev_ref, next_ref, ck_ref, cv_ref, o_ref, *, n_blocks):
    n = pl.program_id(1)
    scale = HEAD_DIM ** -0.5
    qi = lax.broadcasted_iota(jnp.int32, (ATT_BLOCK, 3 * ATT_BLOCK), 0)
    kj = lax.broadcasted_iota(jnp.int32, (ATT_BLOCK, 3 * ATT_BLOCK), 1)
    rel = kj - ATT_BLOCK - qi
    kpos = (n - 1) * ATT_BLOCK + kj
    mask = (jnp.abs(rel) <= WINDOW) & (kpos >= 0) & (kpos < n_blocks * ATT_BLOCK)
    mask4 = jnp.concatenate([mask] * ATT_GROUP, axis=0)
    row_group = lax.broadcasted_iota(jnp.int32, (ATT_GROUP * ATT_BLOCK, 1), 0) // ATT_BLOCK
    for kv in range(ATT_KV_HEADS):
        ks = slice(ATT_DIM + kv * HEAD_DIM, ATT_DIM + (kv + 1) * HEAD_DIM)
        vs = slice(ATT_DIM + KV_DIM + kv * HEAD_DIM, ATT_DIM + KV_DIM + (kv + 1) * HEAD_DIM)
        kw = _bf(jnp.concatenate([prev_ref[:, ks], own_ref[:, ks], next_ref[:, ks]], axis=0))
        vw = _bf(jnp.concatenate([prev_ref[:, vs], own_ref[:, vs], next_ref[:, vs]], axis=0))
        ck = _bf(ck_ref[:, kv * HEAD_DIM:(kv + 1) * HEAD_DIM])
        cv = _bf(cv_ref[:, kv * HEAD_DIM:(kv + 1) * HEAD_DIM])
        q4 = _bf(jnp.concatenate(
            [own_ref[:, (kv * ATT_GROUP + g) * HEAD_DIM:(kv * ATT_GROUP + g + 1) * HEAD_DIM]
             for g in range(ATT_GROUP)], axis=0))
        s_ctx = _mm(q4, ck, NT) * scale
        s_win = jnp.where(mask4, _mm(q4, kw, NT) * scale, NEG)
        sink = jnp.full((ATT_GROUP * ATT_BLOCK, 1), sink_ref[kv * ATT_GROUP], jnp.float32)
        for g in range(1, ATT_GROUP):
            sink = jnp.where(row_group == g, sink_ref[kv * ATT_GROUP + g], sink)
        m = jnp.maximum(jnp.maximum(jnp.max(s_ctx, axis=-1, keepdims=True),
                                    jnp.max(s_win, axis=-1, keepdims=True)), sink)
        e_ctx = jnp.exp(s_ctx - m)
        e_win = jnp.exp(s_win - m)
        den = (jnp.sum(e_ctx, axis=-1, keepdims=True) + jnp.sum(e_win, axis=-1, keepdims=True)
               + jnp.exp(sink - m))
        o4 = (_mm(_bf(e_ctx), cv) + _mm(_bf(e_win), vw)) / den
        for g in range(ATT_GROUP):
            hd = kv * ATT_GROUP + g
            o_ref[:, hd * HEAD_DIM:(hd + 1) * HEAD_DIM] = o4[g * ATT_BLOCK:(g + 1) * ATT_BLOCK]


def _lat_attn_call(p_att, ck, cv, sink, layer, n_seq, seq, row_off):
    nb = seq // ATT_BLOCK
    off = row_off // ATT_BLOCK
    past = ck.shape[2]

    def own(b, n):
        return (off + b * nb + n, 0)

    def prev(b, n):
        return (off + b * nb + jnp.maximum(n - 1, 0), 0)

    def nxt(b, n):
        return (off + b * nb + jnp.minimum(n + 1, nb - 1), 0)

    return pl.pallas_call(
        functools.partial(_lat_attn_kernel, n_blocks=nb),
        grid=(n_seq, nb),
        in_specs=[
            pl.BlockSpec(memory_space=pltpu.SMEM),
            pl.BlockSpec((ATT_BLOCK, ATT_W), own),
            pl.BlockSpec((ATT_BLOCK, ATT_W), prev),
            pl.BlockSpec((ATT_BLOCK, ATT_W), nxt),
            pl.BlockSpec((None, None, past, KV_DIM), lambda b, n: (b, layer, 0, 0)),
            pl.BlockSpec((None, None, past, KV_DIM), lambda b, n: (b, layer, 0, 0)),
        ],
        out_specs=pl.BlockSpec((ATT_BLOCK, ATT_DIM), lambda b, n: (b * nb + n, 0)),
        out_shape=jax.ShapeDtypeStruct((n_seq * seq, ATT_DIM), jnp.float32),
        compiler_params=_params("parallel", "parallel"),
        name="lat_attn",
    )(sink, p_att, p_att, p_att, ck, cv)


def _rwkv_consts():
    L, H, W = RW_CHUNK, RWKV_HEADS, RWKV_DIM
    t = np.arange(L)
    inc = np.stack([(t[None, :] <= t[:, None]), (t[None, :] >= t[:, None])]).astype(np.float32)
    stc = np.stack([(t[None, :] < t[:, None]), (t[None, :] > t[:, None])]).astype(np.float32)
    eye_h = np.eye(H, dtype=np.float32)
    inc_bd = np.stack([np.kron(eye_h, inc[d]) for d in range(2)])
    stc_bd = np.stack([np.kron(eye_h, stc[d]) for d in range(2)])
    blk = np.kron(eye_h, np.ones((L, HEAD_DIM), np.float32))
    eye = np.eye(W, dtype=np.float32)
    return inc, inc_bd, stc_bd, blk, eye


def _rwkv_kernel(p_ref, s0_ref, inc_ref, incbd_ref, stcbd_ref, blk_ref, eye_ref,
                 w0_ref, wup_ref, a0_ref, aup_ref, gup_ref, kkw_ref, kaw_ref, rkw_ref, lng_ref, lnb_ref,
                 o_ref, sfin_ref, y_scr, bon_scr, *, seq, n_par):
    L = RW_CHUNK
    nc = seq // L
    blk = _bf(blk_ref[...])
    ones_bd = blk
    eye = eye_ref[...]
    y_scr[...] = jnp.zeros_like(y_scr)
    bon_scr[...] = jnp.zeros_like(bon_scr)

    def stack(x):
        return tuple(jnp.concatenate([z] * RWKV_HEADS, axis=0) * blk for z in _split(x))

    def chunk(idx, r0, s_in, s_out):
        d = idx % 2
        x = p_ref[pl.ds(r0, L), :]
        rr = x[:, 0:RWKV_DIM]
        rk = x[:, RWKV_DIM:2 * RWKV_DIM]
        rv = x[:, 2 * RWKV_DIM:3 * RWKV_DIM]
        low = x[:, RW_LOW:RW_LOW + LANES]
        zw = w0_ref[d] + _mm_split(_split(jnp.tanh(low)), _split(wup_ref[d]), NN, 3)
        yield
        u = -zw
        softplus = jnp.maximum(u, 0.0) + jnp.log1p(jnp.exp(-jnp.abs(u)))
        lw = -jnp.exp(-softplus - 0.5)
        asig = jax.nn.sigmoid(a0_ref[d] + _mm_split(_split(low), _split(aup_ref[d]), NN, 3))
        yield
        kd = rk * (1.0 + (asig - 1.0) * kaw_ref[...])
        kkr = rk * kkw_ref[...]
        kk = kkr * lax.rsqrt(jnp.maximum(_mm_terms(kkr * kkr, ones_bd, 2), 1e-24))
        yield
        a_s = -kk
        b_s = kk * asig
        bon_scr[pl.ds(r0, L), :] += _mm_terms(rr * kd * rkw_ref[...], ones_bd, 2) * rv
        yield

        cl = _mm_terms(lw, _bf(inc_ref[d]), 3, x_is_rhs=True)
        yield
        cl_end = cl[L - 1:L, :] if d == 0 else cl[0:1, :]
        e_neg = jnp.exp(-cl)
        e_tail = jnp.exp(cl_end - cl)
        a_st = stack(a_s * jnp.exp(cl - lw))
        b_st = stack(b_s * e_neg)
        k_st = stack(kd * e_neg)
        r_st = stack(rr * jnp.exp(cl))
        v_st = stack(rv)
        bh_st = stack(b_s * e_tail)
        kh_st = stack(kd * e_tail)
        stc = stcbd_ref[d]
        inc = incbd_ref[d]
        n_m = stc * _mm_split(a_st, b_st, NT, RW_PASSES["gram_n"])
        yield
        m1 = stc * _mm_split(a_st, k_st, NT, RW_PASSES["gram"])
        yield
        fill = {}
        fillers = [
            lambda: fill.update(p=inc * _mm_split(r_st, b_st, NT, RW_PASSES["gram"])),
            lambda: fill.update(m1v=_mm_split(_split(m1), v_st, NN, RW_PASSES["loc"])),
            lambda: fill.update(q=inc * _mm_split(r_st, k_st, NT, RW_PASSES["gram"])),
            lambda: fill.update(vk=_mm_split(v_st, kh_st, TN, RW_PASSES["loc"])),
            lambda: fill.update(rs=_mm_split(r_st, _split(s_in), NT, RW_PASSES["seq"])),
        ]
        t_m = eye + n_m
        n_p = _split(n_m)
        for it in range(5):
            n_sq = _mm_split(n_p, n_p, NN, RW_PASSES["inv"])
            yield
            fillers[it]()
            yield
            n_p = _split(n_sq)
            t_m = t_m + _mm_split(_split(t_m), n_p, NN, RW_PASSES["inv"])
            yield
        t_s = _split(t_m)
        qv = _mm_split(_split(fill["q"]), v_st, NN, RW_PASSES["loc"])
        yield
        u_loc = _mm_split(t_s, _split(fill["m1v"]), NN, RW_PASSES["loc"])
        yield
        w_t = _mm_split(t_s, a_st, NN, RW_PASSES["loc"])
        yield
        u_m = _mm_split(_split(w_t), _split(s_in), NT, RW_PASSES["seq"]) + u_loc
        yield
        u_s = _split(u_m)
        y_bd = fill["rs"] + _mm_split(_split(fill["p"]), u_s, NN, RW_PASSES["seq"]) + qv
        y = y_bd[0:L] + y_bd[L:2 * L] + y_bd[2 * L:3 * L] + y_bd[3 * L:4 * L]
        y_scr[pl.ds(r0, L), :] += y
        yield
        s_out[idx] = s_in * jnp.exp(cl_end) + _mm_split(u_s, bh_st, TN, RW_PASSES["seq"]) + fill["vk"]

    def body(c, carry):
        s_out = {}
        stages = []
        for sq in range(n_par):
            stages.append(chunk(2 * sq, pl.multiple_of(sq * seq + c * L, L), carry[2 * sq], s_out))
            stages.append(chunk(2 * sq + 1, pl.multiple_of(sq * seq + (nc - 1 - c) * L, L), carry[2 * sq + 1],
                                s_out))
        while stages:
            for gen in list(stages):
                if next(gen, StopIteration) is StopIteration:
                    stages.remove(gen)
        return tuple(s_out[i] for i in range(2 * n_par))

    s_fin = lax.fori_loop(0, nc, body, tuple(s0_ref[i // 2, i % 2] for i in range(2 * n_par)))
    for i in range(2 * n_par):
        sfin_ref[i // 2, i % 2] = s_fin[i]

    inv_n = 1.0 / HEAD_DIM

    def epilogue(c, carry):
        r0 = pl.multiple_of(c * L, L)
        y = y_scr[pl.ds(r0, L), :]
        mu = _mm_terms(y, ones_bd, 3) * inv_n
        yc = y - mu
        var = _mm_terms(yc * yc, ones_bd, 2) * inv_n
        yn = yc * lax.rsqrt(var + RWKV_LNX_EPS) * lng_ref[...] + lnb_ref[...]
        gd = p_ref[pl.ds(r0, L), RW_LOW + LANES:RW_LOW + 2 * LANES]
        g = _mm_split(_split(jax.nn.sigmoid(gd)), _split(gup_ref[...]), NN, 3)
        bon = bon_scr[pl.ds(r0, L), :]
        o_ref[pl.ds(r0, L), :] = (yn + bon) * g
        return carry

    lax.fori_loop(0, n_par * nc, epilogue, 0)


def _rwkv_call(p_rw, s0_bd, consts, wts, n_seq, seq, row_off, n_par, single_buffer_input):
    inc, inc_bd, stc_bd, blk, eye = consts
    W = RWKV_DIM
    rows = n_par * seq
    off = row_off // rows
    p_mode = dict(pipeline_mode=pl.Buffered(1)) if single_buffer_input else {}
    in_specs = [
        pl.BlockSpec((rows, RW_W), lambda b: (off + b, 0), **p_mode),
        pl.BlockSpec((n_par, 2, W, W), lambda b: (b, 0, 0, 0)),
        _full(inc.shape), _full(inc_bd.shape), _full(stc_bd.shape), _full(blk.shape), _full(eye.shape),
    ] + [_full(w.shape) for w in wts]
    return pl.pallas_call(
        functools.partial(_rwkv_kernel, seq=seq, n_par=n_par),
        grid=(n_seq // n_par,),
        in_specs=in_specs,
        out_specs=[
            pl.BlockSpec((rows, W), lambda b: (b, 0)),
            pl.BlockSpec((n_par, 2, W, W), lambda b: (b, 0, 0, 0)),
        ],
        out_shape=[
            jax.ShapeDtypeStruct((n_seq * seq, W), jnp.float32),
            jax.ShapeDtypeStruct((n_seq, 2, W, W), jnp.float32),
        ],
        scratch_shapes=[pltpu.VMEM((rows, W), jnp.float32), pltpu.VMEM((rows, W), jnp.float32)],
        compiler_params=_params("parallel"),
        name="rwkv",
    )(p_rw, s0_bd, inc, inc_bd, stc_bd, blk, eye, *wts)


def _rwkv_weights(w0, w_up, a0, a_up, g_up, kk, ka, rk, lng, lnb):
    W = RWKV_DIM
    wup_p = jnp.zeros((2, LANES, W), jnp.float32)
    aup_p = jnp.zeros((2, LANES, W), jnp.float32)
    for d in range(2):
        wup_p = wup_p.at[d, d * W_RANK:(d + 1) * W_RANK].set(w_up[d])
        aup_p = aup_p.at[d, 2 * W_RANK + d * A_RANK:2 * W_RANK + (d + 1) * A_RANK].set(a_up[d])
    gup_p = jnp.zeros((LANES, W), jnp.float32).at[:G_RANK].set(g_up)
    row = lambda v: v.reshape(1, W)
    return (w0.reshape(2, 1, W), wup_p, a0.reshape(2, 1, W), aup_p, gup_p,
            row(kk), row(ka), row(rk), row(lng), row(lnb))


def _out_proj_kernel(attc_ref, attl_ref, rwc_ref, rwl_ref, cv_ref, cvp_ref, cvn_ref, x_ref, mod_ref,
                     w_ref, cw_ref, g_ref, b_ref, o_ref, *, tb, ncb, bps_c, bps):
    i = pl.program_id(0)
    is_ctx = i < ncb
    pos = jnp.where(is_ctx, i % bps_c, (i - ncb) % bps)
    last = jnp.where(is_ctx, bps_c - 1, bps - 1)
    att = jnp.where(is_ctx, attc_ref[...], attl_ref[...])
    rw = jnp.where(is_ctx, rwc_ref[...], rwl_ref[...])

    cb = cv_ref[:, 0:CONV_DIM]
    z = cv_ref[:, CONV_DIM:2 * CONV_DIM] * cv_ref[:, 2 * CONV_DIM:3 * CONV_DIM]
    halo_p = cvp_ref[SUBLANES - 1:SUBLANES, CONV_DIM:2 * CONV_DIM] * cvp_ref[SUBLANES - 1:SUBLANES, 2 * CONV_DIM:]
    halo_n = cvn_ref[0:1, CONV_DIM:2 * CONV_DIM] * cvn_ref[0:1, 2 * CONV_DIM:]
    halo_p = jnp.where(pos > 0, halo_p, 0.0)
    halo_n = jnp.where(pos < last, halo_n, 0.0)
    row = lax.broadcasted_iota(jnp.int32, z.shape, 0)
    z_prev = jnp.where(row == 0, halo_p, pltpu.roll(z, 1, 0))
    z_next = jnp.where(row == tb - 1, halo_n, pltpu.roll(z, tb - 1, 0))
    conv = cb * (cw_ref[0:1, :] * z_prev + cw_ref[1:2, :] * z + cw_ref[2:3, :] * z_next)

    mix = (_mm(_bf(att), w_ref[0:ATT_DIM, :])
           + _mm(_bf(rw), w_ref[ATT_DIM:ATT_DIM + RWKV_DIM, :])
           + _mm(_bf(conv), w_ref[ATT_DIM + RWKV_DIM:, :]))
    o_ref[...] = _layer_norm(ALPHA * x_ref[...] + mod_ref[2:3, :] * mix, g_ref[...], b_ref[...], LN_EPS)


def _out_proj_call(att_c, att_l, rw_c, rw_l, p_cv, x, mod, w_out_b, conv_w, ln_g, ln_b, tb, n_ctx_rows, ctx_seq,
                   lat_seq):
    rows, d = x.shape
    ncb = n_ctx_rows // tb
    bps = lat_seq // tb
    sub = tb // SUBLANES
    n_sub = rows // SUBLANES

    def ctx_map(i):
        return (jnp.minimum(i, ncb - 1), 0)

    def lat_map(i):
        return (jnp.maximum(i - ncb, 0), 0)

    return pl.pallas_call(
        functools.partial(_out_proj_kernel, tb=tb, ncb=ncb, bps_c=ctx_seq // tb, bps=bps),
        grid=(rows // tb,),
        in_specs=[
            pl.BlockSpec((tb, ATT_DIM), ctx_map),
            pl.BlockSpec((tb, ATT_DIM), lat_map),
            pl.BlockSpec((tb, RWKV_DIM), ctx_map),
            pl.BlockSpec((tb, RWKV_DIM), lat_map),
            pl.BlockSpec((tb, CV_W), lambda i: (i, 0)),
            pl.BlockSpec((SUBLANES, CV_W), lambda i: (jnp.maximum(i * sub - 1, 0), 0)),
            pl.BlockSpec((SUBLANES, CV_W), lambda i: (jnp.minimum((i + 1) * sub, n_sub - 1), 0)),
            pl.BlockSpec((tb, d), lambda i: (i, 0)),
            pl.BlockSpec((None, 6, d), lambda i: (_group_of_block(i, ncb, bps), 0, 0)),
            _full((d, d)),
            _full((3, CONV_DIM)),
            _full((1, d)),
            _full((1, d)),
        ],
        out_specs=pl.BlockSpec((tb, d), lambda i: (i, 0)),
        out_shape=jax.ShapeDtypeStruct((rows, d), jnp.float32),
        compiler_params=_params("parallel"),
        name="out_proj",
    )(att_c, att_l, rw_c, rw_l, p_cv, p_cv, p_cv, x, mod, w_out_b, conv_w, ln_g.reshape(1, d), ln_b.reshape(1, d))


def _gelu_tanh(x):
    c = math.sqrt(2.0 / math.pi)
    half_x = 0.5 * x
    return half_x + half_x * jnp.tanh(x * (c + (c * 0.044715) * (x * x)))


def _extract_top(work, iota, n_out, want_rank=False, by_value=False):
    vals = []
    big = work.shape[0]
    rank = jnp.full(work.shape, float(big), jnp.float32) if want_rank else None
    for k in range(n_out):
        m = jnp.max(work, axis=0, keepdims=True)
        if by_value:
            sel = work == m
        else:
            idx = jnp.min(jnp.where(work == m, iota, big), axis=0, keepdims=True)
            sel = iota == idx
        work = jnp.where(sel, -jnp.inf, work)
        if want_rank:
            rank = jnp.where(sel, float(k), rank)
        vals.append(m)
        yield
    return vals, work, rank


def _lockstep(*gens):
    results = [None] * len(gens)
    live = list(range(len(gens)))
    while live:
        for i in list(live):
            try:
                next(gens[i])
            except StopIteration as stop:
                results[i] = stop.value
                live.remove(i)
        yield
    return results


def _run_lockstep(*gens):
    stepper = _lockstep(*gens)
    while True:
        try:
            next(stepper)
        except StopIteration as stop:
            return stop.value


def _peer_kernel(x_ref, mod_ref, wq_ref, sk_ref, u_ref, vt_ref, g_ref, b_ref, o_ref,
                 ht_scr, s0_scr, e0_scr, s1_scr, r1_scr, e1_scr, a_scr, z_scr, acc_scr, *, tb):
    e_step = pl.program_id(1)
    n_lt = tb // LANES
    K = PEER_TOPK

    @pl.when(e_step == 0)
    def _route():
        h2 = x_ref[...] * (1.0 + mod_ref[4:5, :]) + mod_ref[3:4, :]
        ht = _bf(h2.T)
        ht_scr[...] = ht
        qt = _mm(wq_ref[...], ht)
        for hp in range(2 * PEER_HEADS):
            sc = _mm_split(_split(sk_ref[hp]), _split(qt[hp * PEER_HALF:(hp + 1) * PEER_HALF, :]), NN,
                           PEER_SCORE_PASSES)
            if hp % 2 == 0:
                s0_scr[hp // 2] = sc
            else:
                s1_scr[hp // 2] = sc

        iota_k = lax.broadcasted_iota(jnp.int32, (PEER_NKEYS, LANES), 0)
        iota_c = lax.broadcasted_iota(jnp.int32, ((SUBLANES + 2) * SUBLANES, LANES), 0)

        def route_tile(tile, by_value):
            hd, lanes, sc0, sc1 = tile
            (v0, w0, _), (v1, w1, rank1) = yield from _lockstep(
                _extract_top(sc0, iota_k, K, by_value=by_value),
                _extract_top(sc1, iota_k, K, want_rank=True, by_value=by_value))
            cand0 = w0 != sc0
            cand1 = w1 != sc1
            lo1 = jnp.concatenate(v1[:SUBLANES], axis=0)
            hi1 = jnp.concatenate(v1[SUBLANES:], axis=0)
            hi0 = jnp.concatenate(v0[SUBLANES:], axis=0)
            pair = jnp.concatenate([v0[a] + lo1 for a in range(SUBLANES)] + [v0[0] + hi1, hi0 + v1[0]], axis=0)
            f, wp, _ = yield from _extract_top(pair, iota_c, K + 1, by_value=by_value)
            zsum = jnp.zeros_like(f[0])
            for k in range(K):
                zsum = zsum + jnp.exp(f[k] - f[0])
            th = 0.5 * (f[K - 1] + f[K]) - sc0
            count0 = jnp.zeros_like(sc0)
            for b in range(K):
                count0 = count0 + jnp.where(v1[b] > th, 1.0, 0.0)
            s0_scr[hd, :, lanes] = jnp.where(cand0, count0, 0.0)
            e0_scr[hd, :, lanes] = jnp.where(cand0, jnp.exp(sc0 - v0[0]), 0.0) / zsum
            e1_scr[hd, :, lanes] = jnp.where(cand1, jnp.exp(sc1 - v1[0]), 0.0)
            r1_scr[hd, :, lanes] = rank1
            if not by_value:
                return None
            removed = [jnp.sum(jnp.where(c, 1.0, 0.0), axis=0, keepdims=True) for c in (cand0, cand1, wp != pair)]
            wrong = (removed[0] != K) | (removed[1] != K) | (removed[2] != K + 1)
            return jnp.max(jnp.where(wrong, 1.0, 0.0)) > 0.0

        def per_tile_group(j, carry):
            tiles = []
            for t in [PEER_ROUTE_TILES * j + s for s in range(PEER_ROUTE_TILES)]:
                hd = t // n_lt
                lanes = pl.ds(pl.multiple_of((t % n_lt) * LANES, LANES), LANES)
                tiles.append((hd, lanes, s0_scr[hd, :, lanes], s1_scr[hd, :, lanes]))
            had_ties = _run_lockstep(*[route_tile(tile, by_value=True) for tile in tiles])
            for tile, redo in zip(tiles, had_ties):
                @pl.when(redo)
                def _exact(tile=tile):
                    _run_lockstep(route_tile(tile, by_value=False))
            return carry

        lax.fori_loop(0, PEER_HEADS * n_lt // PEER_ROUTE_TILES, per_tile_group, 0)
        acc_scr[...] = jnp.zeros_like(acc_scr)

    et = SUBLANES * PEER_NKEYS
    group = SUBLANES // 2
    jblk = PEER_GATE_ROWS
    tile3 = (jblk // SUBLANES, SUBLANES, LANES)
    for sub in range(PEER_GROUPS_PER_STEP):
        a_scr[...] = _gelu_tanh(_mm(u_ref[sub * et:(sub + 1) * et, :], ht_scr[...]))
        grp = e_step * PEER_GROUPS_PER_STEP + sub
        keys = pl.ds(pl.multiple_of(grp * SUBLANES, SUBLANES), SUBLANES)
        for ln in range(n_lt):
            lanes = slice(ln * LANES, (ln + 1) * LANES)
            c0 = [s0_scr[hd, keys, lanes] for hd in range(PEER_HEADS)]
            e0 = [e0_scr[hd, keys, lanes] for hd in range(PEER_HEADS)]
            for jb in range(PEER_NKEYS // jblk):
                jrows = slice(jb * jblk, (jb + 1) * jblk)
                for ig in range(SUBLANES // group):
                    g = [jnp.zeros(tile3, jnp.float32) for _ in range(group)]
                    for hd in range(PEER_HEADS):
                        r1 = r1_scr[hd, jrows, lanes].reshape(tile3)
                        e1 = e1_scr[hd, jrows, lanes].reshape(tile3)
                        for k in range(group):
                            ii = ig * group + k
                            cnt = jnp.broadcast_to(c0[hd][ii:ii + 1, :], tile3[1:])[None]
                            w0 = jnp.broadcast_to(e0[hd][ii:ii + 1, :], tile3[1:])[None]
                            g[k] = g[k] + jnp.where(r1 < cnt, e1, 0.0) * w0
                    for k in range(group):
                        r0 = (ig * group + k) * PEER_NKEYS + jb * jblk
                        z_scr[r0:r0 + jblk, lanes] = _bf(g[k].reshape(jblk, LANES) * a_scr[r0:r0 + jblk, lanes])
        acc_scr[...] += _mm(vt_ref[sub], z_scr[...])

    @pl.when(e_step == pl.num_programs(1) - 1)
    def _finish():
        out = acc_scr[...].T
        o_ref[...] = _layer_norm(ALPHA * x_ref[...] + mod_ref[5:6, :] * out, g_ref[...], b_ref[...], LN_EPS)


def _peer_call(x, mod, wq_t, subkeys, u_b, vt_b, ln_g, ln_b, tb, n_ctx_rows, lat_seq):
    rows, d = x.shape
    et = SUBLANES * PEER_NKEYS
    gps = PEER_GROUPS_PER_STEP
    ncb = n_ctx_rows // tb
    bps = lat_seq // tb
    nq = wq_t.shape[0]
    f32 = jnp.float32
    return pl.pallas_call(
        functools.partial(_peer_kernel, tb=tb),
        grid=(rows // tb, u_b.shape[0] // (gps * et)),
        in_specs=[
            pl.BlockSpec((tb, d), lambda i, e: (i, 0)),
            pl.BlockSpec((None, 6, d), lambda i, e: (_group_of_block(i, ncb, bps), 0, 0)),
            pl.BlockSpec((nq, d), lambda i, e: (0, 0), pipeline_mode=pl.Buffered(1)),
            pl.BlockSpec(subkeys.shape, lambda i, e: (0, 0, 0), pipeline_mode=pl.Buffered(1)),
            pl.BlockSpec((gps * et, d), lambda i, e: (e, 0)),
            pl.BlockSpec((gps, d, et), lambda i, e: (e, 0, 0)),
            pl.BlockSpec((1, d), lambda i, e: (0, 0)),
            pl.BlockSpec((1, d), lambda i, e: (0, 0)),
        ],
        out_specs=pl.BlockSpec((tb, d), lambda i, e: (i, 0)),
        out_shape=jax.ShapeDtypeStruct((rows, d), f32),
        scratch_shapes=[
            pltpu.VMEM((d, tb), jnp.bfloat16),
            pltpu.VMEM((PEER_HEADS, PEER_NKEYS, tb), f32),
            pltpu.VMEM((PEER_HEADS, PEER_NKEYS, tb), f32),
            pltpu.VMEM((PEER_HEADS, PEER_NKEYS, tb), f32),
            pltpu.VMEM((PEER_HEADS, PEER_NKEYS, tb), f32),
            pltpu.VMEM((PEER_HEADS, PEER_NKEYS, tb), f32),
            pltpu.VMEM((et, tb), f32),
            pltpu.VMEM((et, tb), jnp.bfloat16),
            pltpu.VMEM((d, tb), f32),
        ],
        compiler_params=_params("parallel", "arbitrary"),
        name="peer",
    )(x, mod, wq_t, subkeys, u_b, vt_b, ln_g.reshape(1, d), ln_b.reshape(1, d))


def _rope_tables(seq, tb):
    half = HEAD_DIM // 4
    freqs = ROPE_THETA ** (-jnp.arange(half, dtype=jnp.float32) / half)
    t = jnp.arange(seq)
    cos_parts, sin_parts = [], []
    for pos in (t // GRID_W, t % GRID_W):
        ang = pos.astype(jnp.float32)[:, None] * freqs[None, :]
        c, s = jnp.cos(ang), jnp.sin(ang)
        cos_parts += [c, c]
        sin_parts += [-s, s]
    cos_h = jnp.concatenate(cos_parts, axis=1)
    sin_h = jnp.concatenate(sin_parts, axis=1)
    n_rot = QK_W // HEAD_DIM
    cos_t = jnp.concatenate([jnp.tile(cos_h, (1, n_rot)), jnp.ones((tb, QK_W), jnp.float32)], axis=0)
    sin_t = jnp.concatenate([jnp.tile(sin_h, (1, n_rot)), jnp.zeros((tb, QK_W), jnp.float32)], axis=0)
    return cos_t, sin_t


def _pad_w_in(w):
    split = w.shape[1] - CV_W
    z = jnp.zeros((w.shape[0], IN_PAD - w.shape[1]), w.dtype)
    return jnp.concatenate([w[:, :split], z, w[:, split:]], axis=1)


def _forward(x_prompt, x_sample, cache_k, cache_v, state_rwkv, c, c_ctx, ln_in_g, ln_in_b, w_ada, b_ada,
             w_in, w_out, att_sink, rw_w0, rw_w_up, rw_a0, rw_a_up, rw_g_up, rw_kk, rw_ka, rw_rk,
             rw_lnx_g, rw_lnx_b, conv_w, ln1_g, ln1_b, ln2_g, ln2_b, peer_wq, peer_subkeys, peer_u, peer_v,
             tb_in=512, tb_out=256, tb_peer=512, rw_par=2):
    nb_c, seq_c, d = x_prompt.shape
    nb_l, seq_l, _ = x_sample.shape
    depth = w_in.shape[0]
    n_ctx = nb_c * seq_c
    past = cache_k.shape[2]
    W = RWKV_DIM

    x = jnp.concatenate([x_prompt.reshape(n_ctx, d), x_sample.reshape(nb_l * seq_l, d)], axis=0)
    x = _ln_call(x, ln_in_g, ln_in_b, tb_in)

    n_groups = 1 + nb_l
    g_pad = -(-n_groups // SUBLANES) * SUBLANES
    cvec = jnp.concatenate([c_ctx[None], c, jnp.zeros((g_pad - n_groups, d), jnp.float32)], axis=0)
    mod_all = _mod_call(cvec, w_ada, b_ada).reshape(depth, g_pad, 6, d)

    cos_t, sin_t = _rope_tables(seq_l, tb_in)
    consts = tuple(jnp.asarray(a) for a in _rwkv_consts())
    ck_all = cache_k.reshape(nb_l, depth, past, KV_DIM)
    cv_all = cache_v.reshape(nb_l, depth, past, KV_DIM)
    eye_h = jnp.eye(RWKV_HEADS, dtype=jnp.float32)
    s0_bd_all = jnp.einsum('bldhij,hg->bldhigj', state_rwkv, eye_h).reshape(nb_l, depth, 2, W, W)
    s0_zero = jnp.zeros((nb_c, 2, W, W), jnp.float32)

    ks, vs, sts = [], [], []
    for l in range(depth):
        mod = mod_all[l]
        p_att, p_rw, p_cv = _in_proj_call(x, mod, _bf(_pad_w_in(w_in[l])), cos_t, sin_t, tb_in, n_ctx, seq_l)
        att_c = _ctx_attn_call(p_att, att_sink[l], nb_c, seq_c)
        att_l = _lat_attn_call(p_att, ck_all, cv_all, att_sink[l], l, nb_l, seq_l, n_ctx)
        wts = _rwkv_weights(rw_w0[l], rw_w_up[l], rw_a0[l], rw_a_up[l], rw_g_up[l], rw_kk[l], rw_ka[l],
                            rw_rk[l], rw_lnx_g[l], rw_lnx_b[l])
        rw_c, sfin = _rwkv_call(p_rw, s0_zero, consts, wts, nb_c, seq_c, 0, rw_par, False)
        rw_l, _ = _rwkv_call(p_rw, s0_bd_all[:, l], consts, wts, nb_l, seq_l, n_ctx, rw_par, True)
        x1 = _out_proj_call(att_c, att_l, rw_c, rw_l, p_cv, x, mod, _bf(w_out[l]), conv_w[l],
                            ln1_g[l], ln1_b[l], tb_out, n_ctx, seq_c, seq_l)
        sk = peer_subkeys[l].reshape(2 * PEER_HEADS, PEER_NKEYS, PEER_HALF)
        vt = _bf(peer_v[l]).reshape(-1, SUBLANES * PEER_NKEYS, d).transpose(0, 2, 1)
        x = _peer_call(x1, mod, _bf(peer_wq[l].T), sk, _bf(peer_u[l]), vt,
                       ln2_g[l], ln2_b[l], tb_peer, n_ctx, seq_l)
        ks.append(p_att[:n_ctx, ATT_DIM:ATT_DIM + KV_DIM].reshape(nb_c, seq_c, ATT_KV_HEADS, HEAD_DIM))
        vs.append(p_att[:n_ctx, ATT_DIM + KV_DIM:].reshape(nb_c, seq_c, ATT_KV_HEADS, HEAD_DIM))
        s5 = sfin.reshape(nb_c, 2, RWKV_HEADS, HEAD_DIM, RWKV_HEADS, HEAD_DIM)
        sts.append(jnp.stack([s5[:, :, h, :, h, :] for h in range(RWKV_HEADS)], axis=2))

    y_prompt = x[:n_ctx].reshape(nb_c, seq_c, d)
    y_sample = x[n_ctx:].reshape(nb_l, seq_l, d)
    return (y_prompt, y_sample, jnp.stack(ks, axis=1), jnp.stack(vs, axis=1), jnp.stack(sts, axis=1))


def kernel(x_prompt, x_sample, cache_k, cache_v, state_rwkv, c, c_ctx, ln_in_g, ln_in_b, w_ada, b_ada, w_in, w_out, att_sink, rw_w0, rw_w_up, rw_a0, rw_a_up, rw_g_up, rw_kk, rw_ka, rw_rk, rw_lnx_g, rw_lnx_b, conv_w, ln1_g, ln1_b, ln2_g, ln2_b, peer_wq, peer_subkeys, peer_u, peer_v):
    return _forward(x_prompt, x_sample, cache_k, cache_v, state_rwkv, c, c_ctx, ln_in_g, ln_in_b, w_ada, b_ada,
                    w_in, w_out, att_sink, rw_w0, rw_w_up, rw_a0, rw_a_up, rw_g_up, rw_kk, rw_ka, rw_rk,
                    rw_lnx_g, rw_lnx_b, conv_w, ln1_g, ln1_b, ln2_g, ln2_b, peer_wq, peer_subkeys, peer_u, peer_v)
```
